```python
import math
import jax
import jax.numpy as jnp
from jax import lax
import numpy as np

D_MODEL = 1024
BATCH = 4
SEQ = 8192
DEPTH = 4

GRID_W = 64
CTX_LEN = 256
N_MIXERS = 3
MLP_HIDDEN = 4 * D_MODEL
NORM_EPS = 1e-6

HY_ORDER = 2
HY_EMB = 33
HY_FILTER_WIDTH = 64
HY_SHORT = 3
HY_FAST_DECAY = 0.3
HY_SLOW_DECAY = 1.5
HY_TARGET = 1e-2

GLA_HEADS = 4
GLA_DK = D_MODEL // 2
GLA_DV = D_MODEL
GLA_HK = GLA_DK // GLA_HEADS
GLA_HV = GLA_DV // GLA_HEADS
GLA_GATE_RANK = 16
GLA_GATE_NORM = 16.0
GLA_CHUNK = 64

MLA_HEADS = 8
MLA_Q_RANK = 384
MLA_KV_RANK = 256
MLA_NOPE = 128
MLA_ROPE = 64
MLA_V = 128
ROPE_THETA = 10000.0
Q_BLOCK = 128

N_HYENA = (DEPTH + 2) // 3
N_GLA = (DEPTH + 1) // 3
N_MLA = DEPTH // 3

kernel_name = 'hybrid_hyena_gla_mla_diffusion_trunk'

F32 = jnp.float32


def rmsnorm(x, g):
    xf = x.astype(F32)
    y = xf * lax.rsqrt(jnp.mean(xf * xf, axis=-1, keepdims=True) + NORM_EPS)
    return (y * g.astype(F32)).astype(x.dtype)


def modulate(h, shift, scale):
    return h * (1 + scale) + shift


def sq_relu_mlp(h, w1, w2):
    return jnp.square(jax.nn.relu(h @ w1)) @ w2


def short_conv(u, w, b):
    L = u.shape[1]
    pad = HY_SHORT // 2
    up = jnp.pad(u, ((0, 0), (pad, HY_SHORT - 1 - pad), (0, 0)))
    out = up[:, 0:L] * w[0]
    for j in range(1, HY_SHORT):
        out = out + up[:, j:j + L] * w[j]
    return out + b


def hyena_filter_spectra(L, f_w1, f_b1, f_w2, f_b2, f_w3, f_b3, f_w4, freq):
    t = jnp.linspace(0.0, 1.0, L, dtype=F32)[:, None]
    bands = (HY_EMB - 1) // 2
    w = 2.0 * math.pi * jnp.arange(L, dtype=F32)[:, None] / L
    f = jnp.linspace(1e-4, bands - 1, bands, dtype=F32)[None, :]
    z = jnp.concatenate([t, jnp.cos(f * w), -jnp.sin(f * w)], axis=-1)
    fr = freq.astype(F32)
    h = jnp.sin(fr * (z @ f_w1.astype(F32) + f_b1.astype(F32)))
    h = jnp.sin(fr * (h @ f_w2.astype(F32) + f_b2.astype(F32)))
    h = jnp.sin(fr * (h @ f_w3.astype(F32) + f_b3.astype(F32)))
    h = (h @ f_w4.astype(F32)).reshape(L, HY_ORDER, 2, D_MODEL)
    max_decay = math.log(HY_TARGET) / HY_FAST_DECAY
    min_decay = math.log(HY_TARGET) / HY_SLOW_DECAY
    deltas = jnp.linspace(min_decay, max_decay, D_MODEL, dtype=F32)
    h = h * jnp.exp(-t * jnp.abs(deltas))[:, None, None, :]
    fwd, bwd = h[:, :, 0], h[:, :, 1]
    two_sided = jnp.concatenate(
        [fwd, jnp.zeros((1, HY_ORDER, D_MODEL), F32), jnp.flip(bwd[1:], axis=0)], axis=0)
    two_sided = two_sided / jnp.sum(jnp.abs(two_sided), axis=0, keepdims=True)
    return jnp.fft.rfft(two_sided, axis=0)


def fft_long_conv(u, spec, skip):
    L = u.shape[1]
    y = jnp.fft.irfft(jnp.fft.rfft(u, n=2 * L, axis=1) * spec[None], n=2 * L, axis=1)[:, :L]
    return y + u * skip


def hyena_mixer(u, w_in, b_in, conv_w, conv_b, f_w1, f_b1, f_w2, f_b2, f_w3, f_b3, f_w4,
                freq, bias, w_out, b_out):
    L = u.shape[1]
    z = short_conv(u @ w_in + b_in, conv_w, conv_b).astype(F32)
    parts = jnp.split(z, HY_ORDER + 1, axis=-1)
    y, gates = parts[0], parts[1:]
    spec = hyena_filter_spectra(L, f_w1, f_b1, f_w2, f_b2, f_w3, f_b3, f_w4, freq)
    skip = bias.astype(F32)
    for n in range(HY_ORDER):
        y = gates[n] * fft_long_conv(y, spec[:, n], skip[n])
    return y.astype(u.dtype) @ w_out + b_out


def gla_inputs(u, w_in, gk_w2, gk_b):
    B, L, _ = u.shape
    z = u @ w_in
    q, k, v, og, r = jnp.split(
        z, [GLA_DK, 2 * GLA_DK, 2 * GLA_DK + GLA_DV, 2 * GLA_DK + 2 * GLA_DV], axis=-1)
    r = r.reshape(B, L, 2, GLA_GATE_RANK)
    gk = jnp.einsum('blzr,zrd->blzd', r, gk_w2) + gk_b
    g = (jax.nn.log_sigmoid(gk.astype(F32)) / GLA_GATE_NORM).reshape(B, L, 2, GLA_HEADS, GLA_HK)
    q = q.astype(F32).reshape(B, L, GLA_HEADS, GLA_HK) * (GLA_HK ** -0.5)
    k = k.astype(F32).reshape(B, L, GLA_HEADS, GLA_HK)
    v = v.astype(F32).reshape(B, L, GLA_HEADS, GLA_HV)
    return q, k, v, g[:, :, 0], g[:, :, 1], og


def gla_chunk_scan(q, k, v, g, s0):
    B, L, H, dk = q.shape
    dv = v.shape[-1]
    n_chunks = L // GLA_CHUNK

    def to_chunks(a):
        return a.reshape(B, n_chunks, GLA_CHUNK, H, a.shape[-1]).transpose(1, 0, 3, 2, 4)

    q, k, v, g = to_chunks(q), to_chunks(k), to_chunks(v), to_chunks(g)
    b = jnp.cumsum(g, axis=3)
    b_last = b[..., -1:, :]
    q_t = q * jnp.exp(b)
    k_t = k * jnp.exp(-b)
    k_end = k * jnp.exp(b_last - b)
    mask = jnp.tril(jnp.ones((GLA_CHUNK, GLA_CHUNK), dtype=bool))
    att = jnp.where(mask, jnp.einsum('nbhcd,nbhsd->nbhcs', q_t, k_t), 0.0)
    o_intra = jnp.einsum('nbhcs,nbhsv->nbhcv', att, v)
    chunk_decay = jnp.exp(b_last[..., 0, :])

    def step(S, inp):
        qn, kn, vn, dn = inp
        o = jnp.einsum('bhcd,bhdv->bhcv', qn, S)
        S = S * dn[..., None] + jnp.einsum('bhcd,bhcv->bhdv', kn, vn)
        return S, o

    s_fin, o_inter = lax.scan(step, s0, (q_t, k_end, v, chunk_decay))
    o = (o_intra + o_inter).transpose(1, 0, 3, 2, 4).reshape(B, L, H, dv)
    return o, s_fin


def gla_direction(q, k, v, g, s0, reverse):
    if reverse:
        q, k, v, g = (jnp.flip(a, axis=1) for a in (q, k, v, g))
    o, s_fin = gla_chunk_scan(q, k, v, g, s0)
    if reverse:
        o = jnp.flip(o, axis=1)
    return o, s_fin


def gla_out(o, og, onorm, wo):
    B, L = o.shape[:2]
    o = rmsnorm(o, onorm).reshape(B, L, GLA_DV).astype(og.dtype)
    return (o * jax.nn.silu(og)) @ wo


def gla_mixer(h_lat, h_ctx, w_in, gk_w2, gk_b, onorm, wo, ctx_out):
    ql, kl, vl, gfl, gbl, ogl = gla_inputs(h_lat, w_in, gk_w2, gk_b)
    qc, kc, vc, gfc, gbc, ogc = gla_inputs(h_ctx, w_in, gk_w2, gk_b)
    s0 = jnp.zeros((h_lat.shape[0], GLA_HEADS, GLA_HK, GLA_HV), F32)
    oc_f, s_f = gla_direction(qc, kc, vc, gfc, s0, False)
    oc_b, s_b = gla_direction(qc, kc, vc, gbc, s0, True)
    ol_f, _ = gla_direction(ql, kl, vl, gfl, s_f, False)
    ol_b, _ = gla_direction(ql, kl, vl, gbl, s_b, True)
    y_lat = gla_out(ol_f + ol_b, ogl, onorm, wo)
    y_ctx = gla_out(oc_f + oc_b, ogc, onorm, wo) if ctx_out else None
    return y_lat, y_ctx


def rotate_half_axis(xh, ang):
    cos = jnp.cos(ang)[None, :, None, :]
    sin = jnp.sin(ang)[None, :, None, :]
    x1, x2 = jnp.split(xh, 2, axis=-1)
    return jnp.concatenate([x1 * cos - x2 * sin, x1 * sin + x2 * cos], axis=-1)


def axial_rope(x, ang_row, ang_col):
    xf = x.astype(F32)
    half = x.shape[-1] // 2
    out = jnp.concatenate([rotate_half_axis(xf[..., :half], ang_row),
                           rotate_half_axis(xf[..., half:], ang_col)], axis=-1)
    return out.astype(x.dtype)


def mla_qkv(u, angles, w_down, qnorm, w_uq, kvnorm, w_ukv):
    B, L, _ = u.shape
    cq, ckv, k_rope = jnp.split(u @ w_down, [MLA_Q_RANK, MLA_Q_RANK + MLA_KV_RANK], axis=-1)
    q = (rmsnorm(cq, qnorm) @ w_uq).reshape(B, L, MLA_HEADS, MLA_NOPE + MLA_ROPE)
    kv = (rmsnorm(ckv, kvnorm) @ w_ukv).reshape(B, L, MLA_HEADS, MLA_NOPE + MLA_V)
    q_nope, q_rope = q[..., :MLA_NOPE], q[..., MLA_NOPE:]
    k_nope, v = kv[..., :MLA_NOPE], kv[..., MLA_NOPE:]
    k_rope = k_rope[:, :, None, :]
    if angles is not None:
        q_rope = axial_rope(q_rope, angles[0], angles[1])
        k_rope = axial_rope(k_rope, angles[0], angles[1])
    q = jnp.concatenate([q_nope, q_rope], axis=-1)
    k = jnp.concatenate([k_nope, jnp.broadcast_to(k_rope, (B, L, MLA_HEADS, MLA_ROPE))], axis=-1)
    return q, k, v


def block_attention(q, k, v):
    B, S, H, dq = q.shape
    scale = (MLA_NOPE + MLA_ROPE) ** -0.5
    n_blocks = S // Q_BLOCK
    qb = q.reshape(B, n_blocks, Q_BLOCK, H, dq).transpose(1, 0, 2, 3, 4)

    def one_block(qi):
        s = jnp.einsum('bqhd,bkhd->bhqk', qi, k, preferred_element_type=F32) * scale
        p = jax.nn.softmax(s, axis=-1)
        return jnp.einsum('bhqk,bkhv->bqhv', p.astype(v.dtype), v)

    o = lax.map(one_block, qb)
    return o.transpose(1, 0, 2, 3, 4).reshape(B, S, H * v.shape[-1])


def mla_mixer(h_lat, h_ctx, w_down, qnorm, w_uq, kvnorm, w_ukv, wo, ctx_out):
    L = h_lat.shape[1]
    rows = L // GRID_W
    r_idx, c_idx = jnp.meshgrid(jnp.arange(rows), jnp.arange(GRID_W), indexing='ij')
    half = MLA_ROPE // 2
    inv_freq = ROPE_THETA ** (-jnp.arange(0, half, 2, dtype=F32) / half)
    ang_row = r_idx.reshape(-1).astype(F32)[:, None] * inv_freq[None, :]
    ang_col = c_idx.reshape(-1).astype(F32)[:, None] * inv_freq[None, :]
    ql, kl, vl = mla_qkv(h_lat, (ang_row, ang_col), w_down, qnorm, w_uq, kvnorm, w_ukv)
    qc, kc, vc = mla_qkv(h_ctx, None, w_down, qnorm, w_uq, kvnorm, w_ukv)
    k_all = jnp.concatenate([kc, kl], axis=1)
    v_all = jnp.concatenate([vc, vl], axis=1)
    y_lat = block_attention(ql, k_all, v_all) @ wo
    y_ctx = block_attention(qc, kc, vc) @ wo if ctx_out else None
    return y_lat, y_ctx


def setup_inputs(seed: int = 0) -> dict:
    key = jax.random.key(seed)
    ks = iter(jax.random.split(key, 64))
    D = D_MODEL

    def nrm(shape, scale):
        return jax.random.normal(next(ks), shape, F32) * scale

    def gain(shape):
        return 1.0 + nrm(shape, 0.05)

    gla_in_cols = 2 * GLA_DK + 2 * GLA_DV + 2 * GLA_GATE_RANK
    fw = HY_FILTER_WIDTH
    return {
        'x': nrm((BATCH, SEQ, D), 1.0),
        'c': nrm((BATCH, D), 1.0),
        'ctx': nrm((BATCH, CTX_LEN, D), 1.0),
        'c_ctx': nrm((D,), 1.0),
        'ada_w': nrm((DEPTH, D, 6 * D), 0.5 * D ** -0.5),
        'ada_b': nrm((DEPTH, 6 * D), 0.01),
        'norm1_g': gain((DEPTH, D)),
        'norm2_g': gain((DEPTH, D)),
        'mlp_w1': nrm((DEPTH, D, MLP_HIDDEN), D ** -0.5),
        'mlp_w2': nrm((DEPTH, MLP_HIDDEN, D), MLP_HIDDEN ** -0.5),
        'final_g': gain((D,)),
        'hy_w_in': nrm((N_HYENA, D, (HY_ORDER + 1) * D), D ** -0.5),
        'hy_b_in': nrm((N_HYENA, (HY_ORDER + 1) * D), 0.01),
        'hy_conv_w': nrm((N_HYENA, HY_SHORT, (HY_ORDER + 1) * D), HY_SHORT ** -0.5),
        'hy_conv_b': nrm((N_HYENA, (HY_ORDER + 1) * D), 0.01),
        'hy_f_w1': nrm((N_HYENA, HY_EMB, fw), HY_EMB ** -0.5),
        'hy_f_b1': nrm((N_HYENA, fw), 0.1),
        'hy_f_w2': nrm((N_HYENA, fw, fw), fw ** -0.5),
        'hy_f_b2': nrm((N_HYENA, fw), 0.1),
        'hy_f_w3': nrm((N_HYENA, fw, fw), fw ** -0.5),
        'hy_f_b3': nrm((N_HYENA, fw), 0.1),
        'hy_f_w4': nrm((N_HYENA, fw, HY_ORDER * 2 * D), fw ** -0.5),
        'hy_freq': 1.0 + nrm((N_HYENA, fw), 0.1),
        'hy_bias': nrm((N_HYENA, HY_ORDER, D), 0.5),
        'hy_w_out': nrm((N_HYENA, D, D), D ** -0.5),
        'hy_b_out': nrm((N_HYENA, D), 0.01),
        'gla_w_in': nrm((N_GLA, D, gla_in_cols), D ** -0.5),
        'gla_gk_w2': nrm((N_GLA, 2, GLA_GATE_RANK, GLA_DK), GLA_GATE_RANK ** -0.5),
        'gla_gk_b': nrm((N_GLA, 2, GLA_DK), 0.1),
        'gla_onorm': gain((N_GLA, GLA_HV)),
        'gla_wo': nrm((N_GLA, GLA_DV, D), GLA_DV ** -0.5),
        'mla_w_down': nrm((N_MLA, D, MLA_Q_RANK + MLA_KV_RANK + MLA_ROPE), D ** -0.5),
        'mla_qnorm': gain((N_MLA, MLA_Q_RANK)),
        'mla_w_uq': nrm((N_MLA, MLA_Q_RANK, MLA_HEADS * (MLA_NOPE + MLA_ROPE)), MLA_Q_RANK ** -0.5),
        'mla_kvnorm': gain((N_MLA, MLA_KV_RANK)),
        'mla_w_ukv': nrm((N_MLA, MLA_KV_RANK, MLA_HEADS * (MLA_NOPE + MLA_V)), MLA_KV_RANK ** -0.5),
        'mla_wo': nrm((N_MLA, MLA_HEADS * MLA_V, D), (MLA_HEADS * MLA_V) ** -0.5),
    }


def reference(x, c, ctx, c_ctx, ada_w, ada_b, norm1_g, norm2_g, mlp_w1, mlp_w2, final_g,
              hy_w_in, hy_b_in, hy_conv_w, hy_conv_b, hy_f_w1, hy_f_b1, hy_f_w2, hy_f_b2,
              hy_f_w3, hy_f_b3, hy_f_w4, hy_freq, hy_bias, hy_w_out, hy_b_out,
              gla_w_in, gla_gk_w2, gla_gk_b, gla_onorm, gla_wo,
              mla_w_down, mla_qnorm, mla_w_uq, mla_kvnorm, mla_w_ukv, mla_wo):
    x_lat = x
    x_ctx = ctx
    silu_c = jax.nn.silu(c)
    silu_cc = jax.nn.silu(c_ctx)
    for i in range(DEPTH):
        kind = i % N_MIXERS
        j = i // N_MIXERS
        ctx_read = kind != 0
        ctx_live = any(l % N_MIXERS != 0 for l in range(i + 1, DEPTH))
        mod_l = jnp.split((silu_c @ ada_w[i] + ada_b[i])[:, None, :], 6, axis=-1)
        h_lat = modulate(rmsnorm(x_lat, norm1_g[i]), mod_l[0], mod_l[1])
        if ctx_read or ctx_live:
            mod_c = jnp.split(silu_cc @ ada_w[i] + ada_b[i], 6, axis=-1)
            h_ctx = modulate(rmsnorm(x_ctx, norm1_g[i]), mod_c[0], mod_c[1])
        if kind == 0:
            hp = (hy_w_in[j], hy_b_in[j], hy_conv_w[j], hy_conv_b[j], hy_f_w1[j], hy_f_b1[j],
                  hy_f_w2[j], hy_f_b2[j], hy_f_w3[j], hy_f_b3[j], hy_f_w4[j], hy_freq[j],
                  hy_bias[j], hy_w_out[j], hy_b_out[j])
            y_lat = hyena_mixer(h_lat, *hp)
            y_ctx = hyena_mixer(h_ctx, *hp) if ctx_live else None
        elif kind == 1:
            y_lat, y_ctx = gla_mixer(h_lat, h_ctx, gla_w_in[j], gla_gk_w2[j], gla_gk_b[j],
                                     gla_onorm[j], gla_wo[j], ctx_live)
        else:
            y_lat, y_ctx = mla_mixer(h_lat, h_ctx, mla_w_down[j], mla_qnorm[j], mla_w_uq[j],
                                     mla_kvnorm[j], mla_w_ukv[j], mla_wo[j], ctx_live)
        x_lat = x_lat + mod_l[2] * y_lat
        x_lat = x_lat + mod_l[5] * sq_relu_mlp(
            modulate(rmsnorm(x_lat, norm2_g[i]), mod_l[3], mod_l[4]), mlp_w1[i], mlp_w2[i])
        if ctx_live:
            x_ctx = x_ctx + mod_c[2] * y_ctx
            x_ctx = x_ctx + mod_c[5] * sq_relu_mlp(
                modulate(rmsnorm(x_ctx, norm2_g[i]), mod_c[3], mod_c[4]), mlp_w1[i], mlp_w2[i])
    return rmsnorm(x_lat, final_g)
```

```python
import functools
import math

import jax
import jax.numpy as jnp
import numpy as np
from jax import lax
from jax.experimental import pallas as pl
from jax.experimental.pallas import tpu as pltpu

F32 = jnp.float32
BF16 = jnp.bfloat16

D_MODEL = 1024
DEPTH = 4
GRID_W = 64
N_MIXERS = 3
NORM_EPS = 1e-6

HY_ORDER = 2
HY_EMB = 33
HY_SHORT = 3
HY_FAST_DECAY = 0.3
HY_SLOW_DECAY = 1.5
HY_TARGET = 1e-2

GLA_HEADS = 4
GLA_DK = D_MODEL // 2
GLA_DV = D_MODEL
GLA_HK = GLA_DK // GLA_HEADS
GLA_HV = GLA_DV // GLA_HEADS
GLA_GATE_RANK = 16
GLA_GATE_NORM = 16.0
GLA_CHUNK = 64

MLA_HEADS = 8
MLA_Q_RANK = 384
MLA_KV_RANK = 256
MLA_NOPE = 128
MLA_ROPE = 64
MLA_V = 128
ROPE_THETA = 10000.0
Q_BLOCK = 128

VMEM_LIMIT_BYTES = 56 * 1024 * 1024
MLA_QK_PAD = 256


def _const_spec(shape):
    nd = len(shape)
    return pl.BlockSpec(shape, lambda *_: (0,) * nd, pipeline_mode=pl.Buffered(1))


def _norm_mod(x, g, shift, scale):
    y = x * lax.rsqrt(jnp.mean(x * x, axis=-1, keepdims=True) + NORM_EPS)
    return (y * g) * (1.0 + scale) + shift


def _mlp_kernel(x_ref, g_ref, mod_ref, w1_ref, w2_ref, fg_ref, o_ref, *, hidden_chunk, final_norm):
    x = x_ref[0]
    h = _norm_mod(x, g_ref[...], mod_ref[0, 3:4, :], mod_ref[0, 4:5, :]).astype(BF16)
    hidden = w1_ref.shape[1]
    acc = jnp.zeros(x.shape, F32)
    for c0 in range(0, hidden, hidden_chunk):
        a = jnp.dot(h, w1_ref[:, c0:c0 + hidden_chunk], preferred_element_type=F32)
        a = jnp.square(jnp.maximum(a, 0.0)).astype(BF16)
        acc = acc + jnp.dot(a, w2_ref[c0:c0 + hidden_chunk, :], preferred_element_type=F32)
    out = x + mod_ref[0, 5:6, :] * acc
    if final_norm:
        out = (out * lax.rsqrt(jnp.mean(out * out, axis=-1, keepdims=True) + NORM_EPS)) * fg_ref[...]
    o_ref[0] = out


def mlp_block(x, norm_g, mod, w1, w2, final_g=None):
    B, L, D = x.shape
    tm = min(512, L)
    per_batch = mod.shape[0] != 1
    final_norm = final_g is not None
    fg = (final_g if final_norm else norm_g).reshape(1, D)
    kern = functools.partial(_mlp_kernel, hidden_chunk=1024, final_norm=final_norm)
    return pl.pallas_call(
        kern,
        grid=(B, L // tm),
        in_specs=[
            pl.BlockSpec((1, tm, D), lambda b, i: (b, i, 0)),
            _const_spec((1, D)),
            pl.BlockSpec((1, 6, D), (lambda b, i: (b, 0, 0)) if per_batch else (lambda b, i: (0, 0, 0))),
            _const_spec(w1.shape),
            _const_spec(w2.shape),
            _const_spec((1, D)),
        ],
        out_specs=pl.BlockSpec((1, tm, D), lambda b, i: (b, i, 0)),
        out_shape=jax.ShapeDtypeStruct((B, L, D), F32),
        compiler_params=pltpu.CompilerParams(
            dimension_semantics=("arbitrary", "arbitrary"), vmem_limit_bytes=VMEM_LIMIT_BYTES),
        name="mlp_block",
    )(x, norm_g.reshape(1, D), mod, w1, w2, fg)


def _attn_kernel(q_ref, kc_ref, vc_ref, k_ref, v_ref, o_ref, m_ref, l_ref, acc_ref, *, tk, scale):
    q = q_ref[0]
    m_ref[...] = jnp.full(m_ref.shape, -jnp.inf, F32)
    l_ref[...] = jnp.zeros(l_ref.shape, F32)
    acc_ref[...] = jnp.zeros(acc_ref.shape, F32)

    def step(k, v):
        s = lax.dot_general(q, k, (((1,), (1,)), ((), ())), preferred_element_type=F32) * scale
        m_prev = m_ref[...]
        m_new = jnp.maximum(m_prev, jnp.max(s, axis=-1, keepdims=True))
        alpha = jnp.exp(m_prev - m_new)
        p = jnp.exp(s - m_new)
        l_ref[...] = alpha * l_ref[...] + jnp.sum(p, axis=-1, keepdims=True)
        acc_ref[...] = alpha * acc_ref[...] + jnp.dot(p.astype(BF16), v, preferred_element_type=F32)
        m_ref[...] = m_new

    step(kc_ref[0], vc_ref[0])

    def body(j, carry):
        r0 = pl.multiple_of(j * tk, tk)
        step(k_ref[0, pl.ds(r0, tk), :], v_ref[0, pl.ds(r0, tk), :])
        return carry

    lax.fori_loop(0, k_ref.shape[1] // tk, body, 0)
    o_ref[0] = (acc_ref[...] / l_ref[...]).astype(o_ref.dtype)


def mla_attention(q, kc, vc, k, v):
    B, L, _ = q.shape
    C = kc.shape[1]
    H = MLA_HEADS
    tq, tk = 512, 512
    kern = functools.partial(_attn_kernel, tk=tk, scale=(MLA_NOPE + MLA_ROPE) ** -0.5)
    return pl.pallas_call(
        kern,
        grid=(B, H, L // tq),
        in_specs=[
            pl.BlockSpec((1, tq, MLA_QK_PAD), lambda b, h, i: (b, i, h)),
            pl.BlockSpec((1, C, MLA_QK_PAD), lambda b, h, i: (b, 0, h)),
            pl.BlockSpec((1, C, MLA_V), lambda b, h, i: (b, 0, h)),
            pl.BlockSpec((1, L, MLA_QK_PAD), lambda b, h, i: (b, 0, h)),
            pl.BlockSpec((1, L, MLA_V), lambda b, h, i: (b, 0, h)),
        ],
        out_specs=pl.BlockSpec((1, tq, MLA_V), lambda b, h, i: (b, i, h)),
        out_shape=jax.ShapeDtypeStruct((B, L, H * MLA_V), BF16),
        scratch_shapes=[pltpu.VMEM((tq, 1), F32), pltpu.VMEM((tq, 1), F32), pltpu.VMEM((tq, MLA_V), F32)],
        compiler_params=pltpu.CompilerParams(
            dimension_semantics=("arbitrary", "arbitrary", "arbitrary"), vmem_limit_bytes=VMEM_LIMIT_BYTES),
        name="mla_attention",
    )(q, kc, vc, k, v)


def rmsnorm(x, g):
    xf = x.astype(F32)
    y = xf * lax.rsqrt(jnp.mean(xf * xf, axis=-1, keepdims=True) + NORM_EPS)
    return (y * g.astype(F32)).astype(x.dtype)


def modulate(h, shift, scale):
    return h * (1 + scale) + shift


def short_conv(u, w, b):
    L = u.shape[1]
    pad = HY_SHORT // 2
    up = jnp.pad(u, ((0, 0), (pad, HY_SHORT - 1 - pad), (0, 0)))
    out = up[:, 0:L] * w[0]
    for j in range(1, HY_SHORT):
        out = out + up[:, j:j + L] * w[j]
    return out + b


def hyena_filter_spectra(L, f_w1, f_b1, f_w2, f_b2, f_w3, f_b3, f_w4, freq):
    t = jnp.linspace(0.0, 1.0, L, dtype=F32)[:, None]
    bands = (HY_EMB - 1) // 2
    w = 2.0 * math.pi * jnp.arange(L, dtype=F32)[:, None] / L
    f = jnp.linspace(1e-4, bands - 1, bands, dtype=F32)[None, :]
    z = jnp.concatenate([t, jnp.cos(f * w), -jnp.sin(f * w)], axis=-1)
    fr = freq.astype(F32)
    h = jnp.sin(fr * (z @ f_w1.astype(F32) + f_b1.astype(F32)))
    h = jnp.sin(fr * (h @ f_w2.astype(F32) + f_b2.astype(F32)))
    h = jnp.sin(fr * (h @ f_w3.astype(F32) + f_b3.astype(F32)))
    h = (h @ f_w4.astype(F32)).reshape(L, HY_ORDER, 2, D_MODEL)
    max_decay = math.log(HY_TARGET) / HY_FAST_DECAY
    min_decay = math.log(HY_TARGET) / HY_SLOW_DECAY
    deltas = jnp.linspace(min_decay, max_decay, D_MODEL, dtype=F32)
    h = h * jnp.exp(-t * jnp.abs(deltas))[:, None, None, :]
    fwd, bwd = h[:, :, 0], h[:, :, 1]
    two_sided = jnp.concatenate(
        [fwd, jnp.zeros((1, HY_ORDER, D_MODEL), F32), jnp.flip(bwd[1:], axis=0)], axis=0)
    two_sided = two_sided / jnp.sum(jnp.abs(two_sided), axis=0, keepdims=True)
    return jnp.fft.rfft(two_sided, axis=0)


def fft_long_conv(u, spec, skip):
    L = u.shape[1]
    y = jnp.fft.irfft(jnp.fft.rfft(u, n=2 * L, axis=1) * spec[None], n=2 * L, axis=1)[:, :L]
    return y + u * skip


def hyena_mixer(u, w_in, b_in, conv_w, conv_b, f_w1, f_b1, f_w2, f_b2, f_w3, f_b3, f_w4,
                freq, bias, w_out, b_out):
    L = u.shape[1]
    z = short_conv(u @ w_in + b_in, conv_w, conv_b).astype(F32)
    parts = jnp.split(z, HY_ORDER + 1, axis=-1)
    y, gates = parts[0], parts[1:]
    spec = hyena_filter_spectra(L, f_w1, f_b1, f_w2, f_b2, f_w3, f_b3, f_w4, freq)
    skip = bias.astype(F32)
    for n in range(HY_ORDER):
        y = gates[n] * fft_long_conv(y, spec[:, n], skip[n])
    return y.astype(u.dtype) @ w_out + b_out


def gla_inputs(u, w_in, gk_w2, gk_b):
    B, L, _ = u.shape
    z = u @ w_in
    q, k, v, og, r = jnp.split(
        z, [GLA_DK, 2 * GLA_DK, 2 * GLA_DK + GLA_DV, 2 * GLA_DK + 2 * GLA_DV], axis=-1)
    r = r.reshape(B, L, 2, GLA_GATE_RANK)
    gk = jnp.einsum('blzr,zrd->blzd', r, gk_w2) + gk_b
    g = (jax.nn.log_sigmoid(gk.astype(F32)) / GLA_GATE_NORM).reshape(B, L, 2, GLA_HEADS, GLA_HK)
    q = q.astype(F32).reshape(B, L, GLA_HEADS, GLA_HK) * (GLA_HK ** -0.5)
    k = k.astype(F32).reshape(B, L, GLA_HEADS, GLA_HK)
    v = v.astype(F32).reshape(B, L, GLA_HEADS, GLA_HV)
    return q, k, v, g[:, :, 0], g[:, :, 1], og


def gla_chunk_scan(q, k, v, g, s0):
    B, L, H, dk = q.shape
    dv = v.shape[-1]
    n_chunks = L // GLA_CHUNK

    def to_chunks(a):
        return a.reshape(B, n_chunks, GLA_CHUNK, H, a.shape[-1]).transpose(1, 0, 3, 2, 4)

    q, k, v, g = to_chunks(q), to_chunks(k), to_chunks(v), to_chunks(g)
    b = jnp.cumsum(g, axis=3)
    b_last = b[..., -1:, :]
    q_t = q * jnp.exp(b)
    k_t = k * jnp.exp(-b)
    k_end = k * jnp.exp(b_last - b)
    mask = jnp.tril(jnp.ones((GLA_CHUNK, GLA_CHUNK), dtype=bool))
    att = jnp.where(mask, jnp.einsum('nbhcd,nbhsd->nbhcs', q_t, k_t), 0.0)
    o_intra = jnp.einsum('nbhcs,nbhsv->nbhcv', att, v)
    chunk_decay = jnp.exp(b_last[..., 0, :])

    def step(S, inp):
        qn, kn, vn, dn = inp
        o = jnp.einsum('bhcd,bhdv->bhcv', qn, S)
        S = S * dn[..., None] + jnp.einsum('bhcd,bhcv->bhdv', kn, vn)
        return S, o

    s_fin, o_inter = lax.scan(step, s0, (q_t, k_end, v, chunk_decay))
    o = (o_intra + o_inter).transpose(1, 0, 3, 2, 4).reshape(B, L, H, dv)
    return o, s_fin


def gla_direction(q, k, v, g, s0, reverse):
    if reverse:
        q, k, v, g = (jnp.flip(a, axis=1) for a in (q, k, v, g))
    o, s_fin = gla_chunk_scan(q, k, v, g, s0)
    if reverse:
        o = jnp.flip(o, axis=1)
    return o, s_fin


def gla_out(o, og, onorm, wo):
    B, L = o.shape[:2]
    o = rmsnorm(o, onorm).reshape(B, L, GLA_DV).astype(og.dtype)
    return (o * jax.nn.silu(og)) @ wo


def gla_mixer(h_lat, h_ctx, w_in, gk_w2, gk_b, onorm, wo, ctx_out):
    ql, kl, vl, gfl, gbl, ogl = gla_inputs(h_lat, w_in, gk_w2, gk_b)
    qc, kc, vc, gfc, gbc, ogc = gla_inputs(h_ctx, w_in, gk_w2, gk_b)
    s0 = jnp.zeros((h_lat.shape[0], GLA_HEADS, GLA_HK, GLA_HV), F32)
    oc_f, s_f = gla_direction(qc, kc, vc, gfc, s0, False)
    oc_b, s_b = gla_direction(qc, kc, vc, gbc, s0, True)
    ol_f, _ = gla_direction(ql, kl, vl, gfl, s_f, False)
    ol_b, _ = gla_direction(ql, kl, vl, gbl, s_b, True)
    y_lat = gla_out(ol_f + ol_b, ogl, onorm, wo)
    y_ctx = gla_out(oc_f + oc_b, ogc, onorm, wo) if ctx_out else None
    return y_lat, y_ctx


def rotate_half_axis(xh, ang):
    cos = jnp.cos(ang)[None, :, None, :]
    sin = jnp.sin(ang)[None, :, None, :]
    x1, x2 = jnp.split(xh, 2, axis=-1)
    return jnp.concatenate([x1 * cos - x2 * sin, x1 * sin + x2 * cos], axis=-1)


def axial_rope(x, ang_row, ang_col):
    xf = x.astype(F32)
    half = x.shape[-1] // 2
    out = jnp.concatenate([rotate_half_axis(xf[..., :half], ang_row),
                           rotate_half_axis(xf[..., half:], ang_col)], axis=-1)
    return out.astype(x.dtype)


def mla_qkv(u, angles, w_down, qnorm, w_uq, kvnorm, w_ukv):
    B, L, _ = u.shape
    cq, ckv, k_rope = jnp.split(u @ w_down, [MLA_Q_RANK, MLA_Q_RANK + MLA_KV_RANK], axis=-1)
    q = (rmsnorm(cq, qnorm) @ w_uq).reshape(B, L, MLA_HEADS, MLA_NOPE + MLA_ROPE)
    kv = (rmsnorm(ckv, kvnorm) @ w_ukv).reshape(B, L, MLA_HEADS, MLA_NOPE + MLA_V)
    q_nope, q_rope = q[..., :MLA_NOPE], q[..., MLA_NOPE:]
    k_nope, v = kv[..., :MLA_NOPE], kv[..., MLA_NOPE:]
    k_rope = k_rope[:, :, None, :]
    if angles is not None:
        q_rope = axial_rope(q_rope, angles[0], angles[1])
        k_rope = axial_rope(k_rope, angles[0], angles[1])
    q = jnp.concatenate([q_nope, q_rope], axis=-1)
    k = jnp.concatenate([k_nope, jnp.broadcast_to(k_rope, (B, L, MLA_HEADS, MLA_ROPE))], axis=-1)
    return q, k, v


def _pad_heads(a):
    B, L, H, d = a.shape
    a = jnp.pad(a, ((0, 0), (0, 0), (0, 0), (0, MLA_QK_PAD - d)))
    return a.reshape(B, L, H * MLA_QK_PAD).astype(BF16)


def mla_mixer(h_lat, h_ctx, w_down, qnorm, w_uq, kvnorm, w_ukv, wo):
    B, L, _ = h_lat.shape
    rows = L // GRID_W
    r_idx, c_idx = jnp.meshgrid(jnp.arange(rows), jnp.arange(GRID_W), indexing='ij')
    half = MLA_ROPE // 2
    inv_freq = ROPE_THETA ** (-jnp.arange(0, half, 2, dtype=F32) / half)
    ang_row = r_idx.reshape(-1).astype(F32)[:, None] * inv_freq[None, :]
    ang_col = c_idx.reshape(-1).astype(F32)[:, None] * inv_freq[None, :]
    ql, kl, vl = mla_qkv(h_lat, (ang_row, ang_col), w_down, qnorm, w_uq, kvnorm, w_ukv)
    qc, kc, vc = mla_qkv(h_ctx, None, w_down, qnorm, w_uq, kvnorm, w_ukv)
    o = mla_attention(_pad_heads(ql), _pad_heads(kc), vc.reshape(B, -1, MLA_HEADS * MLA_V).astype(BF16),
                      _pad_heads(kl), vl.reshape(B, L, MLA_HEADS * MLA_V).astype(BF16))
    return o.astype(F32) @ wo


def kernel(x, c, ctx, c_ctx, ada_w, ada_b, norm1_g, norm2_g, mlp_w1, mlp_w2, final_g, hy_w_in, hy_b_in, hy_conv_w, hy_conv_b, hy_f_w1, hy_f_b1, hy_f_w2, hy_f_b2, hy_f_w3, hy_f_b3, hy_f_w4, hy_freq, hy_bias, hy_w_out, hy_b_out, gla_w_in, gla_gk_w2, gla_gk_b, gla_onorm, gla_wo, mla_w_down, mla_qnorm, mla_w_uq, mla_kvnorm, mla_w_ukv, mla_wo):
    x_lat = x
    x_ctx = ctx
    silu_c = jax.nn.silu(c)
    silu_cc = jax.nn.silu(c_ctx)
    for i in range(DEPTH):
        kind = i % N_MIXERS
        j = i // N_MIXERS
        ctx_read = kind != 0
        ctx_live = any(l % N_MIXERS != 0 for l in range(i + 1, DEPTH))
        mod_lat = (silu_c @ ada_w[i] + ada_b[i]).reshape(-1, 6, D_MODEL)
        mod_l = [mod_lat[:, k:k + 1, :] for k in range(6)]
        h_lat = modulate(rmsnorm(x_lat, norm1_g[i]), mod_l[0], mod_l[1])
        if ctx_read or ctx_live:
            mod_ctx = (silu_cc @ ada_w[i] + ada_b[i]).reshape(1, 6, D_MODEL)
            mod_c = [mod_ctx[0, k] for k in range(6)]
            h_ctx = modulate(rmsnorm(x_ctx, norm1_g[i]), mod_c[0], mod_c[1])
        y_ctx = None
        if kind == 0:
            hp = (hy_w_in[j], hy_b_in[j], hy_conv_w[j], hy_conv_b[j], hy_f_w1[j], hy_f_b1[j],
                  hy_f_w2[j], hy_f_b2[j], hy_f_w3[j], hy_f_b3[j], hy_f_w4[j], hy_freq[j],
                  hy_bias[j], hy_w_out[j], hy_b_out[j])
            y_lat = hyena_mixer(h_lat, *hp)
            y_ctx = hyena_mixer(h_ctx, *hp) if ctx_live else None
        elif kind == 1:
            y_lat, y_ctx = gla_mixer(h_lat, h_ctx, gla_w_in[j], gla_gk_w2[j], gla_gk_b[j],
                                     gla_onorm[j], gla_wo[j], ctx_live)
        else:
            y_lat = mla_mixer(h_lat, h_ctx, mla_w_down[j], mla_qnorm[j], mla_w_uq[j],
                              mla_kvnorm[j], mla_w_ukv[j], mla_wo[j])
        w1 = mlp_w1[i].astype(BF16)
        w2 = mlp_w2[i].astype(BF16)
        x_lat = x_lat + mod_l[2] * y_lat
        x_lat = mlp_block(x_lat, norm2_g[i], mod_lat, w1, w2,
                          final_g=final_g if i == DEPTH - 1 else None)
        if ctx_live:
            x_ctx = x_ctx + mod_c[2] * y_ctx
            x_ctx = mlp_block(x_ctx, norm2_g[i], mod_ctx, w1, w2)
    return x_lat
```

```python
import functools
import math

import jax
import jax.numpy as jnp
import numpy as np
from jax import lax
from jax.experimental import pallas as pl
from jax.experimental.pallas import tpu as pltpu

F32 = jnp.float32
BF16 = jnp.bfloat16

D_MODEL = 1024
DEPTH = 4
GRID_W = 64
N_MIXERS = 3
NORM_EPS = 1e-6

HY_ORDER = 2
HY_EMB = 33
HY_SHORT = 3
HY_FAST_DECAY = 0.3
HY_SLOW_DECAY = 1.5
HY_TARGET = 1e-2

GLA_HEADS = 4
GLA_DK = D_MODEL // 2
GLA_DV = D_MODEL
GLA_HK = GLA_DK // GLA_HEADS
GLA_HV = GLA_DV // GLA_HEADS
GLA_GATE_RANK = 16
GLA_GATE_NORM = 16.0
GLA_CHUNK = 64

MLA_HEADS = 8
MLA_Q_RANK = 384
MLA_KV_RANK = 256
MLA_NOPE = 128
MLA_ROPE = 64
MLA_V = 128
ROPE_THETA = 10000.0
Q_BLOCK = 128

VMEM_LIMIT_BYTES = 56 * 1024 * 1024
MLA_QK_PAD = 256


def _const_spec(shape):
    nd = len(shape)
    return pl.BlockSpec(shape, lambda *_: (0,) * nd, pipeline_mode=pl.Buffered(1))


def _norm_mod(x, g, shift, scale):
    y = x * lax.rsqrt(jnp.mean(x * x, axis=-1, keepdims=True) + NORM_EPS)
    return (y * g) * (1.0 + scale) + shift


def _mlp_kernel(x_ref, g_ref, mod_ref, w1_ref, w2_ref, fg_ref, o_ref, *, hidden_chunk, final_norm):
    x = x_ref[0]
    h = _norm_mod(x, g_ref[...], mod_ref[0, 3:4, :], mod_ref[0, 4:5, :]).astype(BF16)
    hidden = w1_ref.shape[1]
    acc = jnp.zeros(x.shape, F32)
    for c0 in range(0, hidden, hidden_chunk):
        a = jnp.dot(h, w1_ref[:, c0:c0 + hidden_chunk], preferred_element_type=F32)
        a = jnp.square(jnp.maximum(a, 0.0)).astype(BF16)
        acc = acc + jnp.dot(a, w2_ref[c0:c0 + hidden_chunk, :], preferred_element_type=F32)
    out = x + mod_ref[0, 5:6, :] * acc
    if final_norm:
        out = (out * lax.rsqrt(jnp.mean(out * out, axis=-1, keepdims=True) + NORM_EPS)) * fg_ref[...]
    o_ref[0] = out


def mlp_block(x, norm_g, mod, w1, w2, final_g=None):
    B, L, D = x.shape
    tm = min(512, L)
    per_batch = mod.shape[0] != 1
    final_norm = final_g is not None
    fg = (final_g if final_norm else norm_g).reshape(1, D)
    kern = functools.partial(_mlp_kernel, hidden_chunk=1024, final_norm=final_norm)
    return pl.pallas_call(
        kern,
        grid=(B, L // tm),
        in_specs=[
            pl.BlockSpec((1, tm, D), lambda b, i: (b, i, 0)),
            _const_spec((1, D)),
            pl.BlockSpec((1, 6, D), (lambda b, i: (b, 0, 0)) if per_batch else (lambda b, i: (0, 0, 0))),
            _const_spec(w1.shape),
            _const_spec(w2.shape),
            _const_spec((1, D)),
        ],
        out_specs=pl.BlockSpec((1, tm, D), lambda b, i: (b, i, 0)),
        out_shape=jax.ShapeDtypeStruct((B, L, D), F32),
        compiler_params=pltpu.CompilerParams(
            dimension_semantics=("arbitrary", "arbitrary"), vmem_limit_bytes=VMEM_LIMIT_BYTES),
        name="mlp_block",
    )(x, norm_g.reshape(1, D), mod, w1, w2, fg)


def _attn_kernel(q_ref, kc_ref, vc_ref, k_ref, v_ref, o_ref, m_ref, l_ref, acc_ref, sa_ref, sb_ref, *, sub, c):
    q = q_ref[0]
    m_ref[...] = jnp.full(m_ref.shape, -jnp.inf, F32)
    l_ref[...] = jnp.zeros(l_ref.shape, F32)
    acc_ref[...] = jnp.zeros(acc_ref.shape, F32)
    lanes = m_ref.shape[1]
    n_sub = k_ref.shape[1] // sub

    def scores(k):
        return lax.dot_general(q, k, (((1,), (1,)), ((), ())), preferred_element_type=F32)

    def lat(ref, n):
        return ref[0, pl.ds(pl.multiple_of(n * sub, sub), sub), :]

    def accumulate(s, v):
        m_prev = m_ref[...]
        m_new = jnp.maximum(m_prev, jnp.max(s, axis=-1, keepdims=True))
        alpha = jnp.exp2((m_prev - m_new) * c)
        ps = [jnp.exp2((s[:, t:t + lanes] - m_new) * c) for t in range(0, s.shape[1], lanes)]
        l_ref[...] = alpha * l_ref[...] + functools.reduce(lambda a, b: a + b, ps)
        p = jnp.concatenate(ps, axis=1).astype(BF16)
        acc_ref[...] = alpha * acc_ref[...] + jnp.dot(p, v, preferred_element_type=F32)
        m_ref[...] = m_new

    sa_ref[...] = scores(lat(k_ref, 0))
    accumulate(scores(kc_ref[0]), vc_ref[0])

    def pair(n):
        sb_ref[...] = scores(lat(k_ref, n + 1))
        accumulate(sa_ref[...], lat(v_ref, n))

    def body(j, carry):
        n = 2 * j
        pair(n)
        sa_ref[...] = scores(lat(k_ref, n + 2))
        accumulate(sb_ref[...], lat(v_ref, n + 1))
        return carry

    lax.fori_loop(0, n_sub // 2 - 1, body, 0)
    pair(n_sub - 2)
    accumulate(sb_ref[...], lat(v_ref, n_sub - 1))
    l = jnp.sum(l_ref[...], axis=-1, keepdims=True)
    o_ref[0] = (acc_ref[...] / l).astype(o_ref.dtype)


def mla_attention(q, kc, vc, k, v):
    B, L, _ = q.shape
    C = kc.shape[1]
    H = MLA_HEADS
    tq = min(512, L)
    sub = min(512, L // 4)
    assert L % (2 * sub) == 0 and L % tq == 0
    kern = functools.partial(_attn_kernel, sub=sub,
                             c=(MLA_NOPE + MLA_ROPE) ** -0.5 * math.log2(math.e))
    return pl.pallas_call(
        kern,
        grid=(B, H, L // tq),
        in_specs=[
            pl.BlockSpec((1, tq, MLA_QK_PAD), lambda b, h, i: (b, i, h)),
            pl.BlockSpec((1, C, MLA_QK_PAD), lambda b, h, i: (b, 0, h)),
            pl.BlockSpec((1, C, MLA_V), lambda b, h, i: (b, 0, h)),
            pl.BlockSpec((1, L, MLA_QK_PAD), lambda b, h, i: (b, 0, h)),
            pl.BlockSpec((1, L, MLA_V), lambda b, h, i: (b, 0, h)),
        ],
        out_specs=pl.BlockSpec((1, tq, MLA_V), lambda b, h, i: (b, i, h)),
        out_shape=jax.ShapeDtypeStruct((B, L, H * MLA_V), BF16),
        scratch_shapes=[pltpu.VMEM((tq, MLA_V), F32), pltpu.VMEM((tq, MLA_V), F32), pltpu.VMEM((tq, MLA_V), F32),
                        pltpu.VMEM((tq, sub), F32), pltpu.VMEM((tq, sub), F32)],
        compiler_params=pltpu.CompilerParams(
            dimension_semantics=("arbitrary", "arbitrary", "arbitrary"), vmem_limit_bytes=VMEM_LIMIT_BYTES),
        name="mla_attention",
    )(q, kc, vc, k, v)


DFT_N1 = 128
HY_CH_BLOCK = 16
HY_TAP_ROWS = 64


def _dft_constants():
    n1 = DFT_N1
    n = n1 * n1
    idx = np.arange(n1, dtype=np.float64)
    th = 2.0 * np.pi * np.outer(idx, idx) / n1
    cos1, sin1 = np.cos(th), np.sin(th)
    tw = 2.0 * np.pi * np.outer(idx, idx) / n
    fr, fi = cos1, -sin1
    c = dict(
        e1=np.concatenate([cos1, -sin1], axis=0),
        tr=np.cos(tw), ti=-np.sin(tw),
        g=np.block([[fr, fi], [-fi, fr]]),
        gbar=np.block([[fr, -fi], [fi, fr]]),
        e2=np.concatenate([cos1, -sin1], axis=1) / n,
    )
    return {k: jnp.asarray(v, F32) for k, v in c.items()}


def _hy_in_kernel(x_ref, g_ref, mod_ref, wt_ref, b_ref, o_ref, *, row_chunk):
    h = _norm_mod(x_ref[0], g_ref[...], mod_ref[0, 0:1, :], mod_ref[0, 1:2, :]).astype(BF16)
    for r0 in range(0, wt_ref.shape[0], row_chunk):
        z = lax.dot_general(wt_ref[r0:r0 + row_chunk, :], h, (((1,), (1,)), ((), ())),
                            preferred_element_type=F32)
        o_ref[0, r0:r0 + row_chunk, :] = z + b_ref[r0:r0 + row_chunk, :]


def hyena_in_proj(x, norm_g, mod, w_in_t, b_in):
    B, L, D = x.shape
    n_out = w_in_t.shape[0]
    tm = min(512, L)
    per_batch = mod.shape[0] != 1
    return pl.pallas_call(
        functools.partial(_hy_in_kernel, row_chunk=512),
        grid=(B, L // tm),
        in_specs=[
            pl.BlockSpec((1, tm, D), lambda b, i: (b, i, 0)),
            _const_spec((1, D)),
            pl.BlockSpec((1, 6, D), (lambda b, i: (b, 0, 0)) if per_batch else (lambda b, i: (0, 0, 0))),
            _const_spec(w_in_t.shape),
            _const_spec(b_in.shape),
        ],
        out_specs=pl.BlockSpec((1, n_out, tm), lambda b, i: (b, 0, i)),
        out_shape=jax.ShapeDtypeStruct((B, n_out, L), F32),
        compiler_params=pltpu.CompilerParams(
            dimension_semantics=("arbitrary", "arbitrary"), vmem_limit_bytes=VMEM_LIMIT_BYTES),
        name="hyena_in_proj",
    )(x, norm_g.reshape(1, D), mod, w_in_t, b_in)


def _hy_hidden_kernel(zf_ref, w1_ref, b1_ref, w2_ref, b2_ref, w3_ref, b3_ref, fr_ref, o_ref):
    fr = fr_ref[...]
    h = zf_ref[...].astype(BF16)
    for w_ref, b_ref in ((w1_ref, b1_ref), (w2_ref, b2_ref), (w3_ref, b3_ref)):
        h = jnp.sin(fr * (jnp.dot(w_ref[...], h, preferred_element_type=F32) + b_ref[...]))
        out = h
        h = h.astype(BF16)
    o_ref[...] = out


def _hy_taps_kernel(hid_ref, t_ref, w4_ref, dl_ref, o_ref, *, half):
    hid = hid_ref[...].astype(BF16)
    tf = jnp.dot(w4_ref[0, 0].astype(BF16), hid[:, :half], preferred_element_type=F32)
    tb = jnp.dot(w4_ref[0, 1].astype(BF16), hid[:, half:], preferred_element_type=F32)
    taps = jnp.concatenate([tf, tb], axis=1) * jnp.exp(-t_ref[...] * dl_ref[...])
    pos = lax.broadcasted_iota(jnp.int32, taps.shape, 1)
    taps = jnp.where(pos == half, 0.0, taps)
    o_ref[0] = taps / jnp.sum(jnp.abs(taps), axis=1, keepdims=True)


def hyena_filter_taps(L, f_w1, f_b1, f_w2, f_b2, f_w3, f_b3, f_w4, freq):
    width = f_w1.shape[1]
    n = 2 * L
    pos = np.arange(n)
    pos = np.where(pos <= L, np.minimum(pos, L - 1), n - pos).astype(np.float64)
    bands = (HY_EMB - 1) // 2
    t = jnp.asarray(pos / (L - 1), F32)[None, :]
    w = 2.0 * math.pi * jnp.asarray(pos, F32)[None, :] / L
    f = jnp.linspace(1e-4, bands - 1, bands, dtype=F32)[:, None]
    zf = jnp.concatenate([t, jnp.cos(f * w), -jnp.sin(f * w)], axis=0)
    zf = jnp.pad(zf, ((0, width - HY_EMB), (0, 0)))
    w1t = jnp.pad(f_w1.T, ((0, 0), (0, width - HY_EMB))).astype(BF16)
    col = lambda v: v.reshape(width, 1).astype(F32)
    lane_blk = min(2048, n)
    hidden = pl.pallas_call(
        _hy_hidden_kernel,
        grid=(n // lane_blk,),
        in_specs=[pl.BlockSpec((width, lane_blk), lambda i: (0, i))] + [_const_spec((width, width)), _const_spec((width, 1))] * 3
        + [_const_spec((width, 1))],
        out_specs=pl.BlockSpec((width, lane_blk), lambda i: (0, i)),
        out_shape=jax.ShapeDtypeStruct((width, n), F32),
        name="hyena_filter_hidden",
    )(zf, w1t, col(f_b1), f_w2.T.astype(BF16), col(f_b2), f_w3.T.astype(BF16), col(f_b3), col(freq))
    max_decay = math.log(HY_TARGET) / HY_FAST_DECAY
    min_decay = math.log(HY_TARGET) / HY_SLOW_DECAY
    deltas = jnp.abs(jnp.linspace(min_decay, max_decay, D_MODEL, dtype=F32)).reshape(D_MODEL, 1)
    w4t = f_w4.T.reshape(HY_ORDER, 2, D_MODEL, width)
    rows = HY_TAP_ROWS
    return pl.pallas_call(
        functools.partial(_hy_taps_kernel, half=L),
        grid=(HY_ORDER, D_MODEL // rows),
        in_specs=[
            _const_spec((width, n)),
            _const_spec((1, n)),
            pl.BlockSpec((1, 2, rows, width), lambda o, i: (o, 0, i, 0)),
            pl.BlockSpec((rows, 1), lambda o, i: (i, 0)),
        ],
        out_specs=pl.BlockSpec((1, rows, n), lambda o, i: (o, i, 0)),
        out_shape=jax.ShapeDtypeStruct((HY_ORDER, D_MODEL, n), F32),
        compiler_params=pltpu.CompilerParams(
            dimension_semantics=("arbitrary", "arbitrary"), vmem_limit_bytes=VMEM_LIMIT_BYTES),
        name="hyena_filter_taps",
    )(hidden, t, w4t, deltas)


def _forward_dft(load_x, e1, tr, ti, g_ref, zb_ref, n_ch):
    n1 = DFT_N1

    def stage1(c, carry):
        z = jnp.dot(e1, load_x(c), preferred_element_type=F32)
        zr, zi = z[:n1], z[n1:]
        r0 = pl.multiple_of(c * n1, n1)
        zb_ref[pl.ds(r0, n1), 0:n1] = (zr * tr - zi * ti).astype(BF16)
        zb_ref[pl.ds(r0, n1), n1:2 * n1] = (zr * ti + zi * tr).astype(BF16)
        return carry

    lax.fori_loop(0, n_ch, stage1, 0, unroll=2)
    return jnp.dot(zb_ref[...], g_ref[...], preferred_element_type=F32)


def _hy_spectrum_kernel(x_ref, e1_ref, tr_ref, ti_ref, g_ref, o_ref, zb_ref):
    n_ch = x_ref.shape[0]
    e1, tr, ti = e1_ref[...], tr_ref[...], ti_ref[...]
    xh = _forward_dft(lambda c: x_ref[c].astype(BF16), e1, tr, ti, g_ref, zb_ref, n_ch)
    o_ref[...] = xh.reshape(o_ref.shape)


def hyena_filter_spectrum(taps, consts):
    R = taps.shape[0]
    n1 = DFT_N1
    C = HY_CH_BLOCK
    x = taps.reshape(R, n1, n1)
    return pl.pallas_call(
        _hy_spectrum_kernel,
        grid=(R // C,),
        in_specs=[pl.BlockSpec((C, n1, n1), lambda i: (i, 0, 0)), _const_spec((2 * n1, n1)),
                  _const_spec((n1, n1)), _const_spec((n1, n1)), _const_spec((2 * n1, 2 * n1))],
        out_specs=pl.BlockSpec((C, n1, 2 * n1), lambda i: (i, 0, 0)),
        out_shape=jax.ShapeDtypeStruct((R, n1, 2 * n1), F32),
        scratch_shapes=[pltpu.VMEM((C * n1, 2 * n1), BF16)],
        compiler_params=pltpu.CompilerParams(
            dimension_semantics=("arbitrary",), vmem_limit_bytes=VMEM_LIMIT_BYTES),
        name="hyena_filter_spectrum",
    )(x, consts["e1"].astype(BF16), consts["tr"], consts["ti"], consts["g"].astype(BF16))


def _hy_conv_kernel(cw_ref, cb_ref, sk_ref, zy_ref, zg1_ref, zg2_ref, hh_ref, e1_ref, tr_ref, ti_ref,
                    g_ref, gb_ref, e2_ref, o_ref, y_sc, g1_sc, g2_sc, zb_sc, u_sc, *, n_ch, d_model):
    n1 = DFT_N1
    rows = zy_ref.shape[2]
    ch0 = pl.program_id(0) * n_ch
    a_idx = lax.broadcasted_iota(jnp.int32, (rows, n1), 0)
    b_idx = lax.broadcasted_iota(jnp.int32, (rows, n1), 1)
    e1, tr, ti, e2 = e1_ref[...], tr_ref[...], ti_ref[...], e2_ref[...]

    def short_conv(z, col):
        r = pltpu.roll(z, 1, axis=1)
        prev = jnp.where(b_idx == 0, jnp.where(a_idx == 0, 0.0, pltpu.roll(r, 1, axis=0)), r)
        r = pltpu.roll(z, n1 - 1, axis=1)
        nxt = jnp.where(b_idx == n1 - 1, jnp.where(a_idx == rows - 1, 0.0, pltpu.roll(r, rows - 1, axis=0)), r)
        n_col = 3 * d_model
        return cw_ref[col] * prev + cw_ref[n_col + col] * z + cw_ref[2 * n_col + col] * nxt + cb_ref[col]

    def prep(c, carry):
        y_sc[c] = short_conv(zy_ref[0, c], ch0 + c)
        g1_sc[c] = short_conv(zg1_ref[0, c], d_model + ch0 + c)
        g2_sc[c] = short_conv(zg2_ref[0, c], 2 * d_model + ch0 + c)
        return carry

    lax.fori_loop(0, n_ch, prep, 0)

    for order, gate_sc in enumerate((g1_sc, g2_sc)):
        xh = _forward_dft(lambda c: y_sc[c].astype(BF16), e1, tr, ti, g_ref, zb_sc, n_ch)
        hh = hh_ref[order].reshape(n_ch * n1, 2 * n1)
        xr, xi, hr, hi = xh[:, :n1], xh[:, n1:], hh[:, :n1], hh[:, n1:]
        yh = jnp.concatenate([xr * hr - xi * hi, xr * hi + xi * hr], axis=1).astype(BF16)
        u_sc[...] = jnp.dot(yh, gb_ref[...], preferred_element_type=F32)

        def finish(c, carry, order=order, gate_sc=gate_sc):
            r0 = pl.multiple_of(c * n1, n1)
            ur, ui = u_sc[pl.ds(r0, n1), 0:n1], u_sc[pl.ds(r0, n1), n1:2 * n1]
            stacked = jnp.concatenate([ur * tr + ui * ti, ui * tr - ur * ti], axis=0).astype(BF16)
            conv = jnp.dot(e2, stacked, preferred_element_type=F32)
            y = y_sc[c]
            y = gate_sc[c] * (conv + y * sk_ref[order * d_model + ch0 + c])
            if order == HY_ORDER - 1:
                o_ref[0, c] = y.astype(o_ref.dtype)
            else:
                y_sc[c] = y
            return carry

        lax.fori_loop(0, n_ch, finish, 0, unroll=2)


def hyena_long_conv(z_t, spectrum, conv_w, conv_b, skip, consts):
    B, n_col, L = z_t.shape
    D = n_col // (HY_ORDER + 1)
    n1 = DFT_N1
    rows = L // n1
    assert 2 * rows == n1 and HY_ORDER == 2
    C = HY_CH_BLOCK
    nblk = D // C
    z4 = z_t.reshape(B, n_col, rows, n1)
    smem = pl.BlockSpec(memory_space=pltpu.SMEM)
    zspec = lambda off: pl.BlockSpec((1, C, rows, n1), lambda i, b: (b, i + off * nblk, 0, 0))
    kern = functools.partial(_hy_conv_kernel, n_ch=C, d_model=D)
    out = pl.pallas_call(
        kern,
        grid=(nblk, B),
        in_specs=[smem, smem, smem, zspec(0), zspec(1), zspec(2),
                  pl.BlockSpec((HY_ORDER, C, n1, 2 * n1), lambda i, b: (0, i, 0, 0)),
                  _const_spec((2 * n1, rows)), _const_spec((n1, n1)), _const_spec((n1, n1)),
                  _const_spec((2 * n1, 2 * n1)), _const_spec((2 * n1, 2 * n1)), _const_spec((rows, 2 * n1))],
        out_specs=pl.BlockSpec((1, C, rows, n1), lambda i, b: (b, i, 0, 0)),
        out_shape=jax.ShapeDtypeStruct((B, D, rows, n1), BF16),
        scratch_shapes=[pltpu.VMEM((C, rows, n1), F32), pltpu.VMEM((C, rows, n1), F32), pltpu.VMEM((C, rows, n1), F32),
                        pltpu.VMEM((C * n1, 2 * n1), BF16), pltpu.VMEM((C * n1, 2 * n1), F32)],
        compiler_params=pltpu.CompilerParams(
            dimension_semantics=("arbitrary", "arbitrary"), vmem_limit_bytes=VMEM_LIMIT_BYTES),
        name="hyena_long_conv",
    )(conv_w.reshape(-1), conv_b.reshape(-1), skip.reshape(-1), z4, z4, z4, spectrum,
      consts["e1"][:, :rows].astype(BF16), consts["tr"], consts["ti"], consts["g"].astype(BF16),
      consts["gbar"].astype(BF16), consts["e2"][:rows].astype(BF16))
    return out.reshape(B, D, L)


def _hy_out_kernel(y_ref, x_ref, mod_ref, w_ref, b_ref, o_ref):
    acc = lax.dot_general(y_ref[0], w_ref[...], (((0,), (0,)), ((), ())), preferred_element_type=F32)
    o_ref[0] = x_ref[0] + mod_ref[0, 2:3, :] * (acc + b_ref[...])


def hyena_out_proj(y_t, x, mod, w_out, b_out):
    B, L, D = x.shape
    tm = min(512, L)
    per_batch = mod.shape[0] != 1
    return pl.pallas_call(
        _hy_out_kernel,
        grid=(B, L // tm),
        in_specs=[
            pl.BlockSpec((1, D, tm), lambda b, i: (b, 0, i)),
            pl.BlockSpec((1, tm, D), lambda b, i: (b, i, 0)),
            pl.BlockSpec((1, 6, D), (lambda b, i: (b, 0, 0)) if per_batch else (lambda b, i: (0, 0, 0))),
            _const_spec(w_out.shape),
            _const_spec((1, D)),
        ],
        out_specs=pl.BlockSpec((1, tm, D), lambda b, i: (b, i, 0)),
        out_shape=jax.ShapeDtypeStruct((B, L, D), F32),
        compiler_params=pltpu.CompilerParams(
            dimension_semantics=("arbitrary", "arbitrary"), vmem_limit_bytes=VMEM_LIMIT_BYTES),
        name="hyena_out_proj",
    )(y_t, x, mod, w_out, b_out.reshape(1, D))


def rmsnorm(x, g):
    xf = x.astype(F32)
    y = xf * lax.rsqrt(jnp.mean(xf * xf, axis=-1, keepdims=True) + NORM_EPS)
    return (y * g.astype(F32)).astype(x.dtype)


def modulate(h, shift, scale):
    return h * (1 + scale) + shift


def short_conv(u, w, b):
    L = u.shape[1]
    pad = HY_SHORT // 2
    up = jnp.pad(u, ((0, 0), (pad, HY_SHORT - 1 - pad), (0, 0)))
    out = up[:, 0:L] * w[0]
    for j in range(1, HY_SHORT):
        out = out + up[:, j:j + L] * w[j]
    return out + b


def hyena_filter_spectra(L, f_w1, f_b1, f_w2, f_b2, f_w3, f_b3, f_w4, freq):
    t = jnp.linspace(0.0, 1.0, L, dtype=F32)[:, None]
    bands = (HY_EMB - 1) // 2
    w = 2.0 * math.pi * jnp.arange(L, dtype=F32)[:, None] / L
    f = jnp.linspace(1e-4, bands - 1, bands, dtype=F32)[None, :]
    z = jnp.concatenate([t, jnp.cos(f * w), -jnp.sin(f * w)], axis=-1)
    fr = freq.astype(F32)
    h = jnp.sin(fr * (z @ f_w1.astype(F32) + f_b1.astype(F32)))
    h = jnp.sin(fr * (h @ f_w2.astype(F32) + f_b2.astype(F32)))
    h = jnp.sin(fr * (h @ f_w3.astype(F32) + f_b3.astype(F32)))
    h = (h @ f_w4.astype(F32)).reshape(L, HY_ORDER, 2, D_MODEL)
    max_decay = math.log(HY_TARGET) / HY_FAST_DECAY
    min_decay = math.log(HY_TARGET) / HY_SLOW_DECAY
    deltas = jnp.linspace(min_decay, max_decay, D_MODEL, dtype=F32)
    h = h * jnp.exp(-t * jnp.abs(deltas))[:, None, None, :]
    fwd, bwd = h[:, :, 0], h[:, :, 1]
    two_sided = jnp.concatenate(
        [fwd, jnp.zeros((1, HY_ORDER, D_MODEL), F32), jnp.flip(bwd[1:], axis=0)], axis=0)
    two_sided = two_sided / jnp.sum(jnp.abs(two_sided), axis=0, keepdims=True)
    return jnp.fft.rfft(two_sided, axis=0)


def fft_long_conv(u, spec, skip):
    L = u.shape[1]
    y = jnp.fft.irfft(jnp.fft.rfft(u, n=2 * L, axis=1) * spec[None], n=2 * L, axis=1)[:, :L]
    return y + u * skip


def hyena_mixer(u, w_in, b_in, conv_w, conv_b, f_w1, f_b1, f_w2, f_b2, f_w3, f_b3, f_w4,
                freq, bias, w_out, b_out):
    L = u.shape[1]
    z = short_conv(u @ w_in + b_in, conv_w, conv_b).astype(F32)
    parts = jnp.split(z, HY_ORDER + 1, axis=-1)
    y, gates = parts[0], parts[1:]
    spec = hyena_filter_spectra(L, f_w1, f_b1, f_w2, f_b2, f_w3, f_b3, f_w4, freq)
    skip = bias.astype(F32)
    for n in range(HY_ORDER):
        y = gates[n] * fft_long_conv(y, spec[:, n], skip[n])
    return y.astype(u.dtype) @ w_out + b_out


def gla_inputs(u, w_in, gk_w2, gk_b):
    B, L, _ = u.shape
    z = u @ w_in
    q, k, v, og, r = jnp.split(
        z, [GLA_DK, 2 * GLA_DK, 2 * GLA_DK + GLA_DV, 2 * GLA_DK + 2 * GLA_DV], axis=-1)
    r = r.reshape(B, L, 2, GLA_GATE_RANK)
    gk = jnp.einsum('blzr,zrd->blzd', r, gk_w2) + gk_b
    g = (jax.nn.log_sigmoid(gk.astype(F32)) / GLA_GATE_NORM).reshape(B, L, 2, GLA_HEADS, GLA_HK)
    q = q.astype(F32).reshape(B, L, GLA_HEADS, GLA_HK) * (GLA_HK ** -0.5)
    k = k.astype(F32).reshape(B, L, GLA_HEADS, GLA_HK)
    v = v.astype(F32).reshape(B, L, GLA_HEADS, GLA_HV)
    return q, k, v, g[:, :, 0], g[:, :, 1], og


def gla_chunk_scan(q, k, v, g, s0):
    B, L, H, dk = q.shape
    dv = v.shape[-1]
    n_chunks = L // GLA_CHUNK

    def to_chunks(a):
        return a.reshape(B, n_chunks, GLA_CHUNK, H, a.shape[-1]).transpose(1, 0, 3, 2, 4)

    q, k, v, g = to_chunks(q), to_chunks(k), to_chunks(v), to_chunks(g)
    b = jnp.cumsum(g, axis=3)
    b_last = b[..., -1:, :]
    q_t = q * jnp.exp(b)
    k_t = k * jnp.exp(-b)
    k_end = k * jnp.exp(b_last - b)
    mask = jnp.tril(jnp.ones((GLA_CHUNK, GLA_CHUNK), dtype=bool))
    att = jnp.where(mask, jnp.einsum('nbhcd,nbhsd->nbhcs', q_t, k_t), 0.0)
    o_intra = jnp.einsum('nbhcs,nbhsv->nbhcv', att, v)
    chunk_decay = jnp.exp(b_last[..., 0, :])

    def step(S, inp):
        qn, kn, vn, dn = inp
        o = jnp.einsum('bhcd,bhdv->bhcv', qn, S)
        S = S * dn[..., None] + jnp.einsum('bhcd,bhcv->bhdv', kn, vn)
        return S, o

    s_fin, o_inter = lax.scan(step, s0, (q_t, k_end, v, chunk_decay))
    o = (o_intra + o_inter).transpose(1, 0, 3, 2, 4).reshape(B, L, H, dv)
    return o, s_fin


def gla_direction(q, k, v, g, s0, reverse):
    if reverse:
        q, k, v, g = (jnp.flip(a, axis=1) for a in (q, k, v, g))
    o, s_fin = gla_chunk_scan(q, k, v, g, s0)
    if reverse:
        o = jnp.flip(o, axis=1)
    return o, s_fin


def gla_out(o, og, onorm, wo):
    B, L = o.shape[:2]
    o = rmsnorm(o, onorm).reshape(B, L, GLA_DV).astype(og.dtype)
    return (o * jax.nn.silu(og)) @ wo


def gla_mixer(h_lat, h_ctx, w_in, gk_w2, gk_b, onorm, wo, ctx_out):
    ql, kl, vl, gfl, gbl, ogl = gla_inputs(h_lat, w_in, gk_w2, gk_b)
    qc, kc, vc, gfc, gbc, ogc = gla_inputs(h_ctx, w_in, gk_w2, gk_b)
    s0 = jnp.zeros((h_lat.shape[0], GLA_HEADS, GLA_HK, GLA_HV), F32)
    oc_f, s_f = gla_direction(qc, kc, vc, gfc, s0, False)
    oc_b, s_b = gla_direction(qc, kc, vc, gbc, s0, True)
    ol_f, _ = gla_direction(ql, kl, vl, gfl, s_f, False)
    ol_b, _ = gla_direction(ql, kl, vl, gbl, s_b, True)
    y_lat = gla_out(ol_f + ol_b, ogl, onorm, wo)
    y_ctx = gla_out(oc_f + oc_b, ogc, onorm, wo) if ctx_out else None
    return y_lat, y_ctx


def rotate_half_axis(xh, ang):
    cos = jnp.cos(ang)[None, :, None, :]
    sin = jnp.sin(ang)[None, :, None, :]
    x1, x2 = jnp.split(xh, 2, axis=-1)
    return jnp.concatenate([x1 * cos - x2 * sin, x1 * sin + x2 * cos], axis=-1)


def axial_rope(x, ang_row, ang_col):
    xf = x.astype(F32)
    half = x.shape[-1] // 2
    out = jnp.concatenate([rotate_half_axis(xf[..., :half], ang_row),
                           rotate_half_axis(xf[..., half:], ang_col)], axis=-1)
    return out.astype(x.dtype)


def mla_qkv(u, angles, w_down, qnorm, w_uq, kvnorm, w_ukv):
    B, L, _ = u.shape
    cq, ckv, k_rope = jnp.split(u @ w_down, [MLA_Q_RANK, MLA_Q_RANK + MLA_KV_RANK], axis=-1)
    q = (rmsnorm(cq, qnorm) @ w_uq).reshape(B, L, MLA_HEADS, MLA_NOPE + MLA_ROPE)
    kv = (rmsnorm(ckv, kvnorm) @ w_ukv).reshape(B, L, MLA_HEADS, MLA_NOPE + MLA_V)
    q_nope, q_rope = q[..., :MLA_NOPE], q[..., MLA_NOPE:]
    k_nope, v = kv[..., :MLA_NOPE], kv[..., MLA_NOPE:]
    k_rope = k_rope[:, :, None, :]
    if angles is not None:
        q_rope = axial_rope(q_rope, angles[0], angles[1])
        k_rope = axial_rope(k_rope, angles[0], angles[1])
    q = jnp.concatenate([q_nope, q_rope], axis=-1)
    k = jnp.concatenate([k_nope, jnp.broadcast_to(k_rope, (B, L, MLA_HEADS, MLA_ROPE))], axis=-1)
    return q, k, v


def _pad_heads(a):
    B, L, H, d = a.shape
    a = jnp.pad(a, ((0, 0), (0, 0), (0, 0), (0, MLA_QK_PAD - d)))
    return a.reshape(B, L, H * MLA_QK_PAD).astype(BF16)


def mla_mixer(h_lat, h_ctx, w_down, qnorm, w_uq, kvnorm, w_ukv, wo):
    B, L, _ = h_lat.shape
    rows = L // GRID_W
    r_idx, c_idx = jnp.meshgrid(jnp.arange(rows), jnp.arange(GRID_W), indexing='ij')
    half = MLA_ROPE // 2
    inv_freq = ROPE_THETA ** (-jnp.arange(0, half, 2, dtype=F32) / half)
    ang_row = r_idx.reshape(-1).astype(F32)[:, None] * inv_freq[None, :]
    ang_col = c_idx.reshape(-1).astype(F32)[:, None] * inv_freq[None, :]
    ql, kl, vl = mla_qkv(h_lat, (ang_row, ang_col), w_down, qnorm, w_uq, kvnorm, w_ukv)
    qc, kc, vc = mla_qkv(h_ctx, None, w_down, qnorm, w_uq, kvnorm, w_ukv)
    o = mla_attention(_pad_heads(ql), _pad_heads(kc), vc.reshape(B, -1, MLA_HEADS * MLA_V).astype(BF16),
                      _pad_heads(kl), vl.reshape(B, L, MLA_HEADS * MLA_V).astype(BF16))
    return o.astype(F32) @ wo


def kernel(x, c, ctx, c_ctx, ada_w, ada_b, norm1_g, norm2_g, mlp_w1, mlp_w2, final_g, hy_w_in, hy_b_in, hy_conv_w, hy_conv_b, hy_f_w1, hy_f_b1, hy_f_w2, hy_f_b2, hy_f_w3, hy_f_b3, hy_f_w4, hy_freq, hy_bias, hy_w_out, hy_b_out, gla_w_in, gla_gk_w2, gla_gk_b, gla_onorm, gla_wo, mla_w_down, mla_qnorm, mla_w_uq, mla_kvnorm, mla_w_ukv, mla_wo):
    x_lat = x
    x_ctx = ctx
    silu_c = jax.nn.silu(c)
    silu_cc = jax.nn.silu(c_ctx)
    dft = _dft_constants()
    for i in range(DEPTH):
        kind = i % N_MIXERS
        j = i // N_MIXERS
        ctx_read = kind != 0
        ctx_live = any(l % N_MIXERS != 0 for l in range(i + 1, DEPTH))
        mod_lat = (silu_c @ ada_w[i] + ada_b[i]).reshape(-1, 6, D_MODEL)
        mod_l = [mod_lat[:, k:k + 1, :] for k in range(6)]
        h_lat = modulate(rmsnorm(x_lat, norm1_g[i]), mod_l[0], mod_l[1])
        if ctx_read or ctx_live:
            mod_ctx = (silu_cc @ ada_w[i] + ada_b[i]).reshape(1, 6, D_MODEL)
            mod_c = [mod_ctx[0, k] for k in range(6)]
            h_ctx = modulate(rmsnorm(x_ctx, norm1_g[i]), mod_c[0], mod_c[1])
        y_ctx = None
        x_lat_mixed = None
        if kind == 0:
            hp = (hy_w_in[j], hy_b_in[j], hy_conv_w[j], hy_conv_b[j], hy_f_w1[j], hy_f_b1[j],
                  hy_f_w2[j], hy_f_b2[j], hy_f_w3[j], hy_f_b3[j], hy_f_w4[j], hy_freq[j],
                  hy_bias[j], hy_w_out[j], hy_b_out[j])
            L = x_lat.shape[1]
            taps = hyena_filter_taps(L, *hp[4:12])
            spec = hyena_filter_spectrum(taps.reshape(HY_ORDER * D_MODEL, 2 * L), dft)
            spec = spec.reshape(HY_ORDER, D_MODEL, DFT_N1, 2 * DFT_N1)
            z_t = hyena_in_proj(x_lat, norm1_g[i], mod_lat, hy_w_in[j].T.astype(BF16), hy_b_in[j].reshape(-1, 1))
            y_t = hyena_long_conv(z_t, spec, hy_conv_w[j], hy_conv_b[j], hy_bias[j], dft)
            x_lat_mixed = hyena_out_proj(y_t, x_lat, mod_lat, hy_w_out[j].astype(BF16), hy_b_out[j])
            y_ctx = hyena_mixer(h_ctx, *hp) if ctx_live else None
        elif kind == 1:
            y_lat, y_ctx = gla_mixer(h_lat, h_ctx, gla_w_in[j], gla_gk_w2[j], gla_gk_b[j],
                                     gla_onorm[j], gla_wo[j], ctx_live)
        else:
            y_lat = mla_mixer(h_lat, h_ctx, mla_w_down[j], mla_qnorm[j], mla_w_uq[j],
                              mla_kvnorm[j], mla_w_ukv[j], mla_wo[j])
        w1 = mlp_w1[i].astype(BF16)
        w2 = mlp_w2[i].astype(BF16)
        x_lat = x_lat_mixed if x_lat_mixed is not None else x_lat + mod_l[2] * y_lat
        x_lat = mlp_block(x_lat, norm2_g[i], mod_lat, w1, w2,
                          final_g=final_g if i == DEPTH - 1 else None)
        if ctx_live:
            x_ctx = x_ctx + mod_c[2] * y_ctx
            x_ctx = mlp_block(x_ctx, norm2_g[i], mod_ctx, w1, w2)
    return x_lat
```

```python
import functools
import math

import jax
import jax.numpy as jnp
import numpy as np
from jax import lax
from jax.experimental import pallas as pl
from jax.experimental.pallas import tpu as pltpu

F32 = jnp.float32
BF16 = jnp.bfloat16

D_MODEL = 1024
DEPTH = 4
GRID_W = 64
N_MIXERS = 3
NORM_EPS = 1e-6

HY_ORDER = 2
HY_EMB = 33
HY_SHORT = 3
HY_FAST_DECAY = 0.3
HY_SLOW_DECAY = 1.5
HY_TARGET = 1e-2

GLA_HEADS = 4
GLA_DK = D_MODEL // 2
GLA_DV = D_MODEL
GLA_HK = GLA_DK // GLA_HEADS
GLA_HV = GLA_DV // GLA_HEADS
GLA_GATE_RANK = 16
GLA_GATE_NORM = 16.0
GLA_CHUNK = 64

MLA_HEADS = 8
MLA_Q_RANK = 384
MLA_KV_RANK = 256
MLA_NOPE = 128
MLA_ROPE = 64
MLA_V = 128
ROPE_THETA = 10000.0
Q_BLOCK = 128

VMEM_LIMIT_BYTES = 56 * 1024 * 1024
MLA_QK_PAD = 256
SUBLANES = 8


def _const_spec(shape):
    nd = len(shape)
    return pl.BlockSpec(shape, lambda *_: (0,) * nd, pipeline_mode=pl.Buffered(1))


def _norm_mod(x, g, shift, scale):
    y = x * lax.rsqrt(jnp.mean(x * x, axis=-1, keepdims=True) + NORM_EPS)
    return (y * g) * (1.0 + scale) + shift


def _mlp_kernel(x_ref, g_ref, mod_ref, w1_ref, w2_ref, fg_ref, o_ref, *, hidden_chunk, final_norm):
    x = x_ref[0]
    h = _norm_mod(x, g_ref[...], mod_ref[0, 3:4, :], mod_ref[0, 4:5, :]).astype(BF16)
    hidden = w1_ref.shape[1]
    acc = jnp.zeros(x.shape, F32)
    for c0 in range(0, hidden, hidden_chunk):
        a = jnp.dot(h, w1_ref[:, c0:c0 + hidden_chunk], preferred_element_type=F32)
        a = jnp.square(jnp.maximum(a, 0.0)).astype(BF16)
        acc = acc + jnp.dot(a, w2_ref[c0:c0 + hidden_chunk, :], preferred_element_type=F32)
    out = x + mod_ref[0, 5:6, :] * acc
    if final_norm:
        out = (out * lax.rsqrt(jnp.mean(out * out, axis=-1, keepdims=True) + NORM_EPS)) * fg_ref[...]
    o_ref[0] = out


def mlp_block(x, norm_g, mod, w1, w2, final_g=None):
    B, L, D = x.shape
    tm = min(512, L)
    per_batch = mod.shape[0] != 1
    final_norm = final_g is not None
    fg = (final_g if final_norm else norm_g).reshape(1, D)
    kern = functools.partial(_mlp_kernel, hidden_chunk=1024, final_norm=final_norm)
    return pl.pallas_call(
        kern,
        grid=(B, L // tm),
        in_specs=[
            pl.BlockSpec((1, tm, D), lambda b, i: (b, i, 0)),
            _const_spec((1, D)),
            pl.BlockSpec((1, 6, D), (lambda b, i: (b, 0, 0)) if per_batch else (lambda b, i: (0, 0, 0))),
            _const_spec(w1.shape),
            _const_spec(w2.shape),
            _const_spec((1, D)),
        ],
        out_specs=pl.BlockSpec((1, tm, D), lambda b, i: (b, i, 0)),
        out_shape=jax.ShapeDtypeStruct((B, L, D), F32),
        compiler_params=pltpu.CompilerParams(
            dimension_semantics=("arbitrary", "arbitrary"), vmem_limit_bytes=VMEM_LIMIT_BYTES),
        name="mlp_block",
    )(x, norm_g.reshape(1, D), mod, w1, w2, fg)


def _attn_kernel(q_ref, kc_ref, vc_ref, k_ref, v_ref, o_ref, m_ref, l_ref, acc_ref, sa_ref, sb_ref, *, sub, c):
    q = q_ref[0]
    m_ref[...] = jnp.full(m_ref.shape, -jnp.inf, F32)
    l_ref[...] = jnp.zeros(l_ref.shape, F32)
    acc_ref[...] = jnp.zeros(acc_ref.shape, F32)
    lanes = m_ref.shape[1]
    n_sub = k_ref.shape[1] // sub

    def scores(k):
        return lax.dot_general(q, k, (((1,), (1,)), ((), ())), preferred_element_type=F32)

    def lat(ref, n):
        return ref[0, pl.ds(pl.multiple_of(n * sub, sub), sub), :]

    def accumulate(s, v):
        m_prev = m_ref[...]
        m_new = jnp.maximum(m_prev, jnp.max(s, axis=-1, keepdims=True))
        alpha = jnp.exp2((m_prev - m_new) * c)
        ps = [jnp.exp2((s[:, t:t + lanes] - m_new) * c) for t in range(0, s.shape[1], lanes)]
        l_ref[...] = alpha * l_ref[...] + functools.reduce(lambda a, b: a + b, ps)
        p = jnp.concatenate(ps, axis=1).astype(BF16)
        acc_ref[...] = alpha * acc_ref[...] + jnp.dot(p, v, preferred_element_type=F32)
        m_ref[...] = m_new

    sa_ref[...] = scores(lat(k_ref, 0))
    accumulate(scores(kc_ref[0]), vc_ref[0])

    def pair(n):
        sb_ref[...] = scores(lat(k_ref, n + 1))
        accumulate(sa_ref[...], lat(v_ref, n))

    def body(j, carry):
        n = 2 * j
        pair(n)
        sa_ref[...] = scores(lat(k_ref, n + 2))
        accumulate(sb_ref[...], lat(v_ref, n + 1))
        return carry

    lax.fori_loop(0, n_sub // 2 - 1, body, 0)
    pair(n_sub - 2)
    accumulate(sb_ref[...], lat(v_ref, n_sub - 1))
    l = jnp.sum(l_ref[...], axis=-1, keepdims=True)
    o_ref[0] = (acc_ref[...] / l).astype(o_ref.dtype)


def mla_attention(q, kc, vc, k, v):
    B, L, _ = q.shape
    C = kc.shape[1]
    H = MLA_HEADS
    tq = min(1024, L)
    sub = min(512, L // 4)
    assert L % (2 * sub) == 0 and L % tq == 0
    kern = functools.partial(_attn_kernel, sub=sub,
                             c=(MLA_NOPE + MLA_ROPE) ** -0.5 * math.log2(math.e))
    return pl.pallas_call(
        kern,
        grid=(B, H, L // tq),
        in_specs=[
            pl.BlockSpec((1, tq, MLA_QK_PAD), lambda b, h, i: (b, i, h)),
            pl.BlockSpec((1, C, MLA_QK_PAD), lambda b, h, i: (b, 0, h)),
            pl.BlockSpec((1, C, MLA_V), lambda b, h, i: (b, 0, h)),
            pl.BlockSpec((1, L, MLA_QK_PAD), lambda b, h, i: (b, 0, h)),
            pl.BlockSpec((1, L, MLA_V), lambda b, h, i: (b, 0, h)),
        ],
        out_specs=pl.BlockSpec((1, tq, MLA_V), lambda b, h, i: (b, i, h)),
        out_shape=jax.ShapeDtypeStruct((B, L, H * MLA_V), BF16),
        scratch_shapes=[pltpu.VMEM((tq, MLA_V), F32), pltpu.VMEM((tq, MLA_V), F32), pltpu.VMEM((tq, MLA_V), F32),
                        pltpu.VMEM((tq, sub), F32), pltpu.VMEM((tq, sub), F32)],
        compiler_params=pltpu.CompilerParams(
            dimension_semantics=("arbitrary", "arbitrary", "arbitrary"), vmem_limit_bytes=VMEM_LIMIT_BYTES),
        name="mla_attention",
    )(q, kc, vc, k, v)


DFT_N1 = 128
HY_CH_BLOCK = 16
HY_TAP_ROWS = 64


def _dft_constants():
    n1 = DFT_N1
    n = n1 * n1
    idx = np.arange(n1, dtype=np.float64)
    th = 2.0 * np.pi * np.outer(idx, idx) / n1
    cos1, sin1 = np.cos(th), np.sin(th)
    tw = 2.0 * np.pi * np.outer(idx, idx) / n
    fr, fi = cos1, -sin1
    c = dict(
        e1=np.concatenate([cos1, -sin1], axis=0),
        tr=np.cos(tw), ti=-np.sin(tw),
        g=np.block([[fr, fi], [-fi, fr]]),
        gbar=np.block([[fr, -fi], [fi, fr]]),
        e2=np.concatenate([cos1, -sin1], axis=1) / n,
    )
    return {k: jnp.asarray(v, F32) for k, v in c.items()}


def _hy_in_kernel(x_ref, g_ref, mod_ref, wt_ref, b_ref, o_ref, *, row_chunk):
    h = _norm_mod(x_ref[0], g_ref[...], mod_ref[0, 0:1, :], mod_ref[0, 1:2, :]).astype(BF16)
    for r0 in range(0, wt_ref.shape[0], row_chunk):
        z = lax.dot_general(wt_ref[r0:r0 + row_chunk, :], h, (((1,), (1,)), ((), ())),
                            preferred_element_type=F32)
        o_ref[0, r0:r0 + row_chunk, :] = z + b_ref[r0:r0 + row_chunk, :]


def hyena_in_proj(x, norm_g, mod, w_in_t, b_in):
    B, L, D = x.shape
    n_out = w_in_t.shape[0]
    tm = min(512, L)
    per_batch = mod.shape[0] != 1
    return pl.pallas_call(
        functools.partial(_hy_in_kernel, row_chunk=512),
        grid=(B, L // tm),
        in_specs=[
            pl.BlockSpec((1, tm, D), lambda b, i: (b, i, 0)),
            _const_spec((1, D)),
            pl.BlockSpec((1, 6, D), (lambda b, i: (b, 0, 0)) if per_batch else (lambda b, i: (0, 0, 0))),
            _const_spec(w_in_t.shape),
            _const_spec(b_in.shape),
        ],
        out_specs=pl.BlockSpec((1, n_out, tm), lambda b, i: (b, 0, i)),
        out_shape=jax.ShapeDtypeStruct((B, n_out, L), F32),
        compiler_params=pltpu.CompilerParams(
            dimension_semantics=("arbitrary", "arbitrary"), vmem_limit_bytes=VMEM_LIMIT_BYTES),
        name="hyena_in_proj",
    )(x, norm_g.reshape(1, D), mod, w_in_t, b_in)


def _hy_hidden_kernel(zf_ref, w1_ref, b1_ref, w2_ref, b2_ref, w3_ref, b3_ref, fr_ref, o_ref):
    fr = fr_ref[...]
    h = zf_ref[...].astype(BF16)
    for w_ref, b_ref in ((w1_ref, b1_ref), (w2_ref, b2_ref), (w3_ref, b3_ref)):
        h = jnp.sin(fr * (jnp.dot(w_ref[...], h, preferred_element_type=F32) + b_ref[...]))
        out = h
        h = h.astype(BF16)
    o_ref[...] = out


def _hy_taps_kernel(hid_ref, t_ref, w4_ref, dl_ref, o_ref, *, half):
    hid = hid_ref[...].astype(BF16)
    tf = jnp.dot(w4_ref[0, 0].astype(BF16), hid[:, :half], preferred_element_type=F32)
    tb = jnp.dot(w4_ref[0, 1].astype(BF16), hid[:, half:], preferred_element_type=F32)
    taps = jnp.concatenate([tf, tb], axis=1) * jnp.exp(-t_ref[...] * dl_ref[...])
    pos = lax.broadcasted_iota(jnp.int32, taps.shape, 1)
    taps = jnp.where(pos == half, 0.0, taps)
    o_ref[0] = taps / jnp.sum(jnp.abs(taps), axis=1, keepdims=True)


def hyena_filter_taps(L, f_w1, f_b1, f_w2, f_b2, f_w3, f_b3, f_w4, freq):
    width = f_w1.shape[1]
    n = 2 * L
    pos = np.arange(n)
    pos = np.where(pos <= L, np.minimum(pos, L - 1), n - pos).astype(np.float64)
    bands = (HY_EMB - 1) // 2
    t = jnp.asarray(pos / (L - 1), F32)[None, :]
    w = 2.0 * math.pi * jnp.asarray(pos, F32)[None, :] / L
    f = jnp.linspace(1e-4, bands - 1, bands, dtype=F32)[:, None]
    zf = jnp.concatenate([t, jnp.cos(f * w), -jnp.sin(f * w)], axis=0)
    zf = jnp.pad(zf, ((0, width - HY_EMB), (0, 0)))
    w1t = jnp.pad(f_w1.T, ((0, 0), (0, width - HY_EMB))).astype(BF16)
    col = lambda v: v.reshape(width, 1).astype(F32)
    lane_blk = min(2048, n)
    hidden = pl.pallas_call(
        _hy_hidden_kernel,
        grid=(n // lane_blk,),
        in_specs=[pl.BlockSpec((width, lane_blk), lambda i: (0, i))] + [_const_spec((width, width)), _const_spec((width, 1))] * 3
        + [_const_spec((width, 1))],
        out_specs=pl.BlockSpec((width, lane_blk), lambda i: (0, i)),
        out_shape=jax.ShapeDtypeStruct((width, n), F32),
        name="hyena_filter_hidden",
    )(zf, w1t, col(f_b1), f_w2.T.astype(BF16), col(f_b2), f_w3.T.astype(BF16), col(f_b3), col(freq))
    max_decay = math.log(HY_TARGET) / HY_FAST_DECAY
    min_decay = math.log(HY_TARGET) / HY_SLOW_DECAY
    deltas = jnp.abs(jnp.linspace(min_decay, max_decay, D_MODEL, dtype=F32)).reshape(D_MODEL, 1)
    w4t = f_w4.T.reshape(HY_ORDER, 2, D_MODEL, width)
    rows = HY_TAP_ROWS
    return pl.pallas_call(
        functools.partial(_hy_taps_kernel, half=L),
        grid=(HY_ORDER, D_MODEL // rows),
        in_specs=[
            _const_spec((width, n)),
            _const_spec((1, n)),
            pl.BlockSpec((1, 2, rows, width), lambda o, i: (o, 0, i, 0)),
            pl.BlockSpec((rows, 1), lambda o, i: (i, 0)),
        ],
        out_specs=pl.BlockSpec((1, rows, n), lambda o, i: (o, i, 0)),
        out_shape=jax.ShapeDtypeStruct((HY_ORDER, D_MODEL, n), F32),
        compiler_params=pltpu.CompilerParams(
            dimension_semantics=("arbitrary", "arbitrary"), vmem_limit_bytes=VMEM_LIMIT_BYTES),
        name="hyena_filter_taps",
    )(hidden, t, w4t, deltas)


def _forward_dft(load_x, e1, tr, ti, g_ref, zb_ref, n_ch):
    n1 = DFT_N1

    def stage1(c, carry):
        z = jnp.dot(e1, load_x(c), preferred_element_type=F32)
        zr, zi = z[:n1], z[n1:]
        r0 = pl.multiple_of(c * n1, n1)
        zb_ref[pl.ds(r0, n1), 0:n1] = (zr * tr - zi * ti).astype(BF16)
        zb_ref[pl.ds(r0, n1), n1:2 * n1] = (zr * ti + zi * tr).astype(BF16)
        return carry

    lax.fori_loop(0, n_ch, stage1, 0, unroll=8)
    return jnp.dot(zb_ref[...], g_ref[...], preferred_element_type=F32)


def _hy_spectrum_kernel(x_ref, e1_ref, tr_ref, ti_ref, g_ref, o_ref, zb_ref):
    n_ch = x_ref.shape[0]
    e1, tr, ti = e1_ref[...], tr_ref[...], ti_ref[...]
    xh = _forward_dft(lambda c: x_ref[c].astype(BF16), e1, tr, ti, g_ref, zb_ref, n_ch)
    o_ref[...] = xh.reshape(o_ref.shape)


def hyena_filter_spectrum(taps, consts):
    R = taps.shape[0]
    n1 = DFT_N1
    C = HY_CH_BLOCK
    x = taps.reshape(R, n1, n1)
    return pl.pallas_call(
        _hy_spectrum_kernel,
        grid=(R // C,),
        in_specs=[pl.BlockSpec((C, n1, n1), lambda i: (i, 0, 0)), _const_spec((2 * n1, n1)),
                  _const_spec((n1, n1)), _const_spec((n1, n1)), _const_spec((2 * n1, 2 * n1))],
        out_specs=pl.BlockSpec((C, n1, 2 * n1), lambda i: (i, 0, 0)),
        out_shape=jax.ShapeDtypeStruct((R, n1, 2 * n1), F32),
        scratch_shapes=[pltpu.VMEM((C * n1, 2 * n1), BF16)],
        compiler_params=pltpu.CompilerParams(
            dimension_semantics=("arbitrary",), vmem_limit_bytes=VMEM_LIMIT_BYTES),
        name="hyena_filter_spectrum",
    )(x, consts["e1"].astype(BF16), consts["tr"], consts["ti"], consts["g"].astype(BF16))


def _hy_conv_kernel(cw_ref, cb_ref, sk_ref, zy_ref, zg1_ref, zg2_ref, hh_ref, e1_ref, tr_ref, ti_ref,
                    g_ref, gb_ref, e2_ref, o_ref, y_sc, g1_sc, g2_sc, zb_sc, u_sc, pad_sc, *, n_ch, d_model):
    n1 = DFT_N1
    rows = zy_ref.shape[2]
    ch0 = pl.program_id(0) * n_ch
    b_idx = lax.broadcasted_iota(jnp.int32, (rows, n1), 1)
    e1, tr, ti, e2 = e1_ref[...], tr_ref[...], ti_ref[...], e2_ref[...]

    pad = pad_sc.shape[1] - rows
    top = pad // 2
    zero_rows = jnp.zeros((top, n1), F32)

    def short_conv(z_ref, c, slot, col):
        z = z_ref[0, c]
        pad_sc[slot, 0:top, :] = zero_rows
        pad_sc[slot, top + rows:pad + rows, :] = zero_rows
        pad_sc[slot, top:top + rows, :] = z
        up = pad_sc[slot, top - 1:top - 1 + rows, :]
        down = pad_sc[slot, top + 1:top + 1 + rows, :]
        prev = pltpu.roll(jnp.where(b_idx == n1 - 1, up, z), 1, axis=1)
        nxt = pltpu.roll(jnp.where(b_idx == 0, down, z), n1 - 1, axis=1)
        n_col = 3 * d_model
        return cw_ref[col] * prev + cw_ref[n_col + col] * z + cw_ref[2 * n_col + col] * nxt + cb_ref[col]

    def prep(c, carry):
        y_sc[c] = short_conv(zy_ref, c, 0, ch0 + c)
        g1_sc[c] = short_conv(zg1_ref, c, 1, d_model + ch0 + c)
        g2_sc[c] = short_conv(zg2_ref, c, 2, 2 * d_model + ch0 + c)
        return carry

    lax.fori_loop(0, n_ch, prep, 0, unroll=4)

    for order, gate_sc in enumerate((g1_sc, g2_sc)):
        xh = _forward_dft(lambda c: y_sc[c].astype(BF16), e1, tr, ti, g_ref, zb_sc, n_ch)
        hh = hh_ref[order].reshape(n_ch * n1, 2 * n1)
        xr, xi, hr, hi = xh[:, :n1], xh[:, n1:], hh[:, :n1], hh[:, n1:]
        yh = jnp.concatenate([xr * hr - xi * hi, xr * hi + xi * hr], axis=1).astype(BF16)
        u_sc[...] = jnp.dot(yh, gb_ref[...], preferred_element_type=F32)

        def finish(c, carry, order=order, gate_sc=gate_sc):
            r0 = pl.multiple_of(c * n1, n1)
            ur, ui = u_sc[pl.ds(r0, n1), 0:n1], u_sc[pl.ds(r0, n1), n1:2 * n1]
            stacked = jnp.concatenate([ur * tr + ui * ti, ui * tr - ur * ti], axis=0).astype(BF16)
            conv = jnp.dot(e2, stacked, preferred_element_type=F32)
            y = y_sc[c]
            y = gate_sc[c] * (conv + y * sk_ref[order * d_model + ch0 + c])
            if order == HY_ORDER - 1:
                o_ref[0, c] = y.astype(o_ref.dtype)
            else:
                y_sc[c] = y
            return carry

        lax.fori_loop(0, n_ch, finish, 0, unroll=8)


def hyena_long_conv(z_t, spectrum, conv_w, conv_b, skip, consts):
    B, n_col, L = z_t.shape
    D = n_col // (HY_ORDER + 1)
    n1 = DFT_N1
    rows = L // n1
    assert 2 * rows == n1 and HY_ORDER == 2
    C = HY_CH_BLOCK
    nblk = D // C
    z4 = z_t.reshape(B, n_col, rows, n1)
    smem = pl.BlockSpec(memory_space=pltpu.SMEM)
    zspec = lambda off: pl.BlockSpec((1, C, rows, n1), lambda i, b: (b, i + off * nblk, 0, 0))
    kern = functools.partial(_hy_conv_kernel, n_ch=C, d_model=D)
    out = pl.pallas_call(
        kern,
        grid=(nblk, B),
        in_specs=[smem, smem, smem, zspec(0), zspec(1), zspec(2),
                  pl.BlockSpec((HY_ORDER, C, n1, 2 * n1), lambda i, b: (0, i, 0, 0)),
                  _const_spec((2 * n1, rows)), _const_spec((n1, n1)), _const_spec((n1, n1)),
                  _const_spec((2 * n1, 2 * n1)), _const_spec((2 * n1, 2 * n1)), _const_spec((rows, 2 * n1))],
        out_specs=pl.BlockSpec((1, C, rows, n1), lambda i, b: (b, i, 0, 0)),
        out_shape=jax.ShapeDtypeStruct((B, D, rows, n1), BF16),
        scratch_shapes=[pltpu.VMEM((C, rows, n1), F32), pltpu.VMEM((C, rows, n1), F32), pltpu.VMEM((C, rows, n1), F32),
                        pltpu.VMEM((C * n1, 2 * n1), BF16), pltpu.VMEM((C * n1, 2 * n1), F32),
                        pltpu.VMEM((HY_ORDER + 1, rows + 2 * SUBLANES, n1), F32)],
        compiler_params=pltpu.CompilerParams(
            dimension_semantics=("arbitrary", "arbitrary"), vmem_limit_bytes=VMEM_LIMIT_BYTES),
        name="hyena_long_conv",
    )(conv_w.reshape(-1), conv_b.reshape(-1), skip.reshape(-1), z4, z4, z4, spectrum,
      consts["e1"][:, :rows].astype(BF16), consts["tr"], consts["ti"], consts["g"].astype(BF16),
      consts["gbar"].astype(BF16), consts["e2"][:rows].astype(BF16))
    return out.reshape(B, D, L)


def _hy_short_seq_kernel(zy_ref, zg1_ref, zg2_ref, taps_ref, par_ref, dfull_ref, dinv_ref, o_ref):
    rows, L = zy_ref.shape[1], zy_ref.shape[2]
    lane = lax.broadcasted_iota(jnp.int32, (rows, L), 1)
    par = par_ref[...]
    nfreq = dinv_ref.shape[0] // 2

    def short_conv(z, grp):
        prev = jnp.where(lane == 0, 0.0, pltpu.roll(z, 1, axis=1))
        nxt = jnp.where(lane == L - 1, 0.0, pltpu.roll(z, L - 1, axis=1))
        c = 4 * grp
        return par[:, c:c + 1] * prev + par[:, c + 1:c + 2] * z + par[:, c + 2:c + 3] * nxt + par[:, c + 3:c + 4]

    y = short_conv(zy_ref[0], 0)
    gates = (short_conv(zg1_ref[0], 1), short_conv(zg2_ref[0], 2))
    d_first = dfull_ref[0:L, :]
    for order in range(HY_ORDER):
        hh = jnp.dot(taps_ref[order].astype(BF16), dfull_ref[...], preferred_element_type=F32)
        xh = jnp.dot(y.astype(BF16), d_first, preferred_element_type=F32)
        xr, xi, hr, hi = xh[:, :nfreq], xh[:, nfreq:], hh[:, :nfreq], hh[:, nfreq:]
        yh = jnp.concatenate([xr * hr - xi * hi, xr * hi + xi * hr], axis=1).astype(BF16)
        conv = jnp.dot(yh, dinv_ref[...], preferred_element_type=F32)
        y = gates[order] * (conv + y * par[:, 12 + order:13 + order])
    o_ref[0] = y.astype(o_ref.dtype)


def hyena_short_seq_conv(z_t, taps, conv_w, conv_b, skip):
    B, n_col, L = z_t.shape
    D = n_col // (HY_ORDER + 1)
    n = 2 * L
    idx = np.arange(n, dtype=np.float64)
    ang = 2.0 * np.pi * np.outer(idx, idx) / n
    dfull = jnp.asarray(np.concatenate([np.cos(ang), -np.sin(ang)], axis=1), F32)
    dinv = jnp.asarray(np.concatenate([np.cos(ang), -np.sin(ang)], axis=0)[:, :L] / n, F32)
    cw = conv_w.reshape(HY_SHORT, HY_ORDER + 1, D)
    cb = conv_b.reshape(1, HY_ORDER + 1, D)
    par = jnp.concatenate([cw, cb], axis=0)
    par = jnp.transpose(par, (2, 1, 0)).reshape(D, 4 * (HY_ORDER + 1))
    par = jnp.concatenate([par, skip.T, jnp.zeros((D, 2), F32)], axis=1)
    rows = 256
    nblk = D // rows
    zspec = lambda off: pl.BlockSpec((1, rows, L), lambda i, b: (b, i + off * nblk, 0))
    return pl.pallas_call(
        _hy_short_seq_kernel,
        grid=(nblk, B),
        in_specs=[zspec(0), zspec(1), zspec(2),
                  pl.BlockSpec((HY_ORDER, rows, n), lambda i, b: (0, i, 0)),
                  pl.BlockSpec((rows, par.shape[1]), lambda i, b: (i, 0)),
                  _const_spec(dfull.shape), _const_spec(dinv.shape)],
        out_specs=pl.BlockSpec((1, rows, L), lambda i, b: (b, i, 0)),
        out_shape=jax.ShapeDtypeStruct((B, D, L), BF16),
        compiler_params=pltpu.CompilerParams(
            dimension_semantics=("arbitrary", "arbitrary"), vmem_limit_bytes=VMEM_LIMIT_BYTES),
        name="hyena_short_seq_conv",
    )(z_t, z_t, z_t, taps, par, dfull.astype(BF16), dinv.astype(BF16))


def hyena_layer(x, mod, norm_g, w_in, b_in, conv_w, conv_b, f_w1, f_b1, f_w2, f_b2, f_w3, f_b3, f_w4,
                freq, bias, w_out, b_out, dft):
    L = x.shape[1]
    taps = hyena_filter_taps(L, f_w1, f_b1, f_w2, f_b2, f_w3, f_b3, f_w4, freq)
    z_t = hyena_in_proj(x, norm_g, mod, w_in.T.astype(BF16), b_in.reshape(-1, 1))
    if 2 * L == DFT_N1 * DFT_N1:
        spec = hyena_filter_spectrum(taps.reshape(HY_ORDER * D_MODEL, 2 * L), dft)
        spec = spec.reshape(HY_ORDER, D_MODEL, DFT_N1, 2 * DFT_N1)
        y_t = hyena_long_conv(z_t, spec, conv_w, conv_b, bias, dft)
    else:
        y_t = hyena_short_seq_conv(z_t, taps, conv_w, conv_b, bias)
    return hyena_out_proj(y_t, x, mod, w_out.astype(BF16), b_out)


def _hy_out_kernel(y_ref, x_ref, mod_ref, w_ref, b_ref, o_ref):
    acc = lax.dot_general(y_ref[0], w_ref[...], (((0,), (0,)), ((), ())), preferred_element_type=F32)
    o_ref[0] = x_ref[0] + mod_ref[0, 2:3, :] * (acc + b_ref[...])


def hyena_out_proj(y_t, x, mod, w_out, b_out):
    B, L, D = x.shape
    tm = min(512, L)
    per_batch = mod.shape[0] != 1
    return pl.pallas_call(
        _hy_out_kernel,
        grid=(B, L // tm),
        in_specs=[
            pl.BlockSpec((1, D, tm), lambda b, i: (b, 0, i)),
            pl.BlockSpec((1, tm, D), lambda b, i: (b, i, 0)),
            pl.BlockSpec((1, 6, D), (lambda b, i: (b, 0, 0)) if per_batch else (lambda b, i: (0, 0, 0))),
            _const_spec(w_out.shape),
            _const_spec((1, D)),
        ],
        out_specs=pl.BlockSpec((1, tm, D), lambda b, i: (b, i, 0)),
        out_shape=jax.ShapeDtypeStruct((B, L, D), F32),
        compiler_params=pltpu.CompilerParams(
            dimension_semantics=("arbitrary", "arbitrary"), vmem_limit_bytes=VMEM_LIMIT_BYTES),
        name="hyena_out_proj",
    )(y_t, x, mod, w_out, b_out.reshape(1, D))


def _gla_in_kernel(x_ref, g_ref, mod_ref, w_ref, wr_ref, w2_ref, gb_ref, qk_ref, v_ref, og_ref, gate_ref):
    h = _norm_mod(x_ref[0], g_ref[...], mod_ref[0, 0:1, :], mod_ref[0, 1:2, :]).astype(BF16)
    n = qk_ref.shape[2]
    qk = jnp.dot(h, w_ref[:, 0:n], preferred_element_type=F32)
    half = n // 2
    qk_ref[0, :, 0:half] = qk[:, 0:half] * (GLA_HK ** -0.5)
    qk_ref[0, :, half:n] = qk[:, half:n]
    v_ref[0] = jnp.dot(h, w_ref[:, n:2 * n], preferred_element_type=F32).astype(v_ref.dtype)
    og_ref[0] = jnp.dot(h, w_ref[:, 2 * n:3 * n], preferred_element_type=F32)
    r = jnp.dot(h, wr_ref[...], preferred_element_type=F32).astype(BF16)
    gk = jnp.dot(r, w2_ref[...], preferred_element_type=F32) + gb_ref[...]
    gate_ref[0] = -(jnp.maximum(-gk, 0.0) + jnp.log1p(jnp.exp(-jnp.abs(gk)))) * (1.0 / GLA_GATE_NORM)


def gla_in_proj(x, norm_g, mod, w_in, gk_w2, gk_b):
    B, L, D = x.shape
    tm = min(512, L)
    per_batch = mod.shape[0] != 1
    n_main = 2 * GLA_DK + 2 * GLA_DV
    w_main = w_in[:, :n_main].astype(BF16)
    lanes = 128
    w_r = jnp.pad(w_in[:, n_main:], ((0, 0), (0, lanes - 2 * GLA_GATE_RANK))).astype(BF16)
    w2 = jnp.zeros((lanes, 2 * GLA_DK), F32)
    w2 = w2.at[:GLA_GATE_RANK, :GLA_DK].set(gk_w2[0]).at[GLA_GATE_RANK:2 * GLA_GATE_RANK, GLA_DK:].set(gk_w2[1])
    tok = lambda n: pl.BlockSpec((1, tm, n), lambda b, i: (b, i, 0))
    n = 2 * GLA_DK
    assert GLA_DV == n
    return pl.pallas_call(
        _gla_in_kernel,
        grid=(B, L // tm),
        in_specs=[tok(D), _const_spec((1, D)),
                  pl.BlockSpec((1, 6, D), (lambda b, i: (b, 0, 0)) if per_batch else (lambda b, i: (0, 0, 0))),
                  _const_spec(w_main.shape), _const_spec(w_r.shape), _const_spec(w2.shape), _const_spec((1, n))],
        out_specs=[tok(n), tok(n), tok(n), tok(n)],
        out_shape=[jax.ShapeDtypeStruct((B, L, n), F32), jax.ShapeDtypeStruct((B, L, n), BF16),
                   jax.ShapeDtypeStruct((B, L, n), F32), jax.ShapeDtypeStruct((B, L, n), F32)],
        compiler_params=pltpu.CompilerParams(
            dimension_semantics=("arbitrary", "arbitrary"), vmem_limit_bytes=VMEM_LIMIT_BYTES),
        name="gla_in_proj",
    )(x, norm_g.reshape(1, D), mod, w_main, w_r, w2.astype(BF16), gk_b.reshape(1, n))


def _gla_scan_kernel(*refs, reverse, add_prev):
    if add_prev:
        qk_ref, v_ref, g_ref, s0_ref, prev_ref, o_ref, sfin_ref, st_ref = refs
    else:
        qk_ref, v_ref, g_ref, s0_ref, o_ref, sfin_ref, st_ref = refs
        prev_ref = None
    i = pl.program_id(1)
    C, H, dk, dv = GLA_CHUNK, GLA_HEADS, GLA_HK, GLA_HV

    @pl.when(i == 0)
    def _():
        st_ref[...] = s0_ref[0]

    r_idx = lax.broadcasted_iota(jnp.int32, (C, C), 0)
    c_idx = lax.broadcasted_iota(jnp.int32, (C, C), 1)
    keep = (r_idx <= c_idx) if reverse else (r_idx >= c_idx)
    tri = keep.astype(F32)
    n_chunks = qk_ref.shape[1] // C
    order = range(n_chunks - 1, -1, -1) if reverse else range(n_chunks)
    for ci in order:
        rows = slice(ci * C, (ci + 1) * C)
        b = jnp.dot(tri, g_ref[0, rows, :], precision=lax.Precision.HIGHEST, preferred_element_type=F32)
        b_last = b[0:1] if reverse else b[C - 1:C]
        e_pos, e_neg, e_end, dec = jnp.exp(b), jnp.exp(-b), jnp.exp(b_last - b), jnp.exp(b_last)
        for h in range(H):
            kc = slice(h * dk, (h + 1) * dk)
            vc = slice(h * dv, (h + 1) * dv)
            q = qk_ref[0, rows, kc]
            k = qk_ref[0, rows, H * dk + h * dk:H * dk + (h + 1) * dk]
            v = v_ref[0, rows, vc]
            q_t = (q * e_pos[:, kc]).astype(BF16)
            k_t = (k * e_neg[:, kc]).astype(BF16)
            k_end = (k * e_end[:, kc]).astype(BF16)
            att = lax.dot_general(q_t, k_t, (((1,), (1,)), ((), ())), preferred_element_type=F32)
            att = jnp.where(keep, att, 0.0).astype(BF16)
            st = st_ref[h]
            o = jnp.dot(att, v, preferred_element_type=F32) + lax.dot_general(
                q_t, st.astype(BF16), (((1,), (1,)), ((), ())), preferred_element_type=F32)
            st_ref[h] = st * dec[:, kc] + lax.dot_general(
                v, k_end, (((0,), (0,)), ((), ())), preferred_element_type=F32)
            if prev_ref is not None:
                o = o + prev_ref[0, rows, vc]
            o_ref[0, rows, vc] = o

    @pl.when(i == pl.num_programs(1) - 1)
    def _():
        sfin_ref[0] = st_ref[...]


def gla_scan(qk, v, gates, s0, direction, prev=None):
    B, L, _ = qk.shape
    reverse = direction == 1
    T = min(512, L)
    nT = L // T
    H, dk, dv = GLA_HEADS, GLA_HK, GLA_HV
    blk = (lambda i: nT - 1 - i) if reverse else (lambda i: i)
    tok = lambda n, col=0: pl.BlockSpec((1, T, n), lambda b, i: (b, blk(i), col))
    st_spec = pl.BlockSpec((1, H, dv, dk), lambda b, i: (b, 0, 0, 0))
    in_specs = [tok(2 * GLA_DK), tok(GLA_DV), tok(GLA_DK, direction), st_spec]
    args = [qk, v, gates, s0]
    if prev is not None:
        in_specs.append(tok(GLA_DV))
        args.append(prev)
    kern = functools.partial(_gla_scan_kernel, reverse=reverse, add_prev=prev is not None)
    return pl.pallas_call(
        kern,
        grid=(B, nT),
        in_specs=in_specs,
        out_specs=[tok(GLA_DV), st_spec],
        out_shape=[jax.ShapeDtypeStruct((B, L, GLA_DV), F32), jax.ShapeDtypeStruct((B, H, dv, dk), F32)],
        scratch_shapes=[pltpu.VMEM((H, dv, dk), F32)],
        compiler_params=pltpu.CompilerParams(
            dimension_semantics=("arbitrary", "arbitrary"), vmem_limit_bytes=VMEM_LIMIT_BYTES),
        name="gla_scan_bwd" if reverse else "gla_scan_fwd",
    )(*args)


def _gla_out_kernel(o_ref, og_ref, x_ref, mod_ref, on_ref, w_ref, out_ref):
    dv = GLA_HV
    og = og_ref[0]
    parts = []
    for h in range(GLA_HEADS):
        o = o_ref[0, :, h * dv:(h + 1) * dv]
        y = o * lax.rsqrt(jnp.mean(o * o, axis=-1, keepdims=True) + NORM_EPS) * on_ref[...]
        gate = og[:, h * dv:(h + 1) * dv]
        parts.append((y * (gate * jax.nn.sigmoid(gate))).astype(BF16))
    a = jnp.concatenate(parts, axis=1)
    out_ref[0] = x_ref[0] + mod_ref[0, 2:3, :] * jnp.dot(a, w_ref[...], preferred_element_type=F32)


def gla_out_proj(o, og, x, mod, onorm, wo):
    B, L, D = x.shape
    tm = min(512, L)
    per_batch = mod.shape[0] != 1
    tok = pl.BlockSpec((1, tm, D), lambda b, i: (b, i, 0))
    return pl.pallas_call(
        _gla_out_kernel,
        grid=(B, L // tm),
        in_specs=[tok, tok, tok,
                  pl.BlockSpec((1, 6, D), (lambda b, i: (b, 0, 0)) if per_batch else (lambda b, i: (0, 0, 0))),
                  _const_spec((1, GLA_HV)), _const_spec(wo.shape)],
        out_specs=tok,
        out_shape=jax.ShapeDtypeStruct((B, L, D), F32),
        compiler_params=pltpu.CompilerParams(
            dimension_semantics=("arbitrary", "arbitrary"), vmem_limit_bytes=VMEM_LIMIT_BYTES),
        name="gla_out_proj",
    )(o, og, x, mod, onorm.reshape(1, GLA_HV), wo)


def gla_layer(x_lat, x_ctx, mod_lat, mod_ctx, norm_g, w_in, gk_w2, gk_b, onorm, wo, ctx_out):
    B = x_lat.shape[0]
    qk_l, v_l, og_l, g_l = gla_in_proj(x_lat, norm_g, mod_lat, w_in, gk_w2, gk_b)
    qk_c, v_c, og_c, g_c = gla_in_proj(x_ctx, norm_g, mod_ctx, w_in, gk_w2, gk_b)
    s0 = jnp.zeros((B, GLA_HEADS, GLA_HV, GLA_HK), F32)
    oc, s_f = gla_scan(qk_c, v_c, g_c, s0, 0)
    oc, s_b = gla_scan(qk_c, v_c, g_c, s0, 1, prev=oc)
    ol, _ = gla_scan(qk_l, v_l, g_l, s_f, 0)
    ol, _ = gla_scan(qk_l, v_l, g_l, s_b, 1, prev=ol)
    wo = wo.astype(BF16)
    x_lat = gla_out_proj(ol, og_l, x_lat, mod_lat, onorm, wo)
    if ctx_out:
        x_ctx = gla_out_proj(oc, og_c, x_ctx, mod_ctx, onorm, wo)
    return x_lat, x_ctx


def _rope_swap_perm():
    half = MLA_ROPE // 2
    quarter = half // 2
    p = []
    for base in (0, half):
        p += list(range(base + quarter, base + half)) + list(range(base, base + quarter))
    return np.asarray(p)


def mla_rope_tables(L, rotate):
    half = MLA_ROPE // 2
    zeros = jnp.zeros((L, MLA_ROPE), F32)
    if not rotate:
        return jnp.concatenate([jnp.ones((L, MLA_ROPE), F32), zeros], axis=1), jnp.zeros((L, 2 * MLA_ROPE), F32)
    pos = jnp.arange(L)
    inv_freq = ROPE_THETA ** (-jnp.arange(0, half, 2, dtype=F32) / half)
    ang_row = (pos // GRID_W).astype(F32)[:, None] * inv_freq[None, :]
    ang_col = (pos % GRID_W).astype(F32)[:, None] * inv_freq[None, :]
    cr, sr, cc, sc = jnp.cos(ang_row), jnp.sin(ang_row), jnp.cos(ang_col), jnp.sin(ang_col)
    cos = jnp.concatenate([cr, cr, cc, cc, zeros], axis=1)
    sin = jnp.concatenate([-sr, sr, -sc, sc, zeros], axis=1)
    return cos, sin


def _mla_qkv_kernel(x_ref, g_ref, mod_ref, wd_ref, qn_ref, wq_ref, kn_ref, wkv_ref, cos_ref, sin_ref,
                    q_ref, k_ref, v_ref):
    h = _norm_mod(x_ref[0], g_ref[...], mod_ref[0, 0:1, :], mod_ref[0, 1:2, :]).astype(BF16)
    c = jnp.dot(h, wd_ref[...], preferred_element_type=F32)
    cos, sin = cos_ref[...], sin_ref[...]
    lanes = cos.shape[1]

    def rms(a, g):
        return (a * lax.rsqrt(jnp.mean(a * a, axis=-1, keepdims=True) + NORM_EPS) * g).astype(BF16)

    def rope(tile):
        return tile * cos + pltpu.roll(tile, lanes // 2, axis=1) * sin

    cq = rms(c[:, :MLA_Q_RANK], qn_ref[...])
    ckv = rms(c[:, MLA_Q_RANK:MLA_Q_RANK + MLA_KV_RANK], kn_ref[...])
    k_rope = rope(c[:, MLA_Q_RANK + MLA_KV_RANK:]).astype(k_ref.dtype)
    q = jnp.dot(cq, wq_ref[...], preferred_element_type=F32)
    kv = jnp.dot(ckv, wkv_ref[...], preferred_element_type=F32)
    for hd in range(MLA_HEADS):
        o = hd * MLA_QK_PAD
        q_ref[0, :, o:o + MLA_NOPE] = q[:, o:o + MLA_NOPE].astype(q_ref.dtype)
        q_ref[0, :, o + MLA_NOPE:o + MLA_QK_PAD] = rope(q[:, o + MLA_NOPE:o + MLA_QK_PAD]).astype(q_ref.dtype)
        k_ref[0, :, o:o + MLA_NOPE] = kv[:, o:o + MLA_NOPE].astype(k_ref.dtype)
        k_ref[0, :, o + MLA_NOPE:o + MLA_QK_PAD] = k_rope
        v_ref[0, :, hd * MLA_V:(hd + 1) * MLA_V] = kv[:, o + MLA_NOPE:o + MLA_QK_PAD].astype(v_ref.dtype)


def mla_qkv_proj(x, norm_g, mod, w_down, qnorm, w_uq, kvnorm, w_ukv, rotate):
    B, L, D = x.shape
    tm = min(512, L)
    per_batch = mod.shape[0] != 1
    perm = _rope_swap_perm()
    rope0 = MLA_Q_RANK + MLA_KV_RANK
    wd = jnp.concatenate([w_down, w_down[:, rope0:][:, perm]], axis=1).astype(BF16)
    wq = w_uq.reshape(MLA_Q_RANK, MLA_HEADS, MLA_NOPE + MLA_ROPE)
    wq = jnp.concatenate([wq, wq[:, :, MLA_NOPE:][:, :, perm]], axis=2)
    wq = wq.reshape(MLA_Q_RANK, MLA_HEADS * MLA_QK_PAD).astype(BF16)
    cos, sin = mla_rope_tables(L, rotate)
    tok = lambda n: pl.BlockSpec((1, tm, n), lambda b, i: (b, i, 0))
    nq = MLA_HEADS * MLA_QK_PAD
    nv = MLA_HEADS * MLA_V
    tab = pl.BlockSpec((tm, 2 * MLA_ROPE), lambda b, i: (i, 0))
    return pl.pallas_call(
        _mla_qkv_kernel,
        grid=(B, L // tm),
        in_specs=[tok(D), _const_spec((1, D)),
                  pl.BlockSpec((1, 6, D), (lambda b, i: (b, 0, 0)) if per_batch else (lambda b, i: (0, 0, 0))),
                  _const_spec(wd.shape), _const_spec((1, MLA_Q_RANK)), _const_spec(wq.shape),
                  _const_spec((1, MLA_KV_RANK)), _const_spec(w_ukv.shape), tab, tab],
        out_specs=[tok(nq), tok(nq), tok(nv)],
        out_shape=[jax.ShapeDtypeStruct((B, L, nq), BF16), jax.ShapeDtypeStruct((B, L, nq), BF16),
                   jax.ShapeDtypeStruct((B, L, nv), BF16)],
        compiler_params=pltpu.CompilerParams(
            dimension_semantics=("arbitrary", "arbitrary"), vmem_limit_bytes=VMEM_LIMIT_BYTES),
        name="mla_qkv_proj",
    )(x, norm_g.reshape(1, D), mod, wd, qnorm.reshape(1, -1), wq, kvnorm.reshape(1, -1),
      w_ukv.astype(BF16), cos, sin)


def _out_proj_kernel(a_ref, x_ref, mod_ref, w_ref, o_ref):
    o_ref[0] = x_ref[0] + mod_ref[0, 2:3, :] * jnp.dot(a_ref[0], w_ref[...], preferred_element_type=F32)


def out_proj_residual(a, x, mod, w):
    B, L, D = x.shape
    Kd = a.shape[2]
    tm = min(512, L)
    per_batch = mod.shape[0] != 1
    return pl.pallas_call(
        _out_proj_kernel,
        grid=(B, L // tm),
        in_specs=[pl.BlockSpec((1, tm, Kd), lambda b, i: (b, i, 0)),
                  pl.BlockSpec((1, tm, D), lambda b, i: (b, i, 0)),
                  pl.BlockSpec((1, 6, D), (lambda b, i: (b, 0, 0)) if per_batch else (lambda b, i: (0, 0, 0))),
                  _const_spec(w.shape)],
        out_specs=pl.BlockSpec((1, tm, D), lambda b, i: (b, i, 0)),
        out_shape=jax.ShapeDtypeStruct((B, L, D), F32),
        compiler_params=pltpu.CompilerParams(
            dimension_semantics=("arbitrary", "arbitrary"), vmem_limit_bytes=VMEM_LIMIT_BYTES),
        name="out_proj_residual",
    )(a, x, mod, w)


def mla_layer(x_lat, x_ctx, mod_lat, mod_ctx, norm_g, w_down, qnorm, w_uq, kvnorm, w_ukv, wo):
    ql, kl, vl = mla_qkv_proj(x_lat, norm_g, mod_lat, w_down, qnorm, w_uq, kvnorm, w_ukv, True)
    _, kc, vc = mla_qkv_proj(x_ctx, norm_g, mod_ctx, w_down, qnorm, w_uq, kvnorm, w_ukv, False)
    o = mla_attention(ql, kc, vc, kl, vl)
    return out_proj_residual(o, x_lat, mod_lat, wo.astype(BF16))


def rmsnorm(x, g):
    xf = x.astype(F32)
    y = xf * lax.rsqrt(jnp.mean(xf * xf, axis=-1, keepdims=True) + NORM_EPS)
    return (y * g.astype(F32)).astype(x.dtype)


def modulate(h, shift, scale):
    return h * (1 + scale) + shift


def short_conv(u, w, b):
    L = u.shape[1]
    pad = HY_SHORT // 2
    up = jnp.pad(u, ((0, 0), (pad, HY_SHORT - 1 - pad), (0, 0)))
    out = up[:, 0:L] * w[0]
    for j in range(1, HY_SHORT):
        out = out + up[:, j:j + L] * w[j]
    return out + b


def hyena_filter_spectra(L, f_w1, f_b1, f_w2, f_b2, f_w3, f_b3, f_w4, freq):
    t = jnp.linspace(0.0, 1.0, L, dtype=F32)[:, None]
    bands = (HY_EMB - 1) // 2
    w = 2.0 * math.pi * jnp.arange(L, dtype=F32)[:, None] / L
    f = jnp.linspace(1e-4, bands - 1, bands, dtype=F32)[None, :]
    z = jnp.concatenate([t, jnp.cos(f * w), -jnp.sin(f * w)], axis=-1)
    fr = freq.astype(F32)
    h = jnp.sin(fr * (z @ f_w1.astype(F32) + f_b1.astype(F32)))
    h = jnp.sin(fr * (h @ f_w2.astype(F32) + f_b2.astype(F32)))
    h = jnp.sin(fr * (h @ f_w3.astype(F32) + f_b3.astype(F32)))
    h = (h @ f_w4.astype(F32)).reshape(L, HY_ORDER, 2, D_MODEL)
    max_decay = math.log(HY_TARGET) / HY_FAST_DECAY
    min_decay = math.log(HY_TARGET) / HY_SLOW_DECAY
    deltas = jnp.linspace(min_decay, max_decay, D_MODEL, dtype=F32)
    h = h * jnp.exp(-t * jnp.abs(deltas))[:, None, None, :]
    fwd, bwd = h[:, :, 0], h[:, :, 1]
    two_sided = jnp.concatenate(
        [fwd, jnp.zeros((1, HY_ORDER, D_MODEL), F32), jnp.flip(bwd[1:], axis=0)], axis=0)
    two_sided = two_sided / jnp.sum(jnp.abs(two_sided), axis=0, keepdims=True)
    return jnp.fft.rfft(two_sided, axis=0)


def fft_long_conv(u, spec, skip):
    L = u.shape[1]
    y = jnp.fft.irfft(jnp.fft.rfft(u, n=2 * L, axis=1) * spec[None], n=2 * L, axis=1)[:, :L]
    return y + u * skip


def hyena_mixer(u, w_in, b_in, conv_w, conv_b, f_w1, f_b1, f_w2, f_b2, f_w3, f_b3, f_w4,
                freq, bias, w_out, b_out):
    L = u.shape[1]
    z = short_conv(u @ w_in + b_in, conv_w, conv_b).astype(F32)
    parts = jnp.split(z, HY_ORDER + 1, axis=-1)
    y, gates = parts[0], parts[1:]
    spec = hyena_filter_spectra(L, f_w1, f_b1, f_w2, f_b2, f_w3, f_b3, f_w4, freq)
    skip = bias.astype(F32)
    for n in range(HY_ORDER):
        y = gates[n] * fft_long_conv(y, spec[:, n], skip[n])
    return y.astype(u.dtype) @ w_out + b_out


def gla_inputs(u, w_in, gk_w2, gk_b):
    B, L, _ = u.shape
    z = u @ w_in
    q, k, v, og, r = jnp.split(
        z, [GLA_DK, 2 * GLA_DK, 2 * GLA_DK + GLA_DV, 2 * GLA_DK + 2 * GLA_DV], axis=-1)
    r = r.reshape(B, L, 2, GLA_GATE_RANK)
    gk = jnp.einsum('blzr,zrd->blzd', r, gk_w2) + gk_b
    g = (jax.nn.log_sigmoid(gk.astype(F32)) / GLA_GATE_NORM).reshape(B, L, 2, GLA_HEADS, GLA_HK)
    q = q.astype(F32).reshape(B, L, GLA_HEADS, GLA_HK) * (GLA_HK ** -0.5)
    k = k.astype(F32).reshape(B, L, GLA_HEADS, GLA_HK)
    v = v.astype(F32).reshape(B, L, GLA_HEADS, GLA_HV)
    return q, k, v, g[:, :, 0], g[:, :, 1], og


def gla_chunk_scan(q, k, v, g, s0):
    B, L, H, dk = q.shape
    dv = v.shape[-1]
    n_chunks = L // GLA_CHUNK

    def to_chunks(a):
        return a.reshape(B, n_chunks, GLA_CHUNK, H, a.shape[-1]).transpose(1, 0, 3, 2, 4)

    q, k, v, g = to_chunks(q), to_chunks(k), to_chunks(v), to_chunks(g)
    b = jnp.cumsum(g, axis=3)
    b_last = b[..., -1:, :]
    q_t = q * jnp.exp(b)
    k_t = k * jnp.exp(-b)
    k_end = k * jnp.exp(b_last - b)
    mask = jnp.tril(jnp.ones((GLA_CHUNK, GLA_CHUNK), dtype=bool))
    att = jnp.where(mask, jnp.einsum('nbhcd,nbhsd->nbhcs', q_t, k_t), 0.0)
    o_intra = jnp.einsum('nbhcs,nbhsv->nbhcv', att, v)
    chunk_decay = jnp.exp(b_last[..., 0, :])

    def step(S, inp):
        qn, kn, vn, dn = inp
        o = jnp.einsum('bhcd,bhdv->bhcv', qn, S)
        S = S * dn[..., None] + jnp.einsum('bhcd,bhcv->bhdv', kn, vn)
        return S, o

    s_fin, o_inter = lax.scan(step, s0, (q_t, k_end, v, chunk_decay))
    o = (o_intra + o_inter).transpose(1, 0, 3, 2, 4).reshape(B, L, H, dv)
    return o, s_fin


def gla_direction(q, k, v, g, s0, reverse):
    if reverse:
        q, k, v, g = (jnp.flip(a, axis=1) for a in (q, k, v, g))
    o, s_fin = gla_chunk_scan(q, k, v, g, s0)
    if reverse:
        o = jnp.flip(o, axis=1)
    return o, s_fin


def gla_out(o, og, onorm, wo):
    B, L = o.shape[:2]
    o = rmsnorm(o, onorm).reshape(B, L, GLA_DV).astype(og.dtype)
    return (o * jax.nn.silu(og)) @ wo


def gla_mixer(h_lat, h_ctx, w_in, gk_w2, gk_b, onorm, wo, ctx_out):
    ql, kl, vl, gfl, gbl, ogl = gla_inputs(h_lat, w_in, gk_w2, gk_b)
    qc, kc, vc, gfc, gbc, ogc = gla_inputs(h_ctx, w_in, gk_w2, gk_b)
    s0 = jnp.zeros((h_lat.shape[0], GLA_HEADS, GLA_HK, GLA_HV), F32)
    oc_f, s_f = gla_direction(qc, kc, vc, gfc, s0, False)
    oc_b, s_b = gla_direction(qc, kc, vc, gbc, s0, True)
    ol_f, _ = gla_direction(ql, kl, vl, gfl, s_f, False)
    ol_b, _ = gla_direction(ql, kl, vl, gbl, s_b, True)
    y_lat = gla_out(ol_f + ol_b, ogl, onorm, wo)
    y_ctx = gla_out(oc_f + oc_b, ogc, onorm, wo) if ctx_out else None
    return y_lat, y_ctx


def rotate_half_axis(xh, ang):
    cos = jnp.cos(ang)[None, :, None, :]
    sin = jnp.sin(ang)[None, :, None, :]
    x1, x2 = jnp.split(xh, 2, axis=-1)
    return jnp.concatenate([x1 * cos - x2 * sin, x1 * sin + x2 * cos], axis=-1)


def axial_rope(x, ang_row, ang_col):
    xf = x.astype(F32)
    half = x.shape[-1] // 2
    out = jnp.concatenate([rotate_half_axis(xf[..., :half], ang_row),
                           rotate_half_axis(xf[..., half:], ang_col)], axis=-1)
    return out.astype(x.dtype)


def mla_qkv(u, angles, w_down, qnorm, w_uq, kvnorm, w_ukv):
    B, L, _ = u.shape
    cq, ckv, k_rope = jnp.split(u @ w_down, [MLA_Q_RANK, MLA_Q_RANK + MLA_KV_RANK], axis=-1)
    q = (rmsnorm(cq, qnorm) @ w_uq).reshape(B, L, MLA_HEADS, MLA_NOPE + MLA_ROPE)
    kv = (rmsnorm(ckv, kvnorm) @ w_ukv).reshape(B, L, MLA_HEADS, MLA_NOPE + MLA_V)
    q_nope, q_rope = q[..., :MLA_NOPE], q[..., MLA_NOPE:]
    k_nope, v = kv[..., :MLA_NOPE], kv[..., MLA_NOPE:]
    k_rope = k_rope[:, :, None, :]
    if angles is not None:
        q_rope = axial_rope(q_rope, angles[0], angles[1])
        k_rope = axial_rope(k_rope, angles[0], angles[1])
    q = jnp.concatenate([q_nope, q_rope], axis=-1)
    k = jnp.concatenate([k_nope, jnp.broadcast_to(k_rope, (B, L, MLA_HEADS, MLA_ROPE))], axis=-1)
    return q, k, v


def _pad_heads(a):
    B, L, H, d = a.shape
    a = jnp.pad(a, ((0, 0), (0, 0), (0, 0), (0, MLA_QK_PAD - d)))
    return a.reshape(B, L, H * MLA_QK_PAD).astype(BF16)


def mla_mixer(h_lat, h_ctx, w_down, qnorm, w_uq, kvnorm, w_ukv, wo):
    B, L, _ = h_lat.shape
    rows = L // GRID_W
    r_idx, c_idx = jnp.meshgrid(jnp.arange(rows), jnp.arange(GRID_W), indexing='ij')
    half = MLA_ROPE // 2
    inv_freq = ROPE_THETA ** (-jnp.arange(0, half, 2, dtype=F32) / half)
    ang_row = r_idx.reshape(-1).astype(F32)[:, None] * inv_freq[None, :]
    ang_col = c_idx.reshape(-1).astype(F32)[:, None] * inv_freq[None, :]
    ql, kl, vl = mla_qkv(h_lat, (ang_row, ang_col), w_down, qnorm, w_uq, kvnorm, w_ukv)
    qc, kc, vc = mla_qkv(h_ctx, None, w_down, qnorm, w_uq, kvnorm, w_ukv)
    o = mla_attention(_pad_heads(ql), _pad_heads(kc), vc.reshape(B, -1, MLA_HEADS * MLA_V).astype(BF16),
                      _pad_heads(kl), vl.reshape(B, L, MLA_HEADS * MLA_V).astype(BF16))
    return o.astype(F32) @ wo


def kernel(x, c, ctx, c_ctx, ada_w, ada_b, norm1_g, norm2_g, mlp_w1, mlp_w2, final_g, hy_w_in, hy_b_in, hy_conv_w, hy_conv_b, hy_f_w1, hy_f_b1, hy_f_w2, hy_f_b2, hy_f_w3, hy_f_b3, hy_f_w4, hy_freq, hy_bias, hy_w_out, hy_b_out, gla_w_in, gla_gk_w2, gla_gk_b, gla_onorm, gla_wo, mla_w_down, mla_qnorm, mla_w_uq, mla_kvnorm, mla_w_ukv, mla_wo):
    x_lat = x
    x_ctx = ctx
    silu_c = jax.nn.silu(c)
    silu_cc = jax.nn.silu(c_ctx)
    dft = _dft_constants()
    for i in range(DEPTH):
        kind = i % N_MIXERS
        j = i // N_MIXERS
        ctx_live = any(l % N_MIXERS != 0 for l in range(i + 1, DEPTH))
        mod_lat = (silu_c @ ada_w[i] + ada_b[i]).reshape(-1, 6, D_MODEL)
        mod_ctx = (silu_cc @ ada_w[i] + ada_b[i]).reshape(1, 6, D_MODEL)
        if kind == 0:
            hp = (hy_w_in[j], hy_b_in[j], hy_conv_w[j], hy_conv_b[j], hy_f_w1[j], hy_f_b1[j],
                  hy_f_w2[j], hy_f_b2[j], hy_f_w3[j], hy_f_b3[j], hy_f_w4[j], hy_freq[j],
                  hy_bias[j], hy_w_out[j], hy_b_out[j])
            x_lat = hyena_layer(x_lat, mod_lat, norm1_g[i], *hp, dft)
            if ctx_live:
                x_ctx = hyena_layer(x_ctx, mod_ctx, norm1_g[i], *hp, dft)
        elif kind == 1:
            x_lat, x_ctx = gla_layer(x_lat, x_ctx, mod_lat, mod_ctx, norm1_g[i], gla_w_in[j], gla_gk_w2[j],
                                     gla_gk_b[j], gla_onorm[j], gla_wo[j], ctx_live)
        else:
            assert not ctx_live
            x_lat = mla_layer(x_lat, x_ctx, mod_lat, mod_ctx, norm1_g[i], mla_w_down[j], mla_qnorm[j],
                              mla_w_uq[j], mla_kvnorm[j], mla_w_ukv[j], mla_wo[j])
        w1 = mlp_w1[i].astype(BF16)
        w2 = mlp_w2[i].astype(BF16)
        x_lat = mlp_block(x_lat, norm2_g[i], mod_lat, w1, w2,
                          final_g=final_g if i == DEPTH - 1 else None)
        if ctx_live:
            x_ctx = mlp_block(x_ctx, norm2_g[i], mod_ctx, w1, w2)
    return x_lat
```

```python
import functools
import math

import jax
import jax.numpy as jnp
import numpy as np
from jax import lax
from jax.experimental import pallas as pl
from jax.experimental.pallas import tpu as pltpu

F32 = jnp.float32
BF16 = jnp.bfloat16

D_MODEL = 1024
DEPTH = 4
GRID_W = 64
N_MIXERS = 3
NORM_EPS = 1e-6

HY_ORDER = 2
HY_EMB = 33
HY_SHORT = 3
HY_FAST_DECAY = 0.3
HY_SLOW_DECAY = 1.5
HY_TARGET = 1e-2

GLA_HEADS = 4
GLA_DK = D_MODEL // 2
GLA_DV = D_MODEL
GLA_HK = GLA_DK // GLA_HEADS
GLA_HV = GLA_DV // GLA_HEADS
GLA_GATE_RANK = 16
GLA_GATE_NORM = 16.0
GLA_CHUNK = 64

MLA_HEADS = 8
MLA_Q_RANK = 384
MLA_KV_RANK = 256
MLA_NOPE = 128
MLA_ROPE = 64
MLA_V = 128
ROPE_THETA = 10000.0
Q_BLOCK = 128

VMEM_LIMIT_BYTES = 56 * 1024 * 1024
MLA_QK_PAD = 256
SUBLANES = 8


def _const_spec(shape):
    nd = len(shape)
    return pl.BlockSpec(shape, lambda *_: (0,) * nd, pipeline_mode=pl.Buffered(1))


def _norm_mod(x, g, shift, scale):
    y = x * lax.rsqrt(jnp.mean(x * x, axis=-1, keepdims=True) + NORM_EPS)
    return (y * g) * (1.0 + scale) + shift


def _mlp_kernel(x_ref, g_ref, mod_ref, w1_ref, w2_ref, fg_ref, o_ref, *, hidden_chunk, final_norm):
    x = x_ref[0]
    h = _norm_mod(x, g_ref[...], mod_ref[0, 3:4, :], mod_ref[0, 4:5, :]).astype(BF16)
    hidden = w1_ref.shape[1]
    acc = jnp.zeros(x.shape, F32)
    for c0 in range(0, hidden, hidden_chunk):
        a = jnp.dot(h, w1_ref[:, c0:c0 + hidden_chunk], preferred_element_type=F32)
        a = jnp.square(jnp.maximum(a, 0.0)).astype(BF16)
        acc = acc + jnp.dot(a, w2_ref[c0:c0 + hidden_chunk, :], preferred_element_type=F32)
    out = x + mod_ref[0, 5:6, :] * acc
    if final_norm:
        out = (out * lax.rsqrt(jnp.mean(out * out, axis=-1, keepdims=True) + NORM_EPS)) * fg_ref[...]
    o_ref[0] = out


def mlp_block(x, norm_g, mod, w1, w2, final_g=None):
    B, L, D = x.shape
    tm = min(512, L)
    per_batch = mod.shape[0] != 1
    final_norm = final_g is not None
    fg = (final_g if final_norm else norm_g).reshape(1, D)
    kern = functools.partial(_mlp_kernel, hidden_chunk=1024, final_norm=final_norm)
    return pl.pallas_call(
        kern,
        grid=(B, L // tm),
        in_specs=[
            pl.BlockSpec((1, tm, D), lambda b, i: (b, i, 0)),
            _const_spec((1, D)),
            pl.BlockSpec((1, 6, D), (lambda b, i: (b, 0, 0)) if per_batch else (lambda b, i: (0, 0, 0))),
            _const_spec(w1.shape),
            _const_spec(w2.shape),
            _const_spec((1, D)),
        ],
        out_specs=pl.BlockSpec((1, tm, D), lambda b, i: (b, i, 0)),
        out_shape=jax.ShapeDtypeStruct((B, L, D), F32),
        compiler_params=pltpu.CompilerParams(
            dimension_semantics=("arbitrary", "arbitrary"), vmem_limit_bytes=VMEM_LIMIT_BYTES),
        name="mlp_block",
    )(x, norm_g.reshape(1, D), mod, w1, w2, fg)


def _attn_kernel(q_ref, kc_ref, vc_ref, k_ref, v_ref, o_ref, m_ref, l_ref, acc_ref, sa_ref, sb_ref, *, sub, c):
    q = q_ref[0]
    m_ref[...] = jnp.full(m_ref.shape, -jnp.inf, F32)
    l_ref[...] = jnp.zeros(l_ref.shape, F32)
    acc_ref[...] = jnp.zeros(acc_ref.shape, F32)
    lanes = m_ref.shape[1]
    n_sub = k_ref.shape[1] // sub

    def scores(k):
        return lax.dot_general(q, k, (((1,), (1,)), ((), ())), preferred_element_type=F32)

    def lat(ref, n):
        return ref[0, pl.ds(pl.multiple_of(n * sub, sub), sub), :]

    def accumulate(s, v):
        m_prev = m_ref[...]
        m_new = jnp.maximum(m_prev, jnp.max(s, axis=-1, keepdims=True))
        alpha = jnp.exp2((m_prev - m_new) * c)
        ps = [jnp.exp2((s[:, t:t + lanes] - m_new) * c) for t in range(0, s.shape[1], lanes)]
        l_ref[...] = alpha * l_ref[...] + functools.reduce(lambda a, b: a + b, ps)
        p = jnp.concatenate(ps, axis=1).astype(BF16)
        acc_ref[...] = alpha * acc_ref[...] + jnp.dot(p, v, preferred_element_type=F32)
        m_ref[...] = m_new

    sa_ref[...] = scores(lat(k_ref, 0))
    accumulate(scores(kc_ref[0]), vc_ref[0])

    def pair(n):
        sb_ref[...] = scores(lat(k_ref, n + 1))
        accumulate(sa_ref[...], lat(v_ref, n))

    def body(j, carry):
        n = 2 * j
        pair(n)
        sa_ref[...] = scores(lat(k_ref, n + 2))
        accumulate(sb_ref[...], lat(v_ref, n + 1))
        return carry

    lax.fori_loop(0, n_sub // 2 - 1, body, 0)
    pair(n_sub - 2)
    accumulate(sb_ref[...], lat(v_ref, n_sub - 1))
    l = jnp.sum(l_ref[...], axis=-1, keepdims=True)
    o_ref[0] = (acc_ref[...] / l).astype(o_ref.dtype)


def mla_attention(q, kc, vc, k, v):
    B, L, _ = q.shape
    C = kc.shape[1]
    H = MLA_HEADS
    tq = min(2048, L)
    sub = min(512, L // 4)
    assert L % (2 * sub) == 0 and L % tq == 0
    kern = functools.partial(_attn_kernel, sub=sub,
                             c=(MLA_NOPE + MLA_ROPE) ** -0.5 * math.log2(math.e))
    return pl.pallas_call(
        kern,
        grid=(B, H, L // tq),
        in_specs=[
            pl.BlockSpec((1, tq, MLA_QK_PAD), lambda b, h, i: (b, i, h)),
            pl.BlockSpec((1, C, MLA_QK_PAD), lambda b, h, i: (b, 0, h)),
            pl.BlockSpec((1, C, MLA_V), lambda b, h, i: (b, 0, h)),
            pl.BlockSpec((1, L, MLA_QK_PAD), lambda b, h, i: (b, 0, h)),
            pl.BlockSpec((1, L, MLA_V), lambda b, h, i: (b, 0, h)),
        ],
        out_specs=pl.BlockSpec((1, tq, MLA_V), lambda b, h, i: (b, i, h)),
        out_shape=jax.ShapeDtypeStruct((B, L, H * MLA_V), BF16),
        scratch_shapes=[pltpu.VMEM((tq, MLA_V), F32), pltpu.VMEM((tq, MLA_V), F32), pltpu.VMEM((tq, MLA_V), F32),
                        pltpu.VMEM((tq, sub), F32), pltpu.VMEM((tq, sub), F32)],
        compiler_params=pltpu.CompilerParams(
            dimension_semantics=("arbitrary", "arbitrary", "arbitrary"), vmem_limit_bytes=VMEM_LIMIT_BYTES),
        name="mla_attention",
    )(q, kc, vc, k, v)


DFT_N1 = 128
HY_CH_BLOCK = 32
HY_TAP_ROWS = 64


def _dft_constants():
    n1 = DFT_N1
    n = n1 * n1
    idx = np.arange(n1, dtype=np.float64)
    th = 2.0 * np.pi * np.outer(idx, idx) / n1
    cos1, sin1 = np.cos(th), np.sin(th)
    tw = 2.0 * np.pi * np.outer(idx, idx) / n
    fr, fi = cos1, -sin1
    c = dict(
        e1=np.concatenate([cos1, -sin1], axis=0),
        tr=np.cos(tw), ti=-np.sin(tw),
        g=np.block([[fr, fi], [-fi, fr]]),
        gbar=np.block([[fr, -fi], [fi, fr]]),
        e2=np.concatenate([cos1, -sin1], axis=1) / n,
    )
    return {k: jnp.asarray(v, F32) for k, v in c.items()}


def _hy_in_kernel(x_ref, g_ref, mod_ref, wt_ref, b_ref, o_ref, *, row_chunk):
    h = _norm_mod(x_ref[0], g_ref[...], mod_ref[0, 0:1, :], mod_ref[0, 1:2, :]).astype(BF16)
    for r0 in range(0, wt_ref.shape[0], row_chunk):
        z = lax.dot_general(wt_ref[r0:r0 + row_chunk, :], h, (((1,), (1,)), ((), ())),
                            preferred_element_type=F32)
        o_ref[0, r0:r0 + row_chunk, :] = z + b_ref[r0:r0 + row_chunk, :]


def hyena_in_proj(x, norm_g, mod, w_in_t, b_in):
    B, L, D = x.shape
    n_out = w_in_t.shape[0]
    tm = min(512, L)
    per_batch = mod.shape[0] != 1
    return pl.pallas_call(
        functools.partial(_hy_in_kernel, row_chunk=512),
        grid=(B, L // tm),
        in_specs=[
            pl.BlockSpec((1, tm, D), lambda b, i: (b, i, 0)),
            _const_spec((1, D)),
            pl.BlockSpec((1, 6, D), (lambda b, i: (b, 0, 0)) if per_batch else (lambda b, i: (0, 0, 0))),
            _const_spec(w_in_t.shape),
            _const_spec(b_in.shape),
        ],
        out_specs=pl.BlockSpec((1, n_out, tm), lambda b, i: (b, 0, i)),
        out_shape=jax.ShapeDtypeStruct((B, n_out, L), F32),
        compiler_params=pltpu.CompilerParams(
            dimension_semantics=("arbitrary", "arbitrary"), vmem_limit_bytes=VMEM_LIMIT_BYTES),
        name="hyena_in_proj",
    )(x, norm_g.reshape(1, D), mod, w_in_t, b_in)


def _hy_hidden_kernel(zf_ref, w1_ref, b1_ref, w2_ref, b2_ref, w3_ref, b3_ref, fr_ref, o_ref):
    fr = fr_ref[...]
    h = zf_ref[...].astype(BF16)
    for w_ref, b_ref in ((w1_ref, b1_ref), (w2_ref, b2_ref), (w3_ref, b3_ref)):
        h = jnp.sin(fr * (jnp.dot(w_ref[...], h, preferred_element_type=F32) + b_ref[...]))
        out = h
        h = h.astype(BF16)
    o_ref[...] = out


def _hy_taps_kernel(hid_ref, t_ref, w4_ref, dl_ref, o_ref, *, half):
    hid = hid_ref[...].astype(BF16)
    tf = jnp.dot(w4_ref[0, 0].astype(BF16), hid[:, :half], preferred_element_type=F32)
    tb = jnp.dot(w4_ref[0, 1].astype(BF16), hid[:, half:], preferred_element_type=F32)
    taps = jnp.concatenate([tf, tb], axis=1) * jnp.exp(-t_ref[...] * dl_ref[...])
    pos = lax.broadcasted_iota(jnp.int32, taps.shape, 1)
    taps = jnp.where(pos == half, 0.0, taps)
    o_ref[0] = taps / jnp.sum(jnp.abs(taps), axis=1, keepdims=True)


def hyena_filter_taps(L, f_w1, f_b1, f_w2, f_b2, f_w3, f_b3, f_w4, freq):
    width = f_w1.shape[1]
    n = 2 * L
    pos = np.arange(n)
    pos = np.where(pos <= L, np.minimum(pos, L - 1), n - pos).astype(np.float64)
    bands = (HY_EMB - 1) // 2
    t = jnp.asarray(pos / (L - 1), F32)[None, :]
    w = 2.0 * math.pi * jnp.asarray(pos, F32)[None, :] / L
    f = jnp.linspace(1e-4, bands - 1, bands, dtype=F32)[:, None]
    zf = jnp.concatenate([t, jnp.cos(f * w), -jnp.sin(f * w)], axis=0)
    zf = jnp.pad(zf, ((0, width - HY_EMB), (0, 0)))
    w1t = jnp.pad(f_w1.T, ((0, 0), (0, width - HY_EMB))).astype(BF16)
    col = lambda v: v.reshape(width, 1).astype(F32)
    lane_blk = min(2048, n)
    hidden = pl.pallas_call(
        _hy_hidden_kernel,
        grid=(n // lane_blk,),
        in_specs=[pl.BlockSpec((width, lane_blk), lambda i: (0, i))] + [_const_spec((width, width)), _const_spec((width, 1))] * 3
        + [_const_spec((width, 1))],
        out_specs=pl.BlockSpec((width, lane_blk), lambda i: (0, i)),
        out_shape=jax.ShapeDtypeStruct((width, n), F32),
        name="hyena_filter_hidden",
    )(zf, w1t, col(f_b1), f_w2.T.astype(BF16), col(f_b2), f_w3.T.astype(BF16), col(f_b3), col(freq))
    max_decay = math.log(HY_TARGET) / HY_FAST_DECAY
    min_decay = math.log(HY_TARGET) / HY_SLOW_DECAY
    deltas = jnp.abs(jnp.linspace(min_decay, max_decay, D_MODEL, dtype=F32)).reshape(D_MODEL, 1)
    w4t = f_w4.T.reshape(HY_ORDER, 2, D_MODEL, width)
    rows = HY_TAP_ROWS
    return pl.pallas_call(
        functools.partial(_hy_taps_kernel, half=L),
        grid=(HY_ORDER, D_MODEL // rows),
        in_specs=[
            _const_spec((width, n)),
            _const_spec((1, n)),
            pl.BlockSpec((1, 2, rows, width), lambda o, i: (o, 0, i, 0)),
            pl.BlockSpec((rows, 1), lambda o, i: (i, 0)),
        ],
        out_specs=pl.BlockSpec((1, rows, n), lambda o, i: (o, i, 0)),
        out_shape=jax.ShapeDtypeStruct((HY_ORDER, D_MODEL, n), F32),
        compiler_params=pltpu.CompilerParams(
            dimension_semantics=("arbitrary", "arbitrary"), vmem_limit_bytes=VMEM_LIMIT_BYTES),
        name="hyena_filter_taps",
    )(hidden, t, w4t, deltas)


def _dft_stage1(x, c, e1, tr, ti, zb_ref):
    n1 = DFT_N1
    z = jnp.dot(e1, x, preferred_element_type=F32)
    zr, zi = z[:n1], z[n1:]
    r0 = pl.multiple_of(c * n1, n1)
    zb_ref[pl.ds(r0, n1), 0:n1] = (zr * tr - zi * ti).astype(BF16)
    zb_ref[pl.ds(r0, n1), n1:2 * n1] = (zr * ti + zi * tr).astype(BF16)


def _hy_spectrum_kernel(x_ref, e1_ref, tr_ref, ti_ref, g_ref, o_ref, zb_ref):
    n_ch = x_ref.shape[0]
    e1, tr, ti = e1_ref[...], tr_ref[...], ti_ref[...]

    def stage1(c, carry):
        _dft_stage1(x_ref[c].astype(BF16), c, e1, tr, ti, zb_ref)
        return carry

    lax.fori_loop(0, n_ch, stage1, 0, unroll=8)
    xh = jnp.dot(zb_ref[...], g_ref[...], preferred_element_type=F32)
    o_ref[...] = xh.reshape(o_ref.shape).astype(o_ref.dtype)


def hyena_filter_spectrum(taps, consts):
    R = taps.shape[0]
    n1 = DFT_N1
    C = HY_CH_BLOCK
    x = taps.reshape(R, n1, n1)
    return pl.pallas_call(
        _hy_spectrum_kernel,
        grid=(R // C,),
        in_specs=[pl.BlockSpec((C, n1, n1), lambda i: (i, 0, 0)), _const_spec((2 * n1, n1)),
                  _const_spec((n1, n1)), _const_spec((n1, n1)), _const_spec((2 * n1, 2 * n1))],
        out_specs=pl.BlockSpec((C, n1, 2 * n1), lambda i: (i, 0, 0)),
        out_shape=jax.ShapeDtypeStruct((R, n1, 2 * n1), BF16),
        scratch_shapes=[pltpu.VMEM((C * n1, 2 * n1), BF16)],
        compiler_params=pltpu.CompilerParams(
            dimension_semantics=("arbitrary",), vmem_limit_bytes=VMEM_LIMIT_BYTES),
        name="hyena_filter_spectrum",
    )(x, consts["e1"].astype(BF16), consts["tr"], consts["ti"], consts["g"].astype(BF16))


def _hy_conv_kernel(cw_ref, cb_ref, sk_ref, zy_ref, zg1_ref, zg2_ref, hh_ref, e1_ref, tr_ref, ti_ref,
                    g_ref, gb_ref, e2_ref, o_ref, y_sc, g1_sc, g2_sc, zb_sc, u_sc, pad_sc, *, n_ch, d_model):
    n1 = DFT_N1
    rows = zy_ref.shape[2]
    ch0 = pl.program_id(0) * n_ch
    b_idx = lax.broadcasted_iota(jnp.int32, (rows, n1), 1)
    e1, tr, ti, e2 = e1_ref[...], tr_ref[...], ti_ref[...], e2_ref[...]

    pad = pad_sc.shape[1] - rows
    top = pad // 2
    zero_rows = jnp.zeros((top, n1), F32)

    def short_conv(z_ref, c, slot, col):
        z = z_ref[0, c]
        pad_sc[slot, 0:top, :] = zero_rows
        pad_sc[slot, top + rows:pad + rows, :] = zero_rows
        pad_sc[slot, top:top + rows, :] = z
        up = pad_sc[slot, top - 1:top - 1 + rows, :]
        down = pad_sc[slot, top + 1:top + 1 + rows, :]
        prev = pltpu.roll(jnp.where(b_idx == n1 - 1, up, z), 1, axis=1)
        nxt = pltpu.roll(jnp.where(b_idx == 0, down, z), n1 - 1, axis=1)
        n_col = 3 * d_model
        return cw_ref[col] * prev + cw_ref[n_col + col] * z + cw_ref[2 * n_col + col] * nxt + cb_ref[col]

    gate_scs = (g1_sc, g2_sc)

    def spectral_product(order):
        xh = jnp.dot(zb_sc[...], g_ref[...], preferred_element_type=F32)
        hh = hh_ref[order].reshape(n_ch * n1, 2 * n1).astype(F32)
        xr, xi, hr, hi = xh[:, :n1], xh[:, n1:], hh[:, :n1], hh[:, n1:]
        yh = jnp.concatenate([xr * hr - xi * hi, xr * hi + xi * hr], axis=1).astype(BF16)
        u_sc[...] = jnp.dot(yh, gb_ref[...], preferred_element_type=F32)

    def finish(c, order):
        r0 = pl.multiple_of(c * n1, n1)
        ur, ui = u_sc[pl.ds(r0, n1), 0:n1], u_sc[pl.ds(r0, n1), n1:2 * n1]
        stacked = jnp.concatenate([ur * tr + ui * ti, ui * tr - ur * ti], axis=0).astype(BF16)
        conv = jnp.dot(e2, stacked, preferred_element_type=F32)
        return gate_scs[order][c] * (conv + y_sc[c] * sk_ref[order * d_model + ch0 + c])

    def head(c, carry):
        y = short_conv(zy_ref, c, 0, ch0 + c)
        y_sc[c] = y
        g1_sc[c] = short_conv(zg1_ref, c, 1, d_model + ch0 + c)
        g2_sc[c] = short_conv(zg2_ref, c, 2, 2 * d_model + ch0 + c)
        _dft_stage1(y.astype(BF16), c, e1, tr, ti, zb_sc)
        return carry

    lax.fori_loop(0, n_ch, head, 0, unroll=4)
    for order in range(HY_ORDER - 1):
        spectral_product(order)

        def middle(c, carry, order=order):
            y_sc[c] = finish(c, order)
            return carry

        def restart(c, carry):
            _dft_stage1(y_sc[c].astype(BF16), c, e1, tr, ti, zb_sc)
            return carry

        lax.fori_loop(0, n_ch, middle, 0, unroll=8)
        lax.fori_loop(0, n_ch, restart, 0, unroll=8)
    spectral_product(HY_ORDER - 1)

    def tail(c, carry):
        o_ref[0, c] = finish(c, HY_ORDER - 1).astype(o_ref.dtype)
        return carry

    lax.fori_loop(0, n_ch, tail, 0, unroll=8)


def hyena_long_conv(z_t, spectrum, conv_w, conv_b, skip, consts):
    B, n_col, L = z_t.shape
    D = n_col // (HY_ORDER + 1)
    n1 = DFT_N1
    rows = L // n1
    assert 2 * rows == n1 and HY_ORDER == 2
    C = HY_CH_BLOCK
    nblk = D // C
    z4 = z_t.reshape(B, n_col, rows, n1)
    smem = pl.BlockSpec(memory_space=pltpu.SMEM)
    zspec = lambda off: pl.BlockSpec((1, C, rows, n1), lambda i, b: (b, i + off * nblk, 0, 0))
    kern = functools.partial(_hy_conv_kernel, n_ch=C, d_model=D)
    out = pl.pallas_call(
        kern,
        grid=(nblk, B),
        in_specs=[smem, smem, smem, zspec(0), zspec(1), zspec(2),
                  pl.BlockSpec((HY_ORDER, C, n1, 2 * n1), lambda i, b: (0, i, 0, 0)),
                  _const_spec((2 * n1, rows)), _const_spec((n1, n1)), _const_spec((n1, n1)),
                  _const_spec((2 * n1, 2 * n1)), _const_spec((2 * n1, 2 * n1)), _const_spec((rows, 2 * n1))],
        out_specs=pl.BlockSpec((1, C, rows, n1), lambda i, b: (b, i, 0, 0)),
        out_shape=jax.ShapeDtypeStruct((B, D, rows, n1), BF16),
        scratch_shapes=[pltpu.VMEM((C, rows, n1), F32), pltpu.VMEM((C, rows, n1), F32), pltpu.VMEM((C, rows, n1), F32),
                        pltpu.VMEM((C * n1, 2 * n1), BF16), pltpu.VMEM((C * n1, 2 * n1), F32),
                        pltpu.VMEM((HY_ORDER + 1, rows + 2 * SUBLANES, n1), F32)],
        compiler_params=pltpu.CompilerParams(
            dimension_semantics=("arbitrary", "arbitrary"), vmem_limit_bytes=VMEM_LIMIT_BYTES),
        name="hyena_long_conv",
    )(conv_w.reshape(-1), conv_b.reshape(-1), skip.reshape(-1), z4, z4, z4, spectrum,
      consts["e1"][:, :rows].astype(BF16), consts["tr"], consts["ti"], consts["g"].astype(BF16),
      consts["gbar"].astype(BF16), consts["e2"][:rows].astype(BF16))
    return out.reshape(B, D, L)


def _hy_short_seq_kernel(zy_ref, zg1_ref, zg2_ref, taps_ref, par_ref, dfull_ref, dinv_ref, o_ref):
    rows, L = zy_ref.shape[1], zy_ref.shape[2]
    lane = lax.broadcasted_iota(jnp.int32, (rows, L), 1)
    par = par_ref[...]
    nfreq = dinv_ref.shape[0] // 2

    def short_conv(z, grp):
        prev = jnp.where(lane == 0, 0.0, pltpu.roll(z, 1, axis=1))
        nxt = jnp.where(lane == L - 1, 0.0, pltpu.roll(z, L - 1, axis=1))
        c = 4 * grp
        return par[:, c:c + 1] * prev + par[:, c + 1:c + 2] * z + par[:, c + 2:c + 3] * nxt + par[:, c + 3:c + 4]

    y = short_conv(zy_ref[0], 0)
    gates = (short_conv(zg1_ref[0], 1), short_conv(zg2_ref[0], 2))
    d_first = dfull_ref[0:L, :]
    for order in range(HY_ORDER):
        hh = jnp.dot(taps_ref[order].astype(BF16), dfull_ref[...], preferred_element_type=F32)
        xh = jnp.dot(y.astype(BF16), d_first, preferred_element_type=F32)
        xr, xi, hr, hi = xh[:, :nfreq], xh[:, nfreq:], hh[:, :nfreq], hh[:, nfreq:]
        yh = jnp.concatenate([xr * hr - xi * hi, xr * hi + xi * hr], axis=1).astype(BF16)
        conv = jnp.dot(yh, dinv_ref[...], preferred_element_type=F32)
        y = gates[order] * (conv + y * par[:, 12 + order:13 + order])
    o_ref[0] = y.astype(o_ref.dtype)


def hyena_short_seq_conv(z_t, taps, conv_w, conv_b, skip):
    B, n_col, L = z_t.shape
    D = n_col // (HY_ORDER + 1)
    n = 2 * L
    idx = np.arange(n, dtype=np.float64)
    ang = 2.0 * np.pi * np.outer(idx, idx) / n
    dfull = jnp.asarray(np.concatenate([np.cos(ang), -np.sin(ang)], axis=1), F32)
    dinv = jnp.asarray(np.concatenate([np.cos(ang), -np.sin(ang)], axis=0)[:, :L] / n, F32)
    cw = conv_w.reshape(HY_SHORT, HY_ORDER + 1, D)
    cb = conv_b.reshape(1, HY_ORDER + 1, D)
    par = jnp.concatenate([cw, cb], axis=0)
    par = jnp.transpose(par, (2, 1, 0)).reshape(D, 4 * (HY_ORDER + 1))
    par = jnp.concatenate([par, skip.T, jnp.zeros((D, 2), F32)], axis=1)
    rows = 256
    nblk = D // rows
    zspec = lambda off: pl.BlockSpec((1, rows, L), lambda i, b: (b, i + off * nblk, 0))
    return pl.pallas_call(
        _hy_short_seq_kernel,
        grid=(nblk, B),
        in_specs=[zspec(0), zspec(1), zspec(2),
                  pl.BlockSpec((HY_ORDER, rows, n), lambda i, b: (0, i, 0)),
                  pl.BlockSpec((rows, par.shape[1]), lambda i, b: (i, 0)),
                  _const_spec(dfull.shape), _const_spec(dinv.shape)],
        out_specs=pl.BlockSpec((1, rows, L), lambda i, b: (b, i, 0)),
        out_shape=jax.ShapeDtypeStruct((B, D, L), BF16),
        compiler_params=pltpu.CompilerParams(
            dimension_semantics=("arbitrary", "arbitrary"), vmem_limit_bytes=VMEM_LIMIT_BYTES),
        name="hyena_short_seq_conv",
    )(z_t, z_t, z_t, taps, par, dfull.astype(BF16), dinv.astype(BF16))


def hyena_layer(x, mod, norm_g, w_in, b_in, conv_w, conv_b, f_w1, f_b1, f_w2, f_b2, f_w3, f_b3, f_w4,
                freq, bias, w_out, b_out, dft):
    L = x.shape[1]
    taps = hyena_filter_taps(L, f_w1, f_b1, f_w2, f_b2, f_w3, f_b3, f_w4, freq)
    z_t = hyena_in_proj(x, norm_g, mod, w_in.T.astype(BF16), b_in.reshape(-1, 1))
    if 2 * L == DFT_N1 * DFT_N1:
        spec = hyena_filter_spectrum(taps.reshape(HY_ORDER * D_MODEL, 2 * L), dft)
        spec = spec.reshape(HY_ORDER, D_MODEL, DFT_N1, 2 * DFT_N1)
        y_t = hyena_long_conv(z_t, spec, conv_w, conv_b, bias, dft)
    else:
        y_t = hyena_short_seq_conv(z_t, taps, conv_w, conv_b, bias)
    return hyena_out_proj(y_t, x, mod, w_out.astype(BF16), b_out)


def _hy_out_kernel(y_ref, x_ref, mod_ref, w_ref, b_ref, o_ref):
    acc = lax.dot_general(y_ref[0], w_ref[...], (((0,), (0,)), ((), ())), preferred_element_type=F32)
    o_ref[0] = x_ref[0] + mod_ref[0, 2:3, :] * (acc + b_ref[...])


def hyena_out_proj(y_t, x, mod, w_out, b_out):
    B, L, D = x.shape
    tm = min(512, L)
    per_batch = mod.shape[0] != 1
    return pl.pallas_call(
        _hy_out_kernel,
        grid=(B, L // tm),
        in_specs=[
            pl.BlockSpec((1, D, tm), lambda b, i: (b, 0, i)),
            pl.BlockSpec((1, tm, D), lambda b, i: (b, i, 0)),
            pl.BlockSpec((1, 6, D), (lambda b, i: (b, 0, 0)) if per_batch else (lambda b, i: (0, 0, 0))),
            _const_spec(w_out.shape),
            _const_spec((1, D)),
        ],
        out_specs=pl.BlockSpec((1, tm, D), lambda b, i: (b, i, 0)),
        out_shape=jax.ShapeDtypeStruct((B, L, D), F32),
        compiler_params=pltpu.CompilerParams(
            dimension_semantics=("arbitrary", "arbitrary"), vmem_limit_bytes=VMEM_LIMIT_BYTES),
        name="hyena_out_proj",
    )(y_t, x, mod, w_out, b_out.reshape(1, D))


def _gla_in_kernel(x_ref, g_ref, mod_ref, w_ref, wr_ref, w2_ref, gb_ref, qk_ref, v_ref, og_ref, gate_ref):
    h = _norm_mod(x_ref[0], g_ref[...], mod_ref[0, 0:1, :], mod_ref[0, 1:2, :]).astype(BF16)
    n = qk_ref.shape[2]
    qk = jnp.dot(h, w_ref[:, 0:n], preferred_element_type=F32)
    half = n // 2
    qk_ref[0, :, 0:half] = qk[:, 0:half] * (GLA_HK ** -0.5)
    qk_ref[0, :, half:n] = qk[:, half:n]
    v_ref[0] = jnp.dot(h, w_ref[:, n:2 * n], preferred_element_type=F32).astype(v_ref.dtype)
    og_ref[0] = jnp.dot(h, w_ref[:, 2 * n:3 * n], preferred_element_type=F32)
    r = jnp.dot(h, wr_ref[...], preferred_element_type=F32).astype(BF16)
    gk = jnp.dot(r, w2_ref[...], preferred_element_type=F32) + gb_ref[...]
    gate_ref[0] = -(jnp.maximum(-gk, 0.0) + jnp.log1p(jnp.exp(-jnp.abs(gk)))) * (1.0 / GLA_GATE_NORM)


def gla_in_proj(x, norm_g, mod, w_in, gk_w2, gk_b):
    B, L, D = x.shape
    tm = min(512, L)
    per_batch = mod.shape[0] != 1
    n_main = 2 * GLA_DK + 2 * GLA_DV
    w_main = w_in[:, :n_main].astype(BF16)
    lanes = 128
    w_r = jnp.pad(w_in[:, n_main:], ((0, 0), (0, lanes - 2 * GLA_GATE_RANK))).astype(BF16)
    w2 = jnp.zeros((lanes, 2 * GLA_DK), F32)
    w2 = w2.at[:GLA_GATE_RANK, :GLA_DK].set(gk_w2[0]).at[GLA_GATE_RANK:2 * GLA_GATE_RANK, GLA_DK:].set(gk_w2[1])
    tok = lambda n: pl.BlockSpec((1, tm, n), lambda b, i: (b, i, 0))
    n = 2 * GLA_DK
    assert GLA_DV == n
    return pl.pallas_call(
        _gla_in_kernel,
        grid=(B, L // tm),
        in_specs=[tok(D), _const_spec((1, D)),
                  pl.BlockSpec((1, 6, D), (lambda b, i: (b, 0, 0)) if per_batch else (lambda b, i: (0, 0, 0))),
                  _const_spec(w_main.shape), _const_spec(w_r.shape), _const_spec(w2.shape), _const_spec((1, n))],
        out_specs=[tok(n), tok(n), tok(n), tok(n)],
        out_shape=[jax.ShapeDtypeStruct((B, L, n), F32), jax.ShapeDtypeStruct((B, L, n), BF16),
                   jax.ShapeDtypeStruct((B, L, n), F32), jax.ShapeDtypeStruct((B, L, n), F32)],
        compiler_params=pltpu.CompilerParams(
            dimension_semantics=("arbitrary", "arbitrary"), vmem_limit_bytes=VMEM_LIMIT_BYTES),
        name="gla_in_proj",
    )(x, norm_g.reshape(1, D), mod, w_main, w_r, w2.astype(BF16), gk_b.reshape(1, n))


def _gla_scan_kernel(*refs, reverse, add_prev):
    if add_prev:
        qk_ref, v_ref, g_ref, s0_ref, prev_ref, o_ref, sfin_ref, st_ref = refs
    else:
        qk_ref, v_ref, g_ref, s0_ref, o_ref, sfin_ref, st_ref = refs
        prev_ref = None
    i = pl.program_id(1)
    C, H, dk, dv = GLA_CHUNK, GLA_HEADS, GLA_HK, GLA_HV

    @pl.when(i == 0)
    def _():
        st_ref[...] = s0_ref[0]

    r_idx = lax.broadcasted_iota(jnp.int32, (C, C), 0)
    c_idx = lax.broadcasted_iota(jnp.int32, (C, C), 1)
    keep = (r_idx <= c_idx) if reverse else (r_idx >= c_idx)
    tri = keep.astype(F32)
    n_chunks = qk_ref.shape[1] // C
    order = range(n_chunks - 1, -1, -1) if reverse else range(n_chunks)
    for ci in order:
        rows = slice(ci * C, (ci + 1) * C)
        b = jnp.dot(tri, g_ref[0, rows, :], precision=lax.Precision.HIGHEST, preferred_element_type=F32)
        b_last = b[0:1] if reverse else b[C - 1:C]
        e_pos, e_neg, e_end, dec = jnp.exp(b), jnp.exp(-b), jnp.exp(b_last - b), jnp.exp(b_last)
        for h in range(H):
            kc = slice(h * dk, (h + 1) * dk)
            vc = slice(h * dv, (h + 1) * dv)
            q = qk_ref[0, rows, kc]
            k = qk_ref[0, rows, H * dk + h * dk:H * dk + (h + 1) * dk]
            v = v_ref[0, rows, vc]
            q_t = (q * e_pos[:, kc]).astype(BF16)
            k_t = (k * e_neg[:, kc]).astype(BF16)
            k_end = (k * e_end[:, kc]).astype(BF16)
            att = lax.dot_general(q_t, k_t, (((1,), (1,)), ((), ())), preferred_element_type=F32)
            att = jnp.where(keep, att, 0.0).astype(BF16)
            st = st_ref[h]
            o = jnp.dot(att, v, preferred_element_type=F32) + lax.dot_general(
                q_t, st.astype(BF16), (((1,), (1,)), ((), ())), preferred_element_type=F32)
            st_ref[h] = st * dec[:, kc] + lax.dot_general(
                v, k_end, (((0,), (0,)), ((), ())), preferred_element_type=F32)
            if prev_ref is not None:
                o = o + prev_ref[0, rows, vc]
            o_ref[0, rows, vc] = o

    @pl.when(i == pl.num_programs(1) - 1)
    def _():
        sfin_ref[0] = st_ref[...]


def gla_scan(qk, v, gates, s0, direction, prev=None):
    B, L, _ = qk.shape
    reverse = direction == 1
    T = min(512, L)
    nT = L // T
    H, dk, dv = GLA_HEADS, GLA_HK, GLA_HV
    blk = (lambda i: nT - 1 - i) if reverse else (lambda i: i)
    tok = lambda n, col=0: pl.BlockSpec((1, T, n), lambda b, i: (b, blk(i), col))
    st_spec = pl.BlockSpec((1, H, dv, dk), lambda b, i: (b, 0, 0, 0))
    in_specs = [tok(2 * GLA_DK), tok(GLA_DV), tok(GLA_DK, direction), st_spec]
    args = [qk, v, gates, s0]
    if prev is not None:
        in_specs.append(tok(GLA_DV))
        args.append(prev)
    kern = functools.partial(_gla_scan_kernel, reverse=reverse, add_prev=prev is not None)
    return pl.pallas_call(
        kern,
        grid=(B, nT),
        in_specs=in_specs,
        out_specs=[tok(GLA_DV), st_spec],
        out_shape=[jax.ShapeDtypeStruct((B, L, GLA_DV), F32), jax.ShapeDtypeStruct((B, H, dv, dk), F32)],
        scratch_shapes=[pltpu.VMEM((H, dv, dk), F32)],
        compiler_params=pltpu.CompilerParams(
            dimension_semantics=("arbitrary", "arbitrary"), vmem_limit_bytes=VMEM_LIMIT_BYTES),
        name="gla_scan_bwd" if reverse else "gla_scan_fwd",
    )(*args)


def _gla_out_kernel(o_ref, og_ref, x_ref, mod_ref, on_ref, w_ref, out_ref):
    dv = GLA_HV
    og = og_ref[0]
    parts = []
    for h in range(GLA_HEADS):
        o = o_ref[0, :, h * dv:(h + 1) * dv]
        y = o * lax.rsqrt(jnp.mean(o * o, axis=-1, keepdims=True) + NORM_EPS) * on_ref[...]
        gate = og[:, h * dv:(h + 1) * dv]
        parts.append((y * (gate * jax.nn.sigmoid(gate))).astype(BF16))
    a = jnp.concatenate(parts, axis=1)
    out_ref[0] = x_ref[0] + mod_ref[0, 2:3, :] * jnp.dot(a, w_ref[...], preferred_element_type=F32)


def gla_out_proj(o, og, x, mod, onorm, wo):
    B, L, D = x.shape
    tm = min(512, L)
    per_batch = mod.shape[0] != 1
    tok = pl.BlockSpec((1, tm, D), lambda b, i: (b, i, 0))
    return pl.pallas_call(
        _gla_out_kernel,
        grid=(B, L // tm),
        in_specs=[tok, tok, tok,
                  pl.BlockSpec((1, 6, D), (lambda b, i: (b, 0, 0)) if per_batch else (lambda b, i: (0, 0, 0))),
                  _const_spec((1, GLA_HV)), _const_spec(wo.shape)],
        out_specs=tok,
        out_shape=jax.ShapeDtypeStruct((B, L, D), F32),
        compiler_params=pltpu.CompilerParams(
            dimension_semantics=("arbitrary", "arbitrary"), vmem_limit_bytes=VMEM_LIMIT_BYTES),
        name="gla_out_proj",
    )(o, og, x, mod, onorm.reshape(1, GLA_HV), wo)


def gla_layer(x_lat, x_ctx, mod_lat, mod_ctx, norm_g, w_in, gk_w2, gk_b, onorm, wo, ctx_out):
    B = x_lat.shape[0]
    qk_l, v_l, og_l, g_l = gla_in_proj(x_lat, norm_g, mod_lat, w_in, gk_w2, gk_b)
    qk_c, v_c, og_c, g_c = gla_in_proj(x_ctx, norm_g, mod_ctx, w_in, gk_w2, gk_b)
    s0 = jnp.zeros((B, GLA_HEADS, GLA_HV, GLA_HK), F32)
    oc, s_f = gla_scan(qk_c, v_c, g_c, s0, 0)
    oc, s_b = gla_scan(qk_c, v_c, g_c, s0, 1, prev=oc)
    ol, _ = gla_scan(qk_l, v_l, g_l, s_f, 0)
    ol, _ = gla_scan(qk_l, v_l, g_l, s_b, 1, prev=ol)
    wo = wo.astype(BF16)
    x_lat = gla_out_proj(ol, og_l, x_lat, mod_lat, onorm, wo)
    if ctx_out:
        x_ctx = gla_out_proj(oc, og_c, x_ctx, mod_ctx, onorm, wo)
    return x_lat, x_ctx


def _rope_swap_perm():
    half = MLA_ROPE // 2
    quarter = half // 2
    p = []
    for base in (0, half):
        p += list(range(base + quarter, base + half)) + list(range(base, base + quarter))
    return np.asarray(p)


def mla_rope_tables(L, rotate):
    half = MLA_ROPE // 2
    zeros = jnp.zeros((L, MLA_ROPE), F32)
    if not rotate:
        return jnp.concatenate([jnp.ones((L, MLA_ROPE), F32), zeros], axis=1), jnp.zeros((L, 2 * MLA_ROPE), F32)
    pos = jnp.arange(L)
    inv_freq = ROPE_THETA ** (-jnp.arange(0, half, 2, dtype=F32) / half)
    ang_row = (pos // GRID_W).astype(F32)[:, None] * inv_freq[None, :]
    ang_col = (pos % GRID_W).astype(F32)[:, None] * inv_freq[None, :]
    cr, sr, cc, sc = jnp.cos(ang_row), jnp.sin(ang_row), jnp.cos(ang_col), jnp.sin(ang_col)
    cos = jnp.concatenate([cr, cr, cc, cc, zeros], axis=1)
    sin = jnp.concatenate([-sr, sr, -sc, sc, zeros], axis=1)
    return cos, sin


def _mla_qkv_kernel(x_ref, g_ref, mod_ref, wd_ref, qn_ref, wq_ref, kn_ref, wkv_ref, cos_ref, sin_ref,
                    q_ref, k_ref, v_ref):
    h = _norm_mod(x_ref[0], g_ref[...], mod_ref[0, 0:1, :], mod_ref[0, 1:2, :]).astype(BF16)
    c = jnp.dot(h, wd_ref[...], preferred_element_type=F32)
    cos, sin = cos_ref[...], sin_ref[...]
    lanes = cos.shape[1]

    def rms(a, g):
        return (a * lax.rsqrt(jnp.mean(a * a, axis=-1, keepdims=True) + NORM_EPS) * g).astype(BF16)

    def rope(tile):
        return tile * cos + pltpu.roll(tile, lanes // 2, axis=1) * sin

    cq = rms(c[:, :MLA_Q_RANK], qn_ref[...])
    ckv = rms(c[:, MLA_Q_RANK:MLA_Q_RANK + MLA_KV_RANK], kn_ref[...])
    k_rope = rope(c[:, MLA_Q_RANK + MLA_KV_RANK:]).astype(k_ref.dtype)
    q = jnp.dot(cq, wq_ref[...], preferred_element_type=F32)
    kv = jnp.dot(ckv, wkv_ref[...], preferred_element_type=F32)
    for hd in range(MLA_HEADS):
        o = hd * MLA_QK_PAD
        q_ref[0, :, o:o + MLA_NOPE] = q[:, o:o + MLA_NOPE].astype(q_ref.dtype)
        q_ref[0, :, o + MLA_NOPE:o + MLA_QK_PAD] = rope(q[:, o + MLA_NOPE:o + MLA_QK_PAD]).astype(q_ref.dtype)
        k_ref[0, :, o:o + MLA_NOPE] = kv[:, o:o + MLA_NOPE].astype(k_ref.dtype)
        k_ref[0, :, o + MLA_NOPE:o + MLA_QK_PAD] = k_rope
        v_ref[0, :, hd * MLA_V:(hd + 1) * MLA_V] = kv[:, o + MLA_NOPE:o + MLA_QK_PAD].astype(v_ref.dtype)


def mla_qkv_proj(x, norm_g, mod, w_down, qnorm, w_uq, kvnorm, w_ukv, rotate):
    B, L, D = x.shape
    tm = min(512, L)
    per_batch = mod.shape[0] != 1
    perm = _rope_swap_perm()
    rope0 = MLA_Q_RANK + MLA_KV_RANK
    wd = jnp.concatenate([w_down, w_down[:, rope0:][:, perm]], axis=1).astype(BF16)
    wq = w_uq.reshape(MLA_Q_RANK, MLA_HEADS, MLA_NOPE + MLA_ROPE)
    wq = jnp.concatenate([wq, wq[:, :, MLA_NOPE:][:, :, perm]], axis=2)
    wq = wq.reshape(MLA_Q_RANK, MLA_HEADS * MLA_QK_PAD).astype(BF16)
    cos, sin = mla_rope_tables(L, rotate)
    tok = lambda n: pl.BlockSpec((1, tm, n), lambda b, i: (b, i, 0))
    nq = MLA_HEADS * MLA_QK_PAD
    nv = MLA_HEADS * MLA_V
    tab = pl.BlockSpec((tm, 2 * MLA_ROPE), lambda b, i: (i, 0))
    return pl.pallas_call(
        _mla_qkv_kernel,
        grid=(B, L // tm),
        in_specs=[tok(D), _const_spec((1, D)),
                  pl.BlockSpec((1, 6, D), (lambda b, i: (b, 0, 0)) if per_batch else (lambda b, i: (0, 0, 0))),
                  _const_spec(wd.shape), _const_spec((1, MLA_Q_RANK)), _const_spec(wq.shape),
                  _const_spec((1, MLA_KV_RANK)), _const_spec(w_ukv.shape), tab, tab],
        out_specs=[tok(nq), tok(nq), tok(nv)],
        out_shape=[jax.ShapeDtypeStruct((B, L, nq), BF16), jax.ShapeDtypeStruct((B, L, nq), BF16),
                   jax.ShapeDtypeStruct((B, L, nv), BF16)],
        compiler_params=pltpu.CompilerParams(
            dimension_semantics=("arbitrary", "arbitrary"), vmem_limit_bytes=VMEM_LIMIT_BYTES),
        name="mla_qkv_proj",
    )(x, norm_g.reshape(1, D), mod, wd, qnorm.reshape(1, -1), wq, kvnorm.reshape(1, -1),
      w_ukv.astype(BF16), cos, sin)


def _out_proj_kernel(a_ref, x_ref, mod_ref, w_ref, o_ref):
    o_ref[0] = x_ref[0] + mod_ref[0, 2:3, :] * jnp.dot(a_ref[0], w_ref[...], preferred_element_type=F32)


def out_proj_residual(a, x, mod, w):
    B, L, D = x.shape
    Kd = a.shape[2]
    tm = min(512, L)
    per_batch = mod.shape[0] != 1
    return pl.pallas_call(
        _out_proj_kernel,
        grid=(B, L // tm),
        in_specs=[pl.BlockSpec((1, tm, Kd), lambda b, i: (b, i, 0)),
                  pl.BlockSpec((1, tm, D), lambda b, i: (b, i, 0)),
                  pl.BlockSpec((1, 6, D), (lambda b, i: (b, 0, 0)) if per_batch else (lambda b, i: (0, 0, 0))),
                  _const_spec(w.shape)],
        out_specs=pl.BlockSpec((1, tm, D), lambda b, i: (b, i, 0)),
        out_shape=jax.ShapeDtypeStruct((B, L, D), F32),
        compiler_params=pltpu.CompilerParams(
            dimension_semantics=("arbitrary", "arbitrary"), vmem_limit_bytes=VMEM_LIMIT_BYTES),
        name="out_proj_residual",
    )(a, x, mod, w)


def mla_layer(x_lat, x_ctx, mod_lat, mod_ctx, norm_g, w_down, qnorm, w_uq, kvnorm, w_ukv, wo):
    ql, kl, vl = mla_qkv_proj(x_lat, norm_g, mod_lat, w_down, qnorm, w_uq, kvnorm, w_ukv, True)
    _, kc, vc = mla_qkv_proj(x_ctx, norm_g, mod_ctx, w_down, qnorm, w_uq, kvnorm, w_ukv, False)
    o = mla_attention(ql, kc, vc, kl, vl)
    return out_proj_residual(o, x_lat, mod_lat, wo.astype(BF16))


def kernel(x, c, ctx, c_ctx, ada_w, ada_b, norm1_g, norm2_g, mlp_w1, mlp_w2, final_g, hy_w_in, hy_b_in, hy_conv_w, hy_conv_b, hy_f_w1, hy_f_b1, hy_f_w2, hy_f_b2, hy_f_w3, hy_f_b3, hy_f_w4, hy_freq, hy_bias, hy_w_out, hy_b_out, gla_w_in, gla_gk_w2, gla_gk_b, gla_onorm, gla_wo, mla_w_down, mla_qnorm, mla_w_uq, mla_kvnorm, mla_w_ukv, mla_wo):
    x_lat = x
    x_ctx = ctx
    silu_c = jax.nn.silu(c)
    silu_cc = jax.nn.silu(c_ctx)
    dft = _dft_constants()
    for i in range(DEPTH):
        kind = i % N_MIXERS
        j = i // N_MIXERS
        ctx_live = any(l % N_MIXERS != 0 for l in range(i + 1, DEPTH))
        mod_lat = (silu_c @ ada_w[i] + ada_b[i]).reshape(-1, 6, D_MODEL)
        mod_ctx = (silu_cc @ ada_w[i] + ada_b[i]).reshape(1, 6, D_MODEL)
        if kind == 0:
            hp = (hy_w_in[j], hy_b_in[j], hy_conv_w[j], hy_conv_b[j], hy_f_w1[j], hy_f_b1[j],
                  hy_f_w2[j], hy_f_b2[j], hy_f_w3[j], hy_f_b3[j], hy_f_w4[j], hy_freq[j],
                  hy_bias[j], hy_w_out[j], hy_b_out[j])
            x_lat = hyena_layer(x_lat, mod_lat, norm1_g[i], *hp, dft)
            if ctx_live:
                x_ctx = hyena_layer(x_ctx, mod_ctx, norm1_g[i], *hp, dft)
        elif kind == 1:
            x_lat, x_ctx = gla_layer(x_lat, x_ctx, mod_lat, mod_ctx, norm1_g[i], gla_w_in[j], gla_gk_w2[j],
                                     gla_gk_b[j], gla_onorm[j], gla_wo[j], ctx_live)
        else:
            assert not ctx_live
            x_lat = mla_layer(x_lat, x_ctx, mod_lat, mod_ctx, norm1_g[i], mla_w_down[j], mla_qnorm[j],
                              mla_w_uq[j], mla_kvnorm[j], mla_w_ukv[j], mla_wo[j])
        w1 = mlp_w1[i].astype(BF16)
        w2 = mlp_w2[i].astype(BF16)
        x_lat = mlp_block(x_lat, norm2_g[i], mod_lat, w1, w2,
                          final_g=final_g if i == DEPTH - 1 else None)
        if ctx_live:
            x_ctx = mlp_block(x_ctx, norm2_g[i], mod_ctx, w1, w2)
    return x_lat
```

```python
import functools
import math

import jax
import jax.numpy as jnp
import numpy as np
from jax import lax
from jax.experimental import pallas as pl
from jax.experimental.pallas import tpu as pltpu

F32 = jnp.float32
BF16 = jnp.bfloat16

D_MODEL = 1024
DEPTH = 4
GRID_W = 64
N_MIXERS = 3
NORM_EPS = 1e-6

HY_ORDER = 2
HY_EMB = 33
HY_SHORT = 3
HY_FAST_DECAY = 0.3
HY_SLOW_DECAY = 1.5
HY_TARGET = 1e-2

GLA_HEADS = 4
GLA_DK = D_MODEL // 2
GLA_DV = D_MODEL
GLA_HK = GLA_DK // GLA_HEADS
GLA_HV = GLA_DV // GLA_HEADS
GLA_GATE_RANK = 16
GLA_GATE_NORM = 16.0
GLA_CHUNK = 64

MLA_HEADS = 8
MLA_Q_RANK = 384
MLA_KV_RANK = 256
MLA_NOPE = 128
MLA_ROPE = 64
MLA_V = 128
ROPE_THETA = 10000.0
Q_BLOCK = 128

VMEM_LIMIT_BYTES = 56 * 1024 * 1024
MLA_QK_PAD = 256
SUBLANES = 8
MLA_Q_PRESCALE = (MLA_NOPE + MLA_ROPE) ** -0.5 * math.log2(math.e)


def _const_spec(shape):
    nd = len(shape)
    return pl.BlockSpec(shape, lambda *_: (0,) * nd, pipeline_mode=pl.Buffered(1))


def _norm_mod(x, g, shift, scale):
    y = x * lax.rsqrt(jnp.mean(x * x, axis=-1, keepdims=True) + NORM_EPS)
    return (y * g) * (1.0 + scale) + shift


def _mlp_kernel(x_ref, g_ref, mod_ref, w1_ref, w2_ref, fg_ref, o_ref, *, hidden_chunk, final_norm):
    x = x_ref[0]
    h = _norm_mod(x, g_ref[...], mod_ref[0, 3:4, :], mod_ref[0, 4:5, :]).astype(BF16)
    hidden = w1_ref.shape[1]
    acc = jnp.zeros(x.shape, F32)
    for c0 in range(0, hidden, hidden_chunk):
        a = jnp.dot(h, w1_ref[:, c0:c0 + hidden_chunk], preferred_element_type=F32)
        a = jnp.square(jnp.maximum(a, 0.0)).astype(BF16)
        acc = acc + jnp.dot(a, w2_ref[c0:c0 + hidden_chunk, :], preferred_element_type=F32)
    out = x + mod_ref[0, 5:6, :] * acc
    if final_norm:
        out = (out * lax.rsqrt(jnp.mean(out * out, axis=-1, keepdims=True) + NORM_EPS)) * fg_ref[...]
    o_ref[0] = out


def mlp_block(x, norm_g, mod, w1, w2, final_g=None):
    B, L, D = x.shape
    tm = min(512, L)
    per_batch = mod.shape[0] != 1
    final_norm = final_g is not None
    fg = (final_g if final_norm else norm_g).reshape(1, D)
    kern = functools.partial(_mlp_kernel, hidden_chunk=1024, final_norm=final_norm)
    return pl.pallas_call(
        kern,
        grid=(B, L // tm),
        in_specs=[
            pl.BlockSpec((1, tm, D), lambda b, i: (b, i, 0)),
            _const_spec((1, D)),
            pl.BlockSpec((1, 6, D), (lambda b, i: (b, 0, 0)) if per_batch else (lambda b, i: (0, 0, 0))),
            _const_spec(w1.shape),
            _const_spec(w2.shape),
            _const_spec((1, D)),
        ],
        out_specs=pl.BlockSpec((1, tm, D), lambda b, i: (b, i, 0)),
        out_shape=jax.ShapeDtypeStruct((B, L, D), F32),
        compiler_params=pltpu.CompilerParams(
            dimension_semantics=("arbitrary", "arbitrary"), vmem_limit_bytes=VMEM_LIMIT_BYTES),
        name="mlp_block",
    )(x, norm_g.reshape(1, D), mod, w1, w2, fg)


def _attn_kernel(q_ref, kc_ref, vc_ref, k_ref, v_ref, o_ref, m_ref, acc_ref, sa_ref, sb_ref, *, sub):
    q = q_ref[0]
    m_ref[...] = jnp.full(m_ref.shape, -jnp.inf, F32)
    acc_ref[...] = jnp.zeros(acc_ref.shape, F32)
    lanes = m_ref.shape[1]
    n_sub = k_ref.shape[1] // sub

    def scores(k):
        return lax.dot_general(q, k, (((1,), (1,)), ((), ())), preferred_element_type=F32)

    def lat(ref, n):
        return ref[0, pl.ds(pl.multiple_of(n * sub, sub), sub), :]

    def accumulate(s, v):
        m_prev = m_ref[...]
        m_new = jnp.maximum(m_prev, jnp.max(s, axis=-1, keepdims=True))
        alpha = jnp.exp2(m_prev - m_new)
        ps = [jnp.exp2(s[:, t:t + lanes] - m_new) for t in range(0, s.shape[1], lanes)]
        p = jnp.concatenate(ps, axis=1).astype(BF16)
        pv = jnp.dot(p, v, preferred_element_type=F32)
        for t in range(0, acc_ref.shape[1], lanes):
            acc_ref[:, t:t + lanes] = alpha * acc_ref[:, t:t + lanes] + pv[:, t:t + lanes]
        m_ref[...] = m_new

    sa_ref[...] = scores(lat(k_ref, 0))
    accumulate(scores(kc_ref[0]), vc_ref[0])

    def pair(n):
        sb_ref[...] = scores(lat(k_ref, n + 1))
        accumulate(sa_ref[...], lat(v_ref, n))

    def body(j, carry):
        n = 2 * j
        pair(n)
        sa_ref[...] = scores(lat(k_ref, n + 2))
        accumulate(sb_ref[...], lat(v_ref, n + 1))
        return carry

    lax.fori_loop(0, n_sub // 2 - 1, body, 0)
    pair(n_sub - 2)
    accumulate(sb_ref[...], lat(v_ref, n_sub - 1))
    o_ref[0] = (acc_ref[:, 0:lanes] / acc_ref[:, lanes:2 * lanes]).astype(o_ref.dtype)


def mla_attention(q, kc, vc, k, v):
    B, L, _ = q.shape
    C = kc.shape[1]
    H = MLA_HEADS
    tq = min(2048, L)
    sub = min(512, L // 4)
    assert L % (2 * sub) == 0 and L % tq == 0
    kern = functools.partial(_attn_kernel, sub=sub)
    return pl.pallas_call(
        kern,
        grid=(B, H, L // tq),
        in_specs=[
            pl.BlockSpec((1, tq, MLA_QK_PAD), lambda b, h, i: (b, i, h)),
            pl.BlockSpec((1, C, MLA_QK_PAD), lambda b, h, i: (b, 0, h)),
            pl.BlockSpec((1, C, 2 * MLA_V), lambda b, h, i: (b, 0, h)),
            pl.BlockSpec((1, L, MLA_QK_PAD), lambda b, h, i: (b, 0, h)),
            pl.BlockSpec((1, L, 2 * MLA_V), lambda b, h, i: (b, 0, h)),
        ],
        out_specs=pl.BlockSpec((1, tq, MLA_V), lambda b, h, i: (b, i, h)),
        out_shape=jax.ShapeDtypeStruct((B, L, H * MLA_V), BF16),
        scratch_shapes=[pltpu.VMEM((tq, MLA_V), F32), pltpu.VMEM((tq, 2 * MLA_V), F32),
                        pltpu.VMEM((tq, sub), F32), pltpu.VMEM((tq, sub), F32)],
        compiler_params=pltpu.CompilerParams(
            dimension_semantics=("arbitrary", "arbitrary", "arbitrary"), vmem_limit_bytes=VMEM_LIMIT_BYTES),
        name="mla_attention",
    )(q, kc, vc, k, v)


DFT_N1 = 128
HY_CH_BLOCK = 32
HY_TAP_ROWS = 64


def _dft_constants():
    n1 = DFT_N1
    n = n1 * n1
    idx = np.arange(n1, dtype=np.float64)
    th = 2.0 * np.pi * np.outer(idx, idx) / n1
    cos1, sin1 = np.cos(th), np.sin(th)
    tw = 2.0 * np.pi * np.outer(idx, idx) / n
    fr, fi = cos1, -sin1
    c = dict(
        e1=np.concatenate([cos1, -sin1], axis=0),
        tr=np.cos(tw), ti=-np.sin(tw),
        g=np.block([[fr, fi], [-fi, fr]]),
        gbar=np.block([[fr, -fi], [fi, fr]]),
        e2=np.concatenate([cos1, -sin1], axis=1) / n,
    )
    return {k: jnp.asarray(v, F32) for k, v in c.items()}


def _hy_in_kernel(x_ref, g_ref, mod_ref, wt_ref, b_ref, o_ref, *, row_chunk):
    h = _norm_mod(x_ref[0], g_ref[...], mod_ref[0, 0:1, :], mod_ref[0, 1:2, :]).astype(BF16)
    for r0 in range(0, wt_ref.shape[0], row_chunk):
        z = lax.dot_general(wt_ref[r0:r0 + row_chunk, :], h, (((1,), (1,)), ((), ())),
                            preferred_element_type=F32)
        o_ref[0, r0:r0 + row_chunk, :] = z + b_ref[r0:r0 + row_chunk, :]


def hyena_in_proj(x, norm_g, mod, w_in_t, b_in):
    B, L, D = x.shape
    n_out = w_in_t.shape[0]
    tm = min(512, L)
    per_batch = mod.shape[0] != 1
    return pl.pallas_call(
        functools.partial(_hy_in_kernel, row_chunk=512),
        grid=(B, L // tm),
        in_specs=[
            pl.BlockSpec((1, tm, D), lambda b, i: (b, i, 0)),
            _const_spec((1, D)),
            pl.BlockSpec((1, 6, D), (lambda b, i: (b, 0, 0)) if per_batch else (lambda b, i: (0, 0, 0))),
            _const_spec(w_in_t.shape),
            _const_spec(b_in.shape),
        ],
        out_specs=pl.BlockSpec((1, n_out, tm), lambda b, i: (b, 0, i)),
        out_shape=jax.ShapeDtypeStruct((B, n_out, L), F32),
        compiler_params=pltpu.CompilerParams(
            dimension_semantics=("arbitrary", "arbitrary"), vmem_limit_bytes=VMEM_LIMIT_BYTES),
        name="hyena_in_proj",
    )(x, norm_g.reshape(1, D), mod, w_in_t, b_in)


def _hy_hidden_kernel(zf_ref, w1_ref, b1_ref, w2_ref, b2_ref, w3_ref, b3_ref, fr_ref, o_ref):
    fr = fr_ref[...]
    h = zf_ref[...].astype(BF16)
    for w_ref, b_ref in ((w1_ref, b1_ref), (w2_ref, b2_ref), (w3_ref, b3_ref)):
        h = jnp.sin(fr * (jnp.dot(w_ref[...], h, preferred_element_type=F32) + b_ref[...]))
        out = h
        h = h.astype(BF16)
    o_ref[...] = out


def _hy_taps_kernel(hid_ref, t_ref, w4_ref, dl_ref, o_ref, *, half):
    hid = hid_ref[...].astype(BF16)
    tf = jnp.dot(w4_ref[0, 0].astype(BF16), hid[:, :half], preferred_element_type=F32)
    tb = jnp.dot(w4_ref[0, 1].astype(BF16), hid[:, half:], preferred_element_type=F32)
    taps = jnp.concatenate([tf, tb], axis=1) * jnp.exp(-t_ref[...] * dl_ref[...])
    pos = lax.broadcasted_iota(jnp.int32, taps.shape, 1)
    taps = jnp.where(pos == half, 0.0, taps)
    o_ref[0] = taps / jnp.sum(jnp.abs(taps), axis=1, keepdims=True)


def hyena_filter_taps(L, f_w1, f_b1, f_w2, f_b2, f_w3, f_b3, f_w4, freq):
    width = f_w1.shape[1]
    n = 2 * L
    pos = np.arange(n)
    pos = np.where(pos <= L, np.minimum(pos, L - 1), n - pos).astype(np.float64)
    bands = (HY_EMB - 1) // 2
    t = jnp.asarray(pos / (L - 1), F32)[None, :]
    w = 2.0 * math.pi * jnp.asarray(pos, F32)[None, :] / L
    f = jnp.linspace(1e-4, bands - 1, bands, dtype=F32)[:, None]
    zf = jnp.concatenate([t, jnp.cos(f * w), -jnp.sin(f * w)], axis=0)
    zf = jnp.pad(zf, ((0, width - HY_EMB), (0, 0)))
    w1t = jnp.pad(f_w1.T, ((0, 0), (0, width - HY_EMB))).astype(BF16)
    col = lambda v: v.reshape(width, 1).astype(F32)
    lane_blk = min(2048, n)
    hidden = pl.pallas_call(
        _hy_hidden_kernel,
        grid=(n // lane_blk,),
        in_specs=[pl.BlockSpec((width, lane_blk), lambda i: (0, i))] + [_const_spec((width, width)), _const_spec((width, 1))] * 3
        + [_const_spec((width, 1))],
        out_specs=pl.BlockSpec((width, lane_blk), lambda i: (0, i)),
        out_shape=jax.ShapeDtypeStruct((width, n), F32),
        name="hyena_filter_hidden",
    )(zf, w1t, col(f_b1), f_w2.T.astype(BF16), col(f_b2), f_w3.T.astype(BF16), col(f_b3), col(freq))
    max_decay = math.log(HY_TARGET) / HY_FAST_DECAY
    min_decay = math.log(HY_TARGET) / HY_SLOW_DECAY
    deltas = jnp.abs(jnp.linspace(min_decay, max_decay, D_MODEL, dtype=F32)).reshape(D_MODEL, 1)
    w4t = f_w4.T.reshape(HY_ORDER, 2, D_MODEL, width)
    rows = HY_TAP_ROWS
    return pl.pallas_call(
        functools.partial(_hy_taps_kernel, half=L),
        grid=(HY_ORDER, D_MODEL // rows),
        in_specs=[
            _const_spec((width, n)),
            _const_spec((1, n)),
            pl.BlockSpec((1, 2, rows, width), lambda o, i: (o, 0, i, 0)),
            pl.BlockSpec((rows, 1), lambda o, i: (i, 0)),
        ],
        out_specs=pl.BlockSpec((1, rows, n), lambda o, i: (o, i, 0)),
        out_shape=jax.ShapeDtypeStruct((HY_ORDER, D_MODEL, n), F32),
        compiler_params=pltpu.CompilerParams(
            dimension_semantics=("arbitrary", "arbitrary"), vmem_limit_bytes=VMEM_LIMIT_BYTES),
        name="hyena_filter_taps",
    )(hidden, t, w4t, deltas)


def _dft_stage1(x, c, e1, tr, ti, zb_ref):
    n1 = DFT_N1
    z = jnp.dot(e1, x, preferred_element_type=F32)
    zr, zi = z[:n1], z[n1:]
    r0 = pl.multiple_of(c * n1, n1)
    zb_ref[pl.ds(r0, n1), 0:n1] = (zr * tr - zi * ti).astype(BF16)
    zb_ref[pl.ds(r0, n1), n1:2 * n1] = (zr * ti + zi * tr).astype(BF16)


def _hy_spectrum_kernel(x_ref, e1_ref, tr_ref, ti_ref, g_ref, o_ref, zb_ref):
    n_ch = x_ref.shape[0]
    e1, tr, ti = e1_ref[...], tr_ref[...], ti_ref[...]

    def stage1(c, carry):
        _dft_stage1(x_ref[c].astype(BF16), c, e1, tr, ti, zb_ref)
        return carry

    lax.fori_loop(0, n_ch, stage1, 0, unroll=8)
    xh = jnp.dot(zb_ref[...], g_ref[...], preferred_element_type=F32)
    o_ref[...] = xh.reshape(o_ref.shape).astype(o_ref.dtype)


def hyena_filter_spectrum(taps, consts):
    R = taps.shape[0]
    n1 = DFT_N1
    C = HY_CH_BLOCK
    x = taps.reshape(R, n1, n1)
    return pl.pallas_call(
        _hy_spectrum_kernel,
        grid=(R // C,),
        in_specs=[pl.BlockSpec((C, n1, n1), lambda i: (i, 0, 0)), _const_spec((2 * n1, n1)),
                  _const_spec((n1, n1)), _const_spec((n1, n1)), _const_spec((2 * n1, 2 * n1))],
        out_specs=pl.BlockSpec((C, n1, 2 * n1), lambda i: (i, 0, 0)),
        out_shape=jax.ShapeDtypeStruct((R, n1, 2 * n1), BF16),
        scratch_shapes=[pltpu.VMEM((C * n1, 2 * n1), BF16)],
        compiler_params=pltpu.CompilerParams(
            dimension_semantics=("arbitrary",), vmem_limit_bytes=VMEM_LIMIT_BYTES),
        name="hyena_filter_spectrum",
    )(x, consts["e1"].astype(BF16), consts["tr"], consts["ti"], consts["g"].astype(BF16))


def _hy_conv_kernel(cw_ref, cb_ref, sk_ref, zy_ref, zg1_ref, zg2_ref, hh_ref, e1_ref, tr_ref, ti_ref,
                    g_ref, gb_ref, e2_ref, o_ref, y_sc, g1_sc, g2_sc, zb_sc, u_sc, pad_sc, *, n_ch, d_model):
    n1 = DFT_N1
    rows = zy_ref.shape[2]
    ch0 = pl.program_id(0) * n_ch
    b_idx = lax.broadcasted_iota(jnp.int32, (rows, n1), 1)
    e1, tr, ti, e2 = e1_ref[...], tr_ref[...], ti_ref[...], e2_ref[...]

    pad = pad_sc.shape[1] - rows
    top = pad // 2
    zero_rows = jnp.zeros((top, n1), F32)

    def short_conv(z_ref, c, slot, col):
        z = z_ref[0, c]
        pad_sc[slot, 0:top, :] = zero_rows
        pad_sc[slot, top + rows:pad + rows, :] = zero_rows
        pad_sc[slot, top:top + rows, :] = z
        up = pad_sc[slot, top - 1:top - 1 + rows, :]
        down = pad_sc[slot, top + 1:top + 1 + rows, :]
        prev = pltpu.roll(jnp.where(b_idx == n1 - 1, up, z), 1, axis=1)
        nxt = pltpu.roll(jnp.where(b_idx == 0, down, z), n1 - 1, axis=1)
        n_col = 3 * d_model
        return cw_ref[col] * prev + cw_ref[n_col + col] * z + cw_ref[2 * n_col + col] * nxt + cb_ref[col]

    gate_scs = (g1_sc, g2_sc)

    def spectral_product(order):
        xh = jnp.dot(zb_sc[...], g_ref[...], preferred_element_type=F32)
        hh = hh_ref[order].reshape(n_ch * n1, 2 * n1).astype(F32)
        xr, xi, hr, hi = xh[:, :n1], xh[:, n1:], hh[:, :n1], hh[:, n1:]
        yh = jnp.concatenate([xr * hr - xi * hi, xr * hi + xi * hr], axis=1).astype(BF16)
        u_sc[...] = jnp.dot(yh, gb_ref[...], preferred_element_type=F32)

    def finish(c, order):
        r0 = pl.multiple_of(c * n1, n1)
        ur, ui = u_sc[pl.ds(r0, n1), 0:n1], u_sc[pl.ds(r0, n1), n1:2 * n1]
        stacked = jnp.concatenate([ur * tr + ui * ti, ui * tr - ur * ti], axis=0).astype(BF16)
        conv = jnp.dot(e2, stacked, preferred_element_type=F32)
        return gate_scs[order][c] * (conv + y_sc[c] * sk_ref[order * d_model + ch0 + c])

    def head(c, carry):
        y = short_conv(zy_ref, c, 0, ch0 + c)
        y_sc[c] = y
        g1_sc[c] = short_conv(zg1_ref, c, 1, d_model + ch0 + c)
        g2_sc[c] = short_conv(zg2_ref, c, 2, 2 * d_model + ch0 + c)
        _dft_stage1(y.astype(BF16), c, e1, tr, ti, zb_sc)
        return carry

    lax.fori_loop(0, n_ch, head, 0, unroll=4)
    for order in range(HY_ORDER - 1):
        spectral_product(order)

        def middle(c, carry, order=order):
            y_sc[c] = finish(c, order)
            return carry

        def restart(c, carry):
            _dft_stage1(y_sc[c].astype(BF16), c, e1, tr, ti, zb_sc)
            return carry

        lax.fori_loop(0, n_ch, middle, 0, unroll=8)
        lax.fori_loop(0, n_ch, restart, 0, unroll=8)
    spectral_product(HY_ORDER - 1)

    def tail(c, carry):
        o_ref[0, c] = finish(c, HY_ORDER - 1).astype(o_ref.dtype)
        return carry

    lax.fori_loop(0, n_ch, tail, 0, unroll=8)


def hyena_long_conv(z_t, spectrum, conv_w, conv_b, skip, consts):
    B, n_col, L = z_t.shape
    D = n_col // (HY_ORDER + 1)
    n1 = DFT_N1
    rows = L // n1
    assert 2 * rows == n1 and HY_ORDER == 2
    C = HY_CH_BLOCK
    nblk = D // C
    z4 = z_t.reshape(B, n_col, rows, n1)
    smem = pl.BlockSpec(memory_space=pltpu.SMEM)
    zspec = lambda off: pl.BlockSpec((1, C, rows, n1), lambda i, b: (b, i + off * nblk, 0, 0))
    kern = functools.partial(_hy_conv_kernel, n_ch=C, d_model=D)
    out = pl.pallas_call(
        kern,
        grid=(nblk, B),
        in_specs=[smem, smem, smem, zspec(0), zspec(1), zspec(2),
                  pl.BlockSpec((HY_ORDER, C, n1, 2 * n1), lambda i, b: (0, i, 0, 0)),
                  _const_spec((2 * n1, rows)), _const_spec((n1, n1)), _const_spec((n1, n1)),
                  _const_spec((2 * n1, 2 * n1)), _const_spec((2 * n1, 2 * n1)), _const_spec((rows, 2 * n1))],
        out_specs=pl.BlockSpec((1, C, rows, n1), lambda i, b: (b, i, 0, 0)),
        out_shape=jax.ShapeDtypeStruct((B, D, rows, n1), BF16),
        scratch_shapes=[pltpu.VMEM((C, rows, n1), F32), pltpu.VMEM((C, rows, n1), F32), pltpu.VMEM((C, rows, n1), F32),
                        pltpu.VMEM((C * n1, 2 * n1), BF16), pltpu.VMEM((C * n1, 2 * n1), F32),
                        pltpu.VMEM((HY_ORDER + 1, rows + 2 * SUBLANES, n1), F32)],
        compiler_params=pltpu.CompilerParams(
            dimension_semantics=("arbitrary", "arbitrary"), vmem_limit_bytes=VMEM_LIMIT_BYTES),
        name="hyena_long_conv",
    )(conv_w.reshape(-1), conv_b.reshape(-1), skip.reshape(-1), z4, z4, z4, spectrum,
      consts["e1"][:, :rows].astype(BF16), consts["tr"], consts["ti"], consts["g"].astype(BF16),
      consts["gbar"].astype(BF16), consts["e2"][:rows].astype(BF16))
    return out.reshape(B, D, L)


def _hy_short_seq_kernel(zy_ref, zg1_ref, zg2_ref, taps_ref, par_ref, dfull_ref, dinv_ref, o_ref):
    rows, L = zy_ref.shape[1], zy_ref.shape[2]
    lane = lax.broadcasted_iota(jnp.int32, (rows, L), 1)
    par = par_ref[...]
    nfreq = dinv_ref.shape[0] // 2

    def short_conv(z, grp):
        prev = jnp.where(lane == 0, 0.0, pltpu.roll(z, 1, axis=1))
        nxt = jnp.where(lane == L - 1, 0.0, pltpu.roll(z, L - 1, axis=1))
        c = 4 * grp
        return par[:, c:c + 1] * prev + par[:, c + 1:c + 2] * z + par[:, c + 2:c + 3] * nxt + par[:, c + 3:c + 4]

    y = short_conv(zy_ref[0], 0)
    gates = (short_conv(zg1_ref[0], 1), short_conv(zg2_ref[0], 2))
    d_first = dfull_ref[0:L, :]
    for order in range(HY_ORDER):
        hh = jnp.dot(taps_ref[order].astype(BF16), dfull_ref[...], preferred_element_type=F32)
        xh = jnp.dot(y.astype(BF16), d_first, preferred_element_type=F32)
        xr, xi, hr, hi = xh[:, :nfreq], xh[:, nfreq:], hh[:, :nfreq], hh[:, nfreq:]
        yh = jnp.concatenate([xr * hr - xi * hi, xr * hi + xi * hr], axis=1).astype(BF16)
        conv = jnp.dot(yh, dinv_ref[...], preferred_element_type=F32)
        y = gates[order] * (conv + y * par[:, 12 + order:13 + order])
    o_ref[0] = y.astype(o_ref.dtype)


def hyena_short_seq_conv(z_t, taps, conv_w, conv_b, skip):
    B, n_col, L = z_t.shape
    D = n_col // (HY_ORDER + 1)
    n = 2 * L
    idx = np.arange(n, dtype=np.float64)
    ang = 2.0 * np.pi * np.outer(idx, idx) / n
    dfull = jnp.asarray(np.concatenate([np.cos(ang), -np.sin(ang)], axis=1), F32)
    dinv = jnp.asarray(np.concatenate([np.cos(ang), -np.sin(ang)], axis=0)[:, :L] / n, F32)
    cw = conv_w.reshape(HY_SHORT, HY_ORDER + 1, D)
    cb = conv_b.reshape(1, HY_ORDER + 1, D)
    par = jnp.concatenate([cw, cb], axis=0)
    par = jnp.transpose(par, (2, 1, 0)).reshape(D, 4 * (HY_ORDER + 1))
    par = jnp.concatenate([par, skip.T, jnp.zeros((D, 2), F32)], axis=1)
    rows = 256
    nblk = D // rows
    zspec = lambda off: pl.BlockSpec((1, rows, L), lambda i, b: (b, i + off * nblk, 0))
    return pl.pallas_call(
        _hy_short_seq_kernel,
        grid=(nblk, B),
        in_specs=[zspec(0), zspec(1), zspec(2),
                  pl.BlockSpec((HY_ORDER, rows, n), lambda i, b: (0, i, 0)),
                  pl.BlockSpec((rows, par.shape[1]), lambda i, b: (i, 0)),
                  _const_spec(dfull.shape), _const_spec(dinv.shape)],
        out_specs=pl.BlockSpec((1, rows, L), lambda i, b: (b, i, 0)),
        out_shape=jax.ShapeDtypeStruct((B, D, L), BF16),
        compiler_params=pltpu.CompilerParams(
            dimension_semantics=("arbitrary", "arbitrary"), vmem_limit_bytes=VMEM_LIMIT_BYTES),
        name="hyena_short_seq_conv",
    )(z_t, z_t, z_t, taps, par, dfull.astype(BF16), dinv.astype(BF16))


def hyena_layer(x, mod, norm_g, w_in, b_in, conv_w, conv_b, f_w1, f_b1, f_w2, f_b2, f_w3, f_b3, f_w4,
                freq, bias, w_out, b_out, dft):
    L = x.shape[1]
    taps = hyena_filter_taps(L, f_w1, f_b1, f_w2, f_b2, f_w3, f_b3, f_w4, freq)
    z_t = hyena_in_proj(x, norm_g, mod, w_in.T.astype(BF16), b_in.reshape(-1, 1))
    if 2 * L == DFT_N1 * DFT_N1:
        spec = hyena_filter_spectrum(taps.reshape(HY_ORDER * D_MODEL, 2 * L), dft)
        spec = spec.reshape(HY_ORDER, D_MODEL, DFT_N1, 2 * DFT_N1)
        y_t = hyena_long_conv(z_t, spec, conv_w, conv_b, bias, dft)
    else:
        y_t = hyena_short_seq_conv(z_t, taps, conv_w, conv_b, bias)
    return hyena_out_proj(y_t, x, mod, w_out.astype(BF16), b_out)


def _hy_out_kernel(y_ref, x_ref, mod_ref, w_ref, b_ref, o_ref):
    acc = lax.dot_general(y_ref[0], w_ref[...], (((0,), (0,)), ((), ())), preferred_element_type=F32)
    o_ref[0] = x_ref[0] + mod_ref[0, 2:3, :] * (acc + b_ref[...])


def hyena_out_proj(y_t, x, mod, w_out, b_out):
    B, L, D = x.shape
    tm = min(512, L)
    per_batch = mod.shape[0] != 1
    return pl.pallas_call(
        _hy_out_kernel,
        grid=(B, L // tm),
        in_specs=[
            pl.BlockSpec((1, D, tm), lambda b, i: (b, 0, i)),
            pl.BlockSpec((1, tm, D), lambda b, i: (b, i, 0)),
            pl.BlockSpec((1, 6, D), (lambda b, i: (b, 0, 0)) if per_batch else (lambda b, i: (0, 0, 0))),
            _const_spec(w_out.shape),
            _const_spec((1, D)),
        ],
        out_specs=pl.BlockSpec((1, tm, D), lambda b, i: (b, i, 0)),
        out_shape=jax.ShapeDtypeStruct((B, L, D), F32),
        compiler_params=pltpu.CompilerParams(
            dimension_semantics=("arbitrary", "arbitrary"), vmem_limit_bytes=VMEM_LIMIT_BYTES),
        name="hyena_out_proj",
    )(y_t, x, mod, w_out, b_out.reshape(1, D))


def _gla_in_kernel(x_ref, g_ref, mod_ref, w_ref, wr_ref, w2_ref, gb_ref, qk_ref, v_ref, og_ref, gate_ref):
    h = _norm_mod(x_ref[0], g_ref[...], mod_ref[0, 0:1, :], mod_ref[0, 1:2, :]).astype(BF16)
    n = qk_ref.shape[2]
    qk = jnp.dot(h, w_ref[:, 0:n], preferred_element_type=F32)
    half = n // 2
    qk_ref[0, :, 0:half] = qk[:, 0:half] * (GLA_HK ** -0.5)
    qk_ref[0, :, half:n] = qk[:, half:n]
    v_ref[0] = jnp.dot(h, w_ref[:, n:2 * n], preferred_element_type=F32).astype(v_ref.dtype)
    og_ref[0] = jnp.dot(h, w_ref[:, 2 * n:3 * n], preferred_element_type=F32)
    r = jnp.dot(h, wr_ref[...], preferred_element_type=F32).astype(BF16)
    gk = jnp.dot(r, w2_ref[...], preferred_element_type=F32) + gb_ref[...]
    gate_ref[0] = -(jnp.maximum(-gk, 0.0) + jnp.log1p(jnp.exp(-jnp.abs(gk)))) * (1.0 / GLA_GATE_NORM)


def gla_in_proj(x, norm_g, mod, w_in, gk_w2, gk_b):
    B, L, D = x.shape
    tm = min(512, L)
    per_batch = mod.shape[0] != 1
    n_main = 2 * GLA_DK + 2 * GLA_DV
    w_main = w_in[:, :n_main].astype(BF16)
    lanes = 128
    w_r = jnp.pad(w_in[:, n_main:], ((0, 0), (0, lanes - 2 * GLA_GATE_RANK))).astype(BF16)
    w2 = jnp.zeros((lanes, 2 * GLA_DK), F32)
    w2 = w2.at[:GLA_GATE_RANK, :GLA_DK].set(gk_w2[0]).at[GLA_GATE_RANK:2 * GLA_GATE_RANK, GLA_DK:].set(gk_w2[1])
    tok = lambda n: pl.BlockSpec((1, tm, n), lambda b, i: (b, i, 0))
    n = 2 * GLA_DK
    assert GLA_DV == n
    return pl.pallas_call(
        _gla_in_kernel,
        grid=(B, L // tm),
        in_specs=[tok(D), _const_spec((1, D)),
                  pl.BlockSpec((1, 6, D), (lambda b, i: (b, 0, 0)) if per_batch else (lambda b, i: (0, 0, 0))),
                  _const_spec(w_main.shape), _const_spec(w_r.shape), _const_spec(w2.shape), _const_spec((1, n))],
        out_specs=[tok(n), tok(n), tok(n), tok(n)],
        out_shape=[jax.ShapeDtypeStruct((B, L, n), F32), jax.ShapeDtypeStruct((B, L, n), BF16),
                   jax.ShapeDtypeStruct((B, L, n), F32), jax.ShapeDtypeStruct((B, L, n), F32)],
        compiler_params=pltpu.CompilerParams(
            dimension_semantics=("arbitrary", "arbitrary"), vmem_limit_bytes=VMEM_LIMIT_BYTES),
        name="gla_in_proj",
    )(x, norm_g.reshape(1, D), mod, w_main, w_r, w2.astype(BF16), gk_b.reshape(1, n))


def _gla_scan_kernel(*refs, reverse, add_prev):
    if add_prev:
        qk_ref, v_ref, g_ref, s0_ref, prev_ref, o_ref, sfin_ref, st_ref = refs
    else:
        qk_ref, v_ref, g_ref, s0_ref, o_ref, sfin_ref, st_ref = refs
        prev_ref = None
    i = pl.program_id(1)
    C, H, dk, dv = GLA_CHUNK, GLA_HEADS, GLA_HK, GLA_HV

    @pl.when(i == 0)
    def _():
        st_ref[...] = s0_ref[0]

    r_idx = lax.broadcasted_iota(jnp.int32, (C, C), 0)
    c_idx = lax.broadcasted_iota(jnp.int32, (C, C), 1)
    keep = (r_idx <= c_idx) if reverse else (r_idx >= c_idx)
    tri = keep.astype(F32)
    n_chunks = qk_ref.shape[1] // C
    order = range(n_chunks - 1, -1, -1) if reverse else range(n_chunks)
    for ci in order:
        rows = slice(ci * C, (ci + 1) * C)
        b = jnp.dot(tri, g_ref[0, rows, :], precision=lax.Precision.HIGHEST, preferred_element_type=F32)
        b_last = b[0:1] if reverse else b[C - 1:C]
        e_pos, e_neg, e_end, dec = jnp.exp(b), jnp.exp(-b), jnp.exp(b_last - b), jnp.exp(b_last)
        for h in range(H):
            kc = slice(h * dk, (h + 1) * dk)
            vc = slice(h * dv, (h + 1) * dv)
            q = qk_ref[0, rows, kc]
            k = qk_ref[0, rows, H * dk + h * dk:H * dk + (h + 1) * dk]
            v = v_ref[0, rows, vc]
            q_t = (q * e_pos[:, kc]).astype(BF16)
            k_t = (k * e_neg[:, kc]).astype(BF16)
            k_end = (k * e_end[:, kc]).astype(BF16)
            att = lax.dot_general(q_t, k_t, (((1,), (1,)), ((), ())), preferred_element_type=F32)
            att = jnp.where(keep, att, 0.0).astype(BF16)
            st = st_ref[h]
            o = jnp.dot(att, v, preferred_element_type=F32) + lax.dot_general(
                q_t, st.astype(BF16), (((1,), (1,)), ((), ())), preferred_element_type=F32)
            st_ref[h] = st * dec[:, kc] + lax.dot_general(
                v, k_end, (((0,), (0,)), ((), ())), preferred_element_type=F32)
            if prev_ref is not None:
                o = o + prev_ref[0, rows, vc]
            o_ref[0, rows, vc] = o

    @pl.when(i == pl.num_programs(1) - 1)
    def _():
        sfin_ref[0] = st_ref[...]


def gla_scan(qk, v, gates, s0, direction, prev=None):
    B, L, _ = qk.shape
    reverse = direction == 1
    T = min(512, L)
    nT = L // T
    H, dk, dv = GLA_HEADS, GLA_HK, GLA_HV
    blk = (lambda i: nT - 1 - i) if reverse else (lambda i: i)
    tok = lambda n, col=0: pl.BlockSpec((1, T, n), lambda b, i: (b, blk(i), col))
    st_spec = pl.BlockSpec((1, H, dv, dk), lambda b, i: (b, 0, 0, 0))
    in_specs = [tok(2 * GLA_DK), tok(GLA_DV), tok(GLA_DK, direction), st_spec]
    args = [qk, v, gates, s0]
    if prev is not None:
        in_specs.append(tok(GLA_DV))
        args.append(prev)
    kern = functools.partial(_gla_scan_kernel, reverse=reverse, add_prev=prev is not None)
    return pl.pallas_call(
        kern,
        grid=(B, nT),
        in_specs=in_specs,
        out_specs=[tok(GLA_DV), st_spec],
        out_shape=[jax.ShapeDtypeStruct((B, L, GLA_DV), F32), jax.ShapeDtypeStruct((B, H, dv, dk), F32)],
        scratch_shapes=[pltpu.VMEM((H, dv, dk), F32)],
        compiler_params=pltpu.CompilerParams(
            dimension_semantics=("arbitrary", "arbitrary"), vmem_limit_bytes=VMEM_LIMIT_BYTES),
        name="gla_scan_bwd" if reverse else "gla_scan_fwd",
    )(*args)


def _gla_out_kernel(o_ref, og_ref, x_ref, mod_ref, on_ref, w_ref, out_ref):
    dv = GLA_HV
    og = og_ref[0]
    parts = []
    for h in range(GLA_HEADS):
        o = o_ref[0, :, h * dv:(h + 1) * dv]
        y = o * lax.rsqrt(jnp.mean(o * o, axis=-1, keepdims=True) + NORM_EPS) * on_ref[...]
        gate = og[:, h * dv:(h + 1) * dv]
        parts.append((y * (gate * jax.nn.sigmoid(gate))).astype(BF16))
    a = jnp.concatenate(parts, axis=1)
    out_ref[0] = x_ref[0] + mod_ref[0, 2:3, :] * jnp.dot(a, w_ref[...], preferred_element_type=F32)


def gla_out_proj(o, og, x, mod, onorm, wo):
    B, L, D = x.shape
    tm = min(512, L)
    per_batch = mod.shape[0] != 1
    tok = pl.BlockSpec((1, tm, D), lambda b, i: (b, i, 0))
    return pl.pallas_call(
        _gla_out_kernel,
        grid=(B, L // tm),
        in_specs=[tok, tok, tok,
                  pl.BlockSpec((1, 6, D), (lambda b, i: (b, 0, 0)) if per_batch else (lambda b, i: (0, 0, 0))),
                  _const_spec((1, GLA_HV)), _const_spec(wo.shape)],
        out_specs=tok,
        out_shape=jax.ShapeDtypeStruct((B, L, D), F32),
        compiler_params=pltpu.CompilerParams(
            dimension_semantics=("arbitrary", "arbitrary"), vmem_limit_bytes=VMEM_LIMIT_BYTES),
        name="gla_out_proj",
    )(o, og, x, mod, onorm.reshape(1, GLA_HV), wo)


def gla_layer(x_lat, x_ctx, mod_lat, mod_ctx, norm_g, w_in, gk_w2, gk_b, onorm, wo, ctx_out):
    B = x_lat.shape[0]
    qk_l, v_l, og_l, g_l = gla_in_proj(x_lat, norm_g, mod_lat, w_in, gk_w2, gk_b)
    qk_c, v_c, og_c, g_c = gla_in_proj(x_ctx, norm_g, mod_ctx, w_in, gk_w2, gk_b)
    s0 = jnp.zeros((B, GLA_HEADS, GLA_HV, GLA_HK), F32)
    oc, s_f = gla_scan(qk_c, v_c, g_c, s0, 0)
    oc, s_b = gla_scan(qk_c, v_c, g_c, s0, 1, prev=oc)
    ol, _ = gla_scan(qk_l, v_l, g_l, s_f, 0)
    ol, _ = gla_scan(qk_l, v_l, g_l, s_b, 1, prev=ol)
    wo = wo.astype(BF16)
    x_lat = gla_out_proj(ol, og_l, x_lat, mod_lat, onorm, wo)
    if ctx_out:
        x_ctx = gla_out_proj(oc, og_c, x_ctx, mod_ctx, onorm, wo)
    return x_lat, x_ctx


def _rope_swap_perm():
    half = MLA_ROPE // 2
    quarter = half // 2
    p = []
    for base in (0, half):
        p += list(range(base + quarter, base + half)) + list(range(base, base + quarter))
    return np.asarray(p)


def mla_rope_tables(L, rotate):
    half = MLA_ROPE // 2
    zeros = jnp.zeros((L, MLA_ROPE), F32)
    if not rotate:
        return jnp.concatenate([jnp.ones((L, MLA_ROPE), F32), zeros], axis=1), jnp.zeros((L, 2 * MLA_ROPE), F32)
    pos = jnp.arange(L)
    inv_freq = ROPE_THETA ** (-jnp.arange(0, half, 2, dtype=F32) / half)
    ang_row = (pos // GRID_W).astype(F32)[:, None] * inv_freq[None, :]
    ang_col = (pos % GRID_W).astype(F32)[:, None] * inv_freq[None, :]
    cr, sr, cc, sc = jnp.cos(ang_row), jnp.sin(ang_row), jnp.cos(ang_col), jnp.sin(ang_col)
    cos = jnp.concatenate([cr, cr, cc, cc, zeros], axis=1)
    sin = jnp.concatenate([-sr, sr, -sc, sc, zeros], axis=1)
    return cos, sin


def _mla_qkv_kernel(x_ref, g_ref, mod_ref, wd_ref, qn_ref, wq_ref, kn_ref, wkv_ref, cos_ref, sin_ref,
                    q_ref, k_ref, v_ref):
    h = _norm_mod(x_ref[0], g_ref[...], mod_ref[0, 0:1, :], mod_ref[0, 1:2, :]).astype(BF16)
    c = jnp.dot(h, wd_ref[...], preferred_element_type=F32)
    cos, sin = cos_ref[...], sin_ref[...]
    lanes = cos.shape[1]

    def rms(a, g):
        return (a * lax.rsqrt(jnp.mean(a * a, axis=-1, keepdims=True) + NORM_EPS) * g).astype(BF16)

    def rope(tile):
        return tile * cos + pltpu.roll(tile, lanes // 2, axis=1) * sin

    cq = rms(c[:, :MLA_Q_RANK], qn_ref[...])
    ckv = rms(c[:, MLA_Q_RANK:MLA_Q_RANK + MLA_KV_RANK], kn_ref[...])
    k_rope = rope(c[:, MLA_Q_RANK + MLA_KV_RANK:]).astype(k_ref.dtype)
    q = jnp.dot(cq, wq_ref[...], preferred_element_type=F32)
    kv = jnp.dot(ckv, wkv_ref[...], preferred_element_type=F32)
    ones = jnp.ones((x_ref.shape[1], MLA_V), v_ref.dtype)
    for hd in range(MLA_HEADS):
        o = hd * MLA_QK_PAD
        q_ref[0, :, o:o + MLA_NOPE] = (q[:, o:o + MLA_NOPE] * MLA_Q_PRESCALE).astype(q_ref.dtype)
        q_ref[0, :, o + MLA_NOPE:o + MLA_QK_PAD] = (
            rope(q[:, o + MLA_NOPE:o + MLA_QK_PAD]) * MLA_Q_PRESCALE).astype(q_ref.dtype)
        k_ref[0, :, o:o + MLA_NOPE] = kv[:, o:o + MLA_NOPE].astype(k_ref.dtype)
        k_ref[0, :, o + MLA_NOPE:o + MLA_QK_PAD] = k_rope
        v_ref[0, :, 2 * hd * MLA_V:(2 * hd + 1) * MLA_V] = kv[:, o + MLA_NOPE:o + MLA_QK_PAD].astype(v_ref.dtype)
        v_ref[0, :, (2 * hd + 1) * MLA_V:(2 * hd + 2) * MLA_V] = ones


def mla_qkv_proj(x, norm_g, mod, w_down, qnorm, w_uq, kvnorm, w_ukv, rotate):
    B, L, D = x.shape
    tm = min(512, L)
    per_batch = mod.shape[0] != 1
    perm = _rope_swap_perm()
    rope0 = MLA_Q_RANK + MLA_KV_RANK
    wd = jnp.concatenate([w_down, w_down[:, rope0:][:, perm]], axis=1).astype(BF16)
    wq = w_uq.reshape(MLA_Q_RANK, MLA_HEADS, MLA_NOPE + MLA_ROPE)
    wq = jnp.concatenate([wq, wq[:, :, MLA_NOPE:][:, :, perm]], axis=2)
    wq = wq.reshape(MLA_Q_RANK, MLA_HEADS * MLA_QK_PAD).astype(BF16)
    cos, sin = mla_rope_tables(L, rotate)
    tok = lambda n: pl.BlockSpec((1, tm, n), lambda b, i: (b, i, 0))
    nq = MLA_HEADS * MLA_QK_PAD
    nv = MLA_HEADS * 2 * MLA_V
    tab = pl.BlockSpec((tm, 2 * MLA_ROPE), lambda b, i: (i, 0))
    return pl.pallas_call(
        _mla_qkv_kernel,
        grid=(B, L // tm),
        in_specs=[tok(D), _const_spec((1, D)),
                  pl.BlockSpec((1, 6, D), (lambda b, i: (b, 0, 0)) if per_batch else (lambda b, i: (0, 0, 0))),
                  _const_spec(wd.shape), _const_spec((1, MLA_Q_RANK)), _const_spec(wq.shape),
                  _const_spec((1, MLA_KV_RANK)), _const_spec(w_ukv.shape), tab, tab],
        out_specs=[tok(nq), tok(nq), tok(nv)],
        out_shape=[jax.ShapeDtypeStruct((B, L, nq), BF16), jax.ShapeDtypeStruct((B, L, nq), BF16),
                   jax.ShapeDtypeStruct((B, L, nv), BF16)],
        compiler_params=pltpu.CompilerParams(
            dimension_semantics=("arbitrary", "arbitrary"), vmem_limit_bytes=VMEM_LIMIT_BYTES),
        name="mla_qkv_proj",
    )(x, norm_g.reshape(1, D), mod, wd, qnorm.reshape(1, -1), wq, kvnorm.reshape(1, -1),
      w_ukv.astype(BF16), cos, sin)


def _out_proj_kernel(a_ref, x_ref, mod_ref, w_ref, o_ref):
    o_ref[0] = x_ref[0] + mod_ref[0, 2:3, :] * jnp.dot(a_ref[0], w_ref[...], preferred_element_type=F32)


def out_proj_residual(a, x, mod, w):
    B, L, D = x.shape
    Kd = a.shape[2]
    tm = min(512, L)
    per_batch = mod.shape[0] != 1
    return pl.pallas_call(
        _out_proj_kernel,
        grid=(B, L // tm),
        in_specs=[pl.BlockSpec((1, tm, Kd), lambda b, i: (b, i, 0)),
                  pl.BlockSpec((1, tm, D), lambda b, i: (b, i, 0)),
                  pl.BlockSpec((1, 6, D), (lambda b, i: (b, 0, 0)) if per_batch else (lambda b, i: (0, 0, 0))),
                  _const_spec(w.shape)],
        out_specs=pl.BlockSpec((1, tm, D), lambda b, i: (b, i, 0)),
        out_shape=jax.ShapeDtypeStruct((B, L, D), F32),
        compiler_params=pltpu.CompilerParams(
            dimension_semantics=("arbitrary", "arbitrary"), vmem_limit_bytes=VMEM_LIMIT_BYTES),
        name="out_proj_residual",
    )(a, x, mod, w)


def mla_layer(x_lat, x_ctx, mod_lat, mod_ctx, norm_g, w_down, qnorm, w_uq, kvnorm, w_ukv, wo):
    ql, kl, vl = mla_qkv_proj(x_lat, norm_g, mod_lat, w_down, qnorm, w_uq, kvnorm, w_ukv, True)
    _, kc, vc = mla_qkv_proj(x_ctx, norm_g, mod_ctx, w_down, qnorm, w_uq, kvnorm, w_ukv, False)
    o = mla_attention(ql, kc, vc, kl, vl)
    return out_proj_residual(o, x_lat, mod_lat, wo.astype(BF16))


def kernel(x, c, ctx, c_ctx, ada_w, ada_b, norm1_g, norm2_g, mlp_w1, mlp_w2, final_g, hy_w_in, hy_b_in, hy_conv_w, hy_conv_b, hy_f_w1, hy_f_b1, hy_f_w2, hy_f_b2, hy_f_w3, hy_f_b3, hy_f_w4, hy_freq, hy_bias, hy_w_out, hy_b_out, gla_w_in, gla_gk_w2, gla_gk_b, gla_onorm, gla_wo, mla_w_down, mla_qnorm, mla_w_uq, mla_kvnorm, mla_w_ukv, mla_wo):
    x_lat = x
    x_ctx = ctx
    silu_c = jax.nn.silu(c)
    silu_cc = jax.nn.silu(c_ctx)
    dft = _dft_constants()
    for i in range(DEPTH):
        kind = i % N_MIXERS
        j = i // N_MIXERS
        ctx_live = any(l % N_MIXERS != 0 for l in range(i + 1, DEPTH))
        mod_lat = (silu_c @ ada_w[i] + ada_b[i]).reshape(-1, 6, D_MODEL)
        mod_ctx = (silu_cc @ ada_w[i] + ada_b[i]).reshape(1, 6, D_MODEL)
        if kind == 0:
            hp = (hy_w_in[j], hy_b_in[j], hy_conv_w[j], hy_conv_b[j], hy_f_w1[j], hy_f_b1[j],
                  hy_f_w2[j], hy_f_b2[j], hy_f_w3[j], hy_f_b3[j], hy_f_w4[j], hy_freq[j],
                  hy_bias[j], hy_w_out[j], hy_b_out[j])
            x_lat = hyena_layer(x_lat, mod_lat, norm1_g[i], *hp, dft)
            if ctx_live:
                x_ctx = hyena_layer(x_ctx, mod_ctx, norm1_g[i], *hp, dft)
        elif kind == 1:
            x_lat, x_ctx = gla_layer(x_lat, x_ctx, mod_lat, mod_ctx, norm1_g[i], gla_w_in[j], gla_gk_w2[j],
                                     gla_gk_b[j], gla_onorm[j], gla_wo[j], ctx_live)
        else:
            assert not ctx_live
            x_lat = mla_layer(x_lat, x_ctx, mod_lat, mod_ctx, norm1_g[i], mla_w_down[j], mla_qnorm[j],
                              mla_w_uq[j], mla_kvnorm[j], mla_w_ukv[j], mla_wo[j])
        w1 = mlp_w1[i].astype(BF16)
        w2 = mlp_w2[i].astype(BF16)
        x_lat = mlp_block(x_lat, norm2_g[i], mod_lat, w1, w2,
                          final_g=final_g if i == DEPTH - 1 else None)
        if ctx_live:
            x_ctx = mlp_block(x_ctx, norm2_g[i], mod_ctx, w1, w2)
    return x_lat
```

```python
import functools
import math

import jax
import jax.numpy as jnp
import numpy as np
from jax import lax
from jax.experimental import pallas as pl
from jax.experimental.pallas import tpu as pltpu

F32 = jnp.float32
BF16 = jnp.bfloat16

D_MODEL = 1024
DEPTH = 4
GRID_W = 64
N_MIXERS = 3
NORM_EPS = 1e-6

HY_ORDER = 2
HY_EMB = 33
HY_SHORT = 3
HY_FAST_DECAY = 0.3
HY_SLOW_DECAY = 1.5
HY_TARGET = 1e-2

GLA_HEADS = 4
GLA_DK = D_MODEL // 2
GLA_DV = D_MODEL
GLA_HK = GLA_DK // GLA_HEADS
GLA_HV = GLA_DV // GLA_HEADS
GLA_GATE_RANK = 16
GLA_GATE_NORM = 16.0
GLA_CHUNK = 64

MLA_HEADS = 8
MLA_Q_RANK = 384
MLA_KV_RANK = 256
MLA_NOPE = 128
MLA_ROPE = 64
MLA_V = 128
ROPE_THETA = 10000.0
Q_BLOCK = 128

VMEM_LIMIT_BYTES = 56 * 1024 * 1024
MLA_QK_PAD = 256
SUBLANES = 8
MLA_Q_PRESCALE = (MLA_NOPE + MLA_ROPE) ** -0.5 * math.log2(math.e)


def _const_spec(shape):
    nd = len(shape)
    return pl.BlockSpec(shape, lambda *_: (0,) * nd, pipeline_mode=pl.Buffered(1))


def _norm_mod(x, g, shift, scale):
    y = x * lax.rsqrt(jnp.mean(x * x, axis=-1, keepdims=True) + NORM_EPS)
    return (y * g) * (1.0 + scale) + shift


def _layer_tail_kernel(*refs, kind, hidden_chunk, final_norm):
    if kind == "hyena":
        y_ref, wo_ref, bo_ref = refs[:3]
        rest = refs[3:]
        y = lax.dot_general(y_ref[0], wo_ref[...], (((0,), (0,)), ((), ())),
                            preferred_element_type=F32) + bo_ref[...]
    elif kind == "gla":
        o_ref_in, og_ref, on_ref, wo_ref = refs[:4]
        rest = refs[4:]
        og = og_ref[0]
        parts = []
        for h in range(GLA_HEADS):
            cols = slice(h * GLA_HV, (h + 1) * GLA_HV)
            o = o_ref_in[0, :, cols]
            n = o * lax.rsqrt(jnp.mean(o * o, axis=-1, keepdims=True) + NORM_EPS) * on_ref[...]
            parts.append((n * (og[:, cols] * jax.nn.sigmoid(og[:, cols]))).astype(BF16))
        y = jnp.dot(jnp.concatenate(parts, axis=1), wo_ref[...], preferred_element_type=F32)
    else:
        a_ref, wo_ref = refs[:2]
        rest = refs[2:]
        y = jnp.dot(a_ref[0], wo_ref[...], preferred_element_type=F32)
    x_ref, g_ref, mod_ref, w1_ref, w2_ref, fg_ref, out_ref = rest
    x = x_ref[0] + mod_ref[0, 2:3, :] * y
    h = _norm_mod(x, g_ref[...], mod_ref[0, 3:4, :], mod_ref[0, 4:5, :]).astype(BF16)
    hidden = w1_ref.shape[1]
    acc = jnp.zeros(x.shape, F32)
    for c0 in range(0, hidden, hidden_chunk):
        a = jnp.dot(h, w1_ref[:, c0:c0 + hidden_chunk], preferred_element_type=F32)
        a = jnp.square(jnp.maximum(a, 0.0)).astype(BF16)
        acc = acc + jnp.dot(a, w2_ref[c0:c0 + hidden_chunk, :], preferred_element_type=F32)
    out = x + mod_ref[0, 5:6, :] * acc
    if final_norm:
        out = (out * lax.rsqrt(jnp.mean(out * out, axis=-1, keepdims=True) + NORM_EPS)) * fg_ref[...]
    out_ref[0] = out


def layer_tail(kind, mixer_args, x, mod, norm_g, w1, w2, final_g=None):
    B, L, D = x.shape
    tm = min(512, L)
    per_batch = mod.shape[0] != 1
    final_norm = final_g is not None
    fg = (final_g if final_norm else norm_g).reshape(1, D)
    tok = lambda n: pl.BlockSpec((1, tm, n), lambda b, i: (b, i, 0))
    if kind == "hyena":
        y_t, w_o, b_o = mixer_args
        m_specs = [pl.BlockSpec((1, D, tm), lambda b, i: (b, 0, i)), _const_spec(w_o.shape), _const_spec((1, D))]
        m_args = [y_t, w_o, b_o.reshape(1, D)]
    elif kind == "gla":
        o, og, onorm, w_o = mixer_args
        m_specs = [tok(D), tok(D), _const_spec((1, GLA_HV)), _const_spec(w_o.shape)]
        m_args = [o, og, onorm.reshape(1, GLA_HV), w_o]
    else:
        a, w_o = mixer_args
        m_specs = [tok(a.shape[2]), _const_spec(w_o.shape)]
        m_args = [a, w_o]
    kern = functools.partial(_layer_tail_kernel, kind=kind, hidden_chunk=1024, final_norm=final_norm)
    return pl.pallas_call(
        kern,
        grid=(B, L // tm),
        in_specs=m_specs + [
            tok(D),
            _const_spec((1, D)),
            pl.BlockSpec((1, 6, D), (lambda b, i: (b, 0, 0)) if per_batch else (lambda b, i: (0, 0, 0))),
            _const_spec(w1.shape),
            _const_spec(w2.shape),
            _const_spec((1, D)),
        ],
        out_specs=tok(D),
        out_shape=jax.ShapeDtypeStruct((B, L, D), F32),
        compiler_params=pltpu.CompilerParams(
            dimension_semantics=("arbitrary", "arbitrary"), vmem_limit_bytes=VMEM_LIMIT_BYTES),
        name="layer_tail_" + kind,
    )(*m_args, x, norm_g.reshape(1, D), mod, w1, w2, fg)


def _attn_kernel(q_ref, kc_ref, vc_ref, k_ref, v_ref, o_ref, m_ref, acc_ref, sa_ref, sb_ref, *, sub):
    q = q_ref[0]
    m_ref[...] = jnp.full(m_ref.shape, -jnp.inf, F32)
    acc_ref[...] = jnp.zeros(acc_ref.shape, F32)
    lanes = m_ref.shape[1]
    n_sub = k_ref.shape[1] // sub

    def scores(k):
        return lax.dot_general(q, k, (((1,), (1,)), ((), ())), preferred_element_type=F32)

    def lat(ref, n):
        return ref[0, pl.ds(pl.multiple_of(n * sub, sub), sub), :]

    def accumulate(s, v):
        m_prev = m_ref[...]
        m_new = jnp.maximum(m_prev, jnp.max(s, axis=-1, keepdims=True))
        alpha = jnp.exp2(m_prev - m_new)
        ps = [jnp.exp2(s[:, t:t + lanes] - m_new) for t in range(0, s.shape[1], lanes)]
        p = jnp.concatenate(ps, axis=1).astype(BF16)
        pv = jnp.dot(p, v, preferred_element_type=F32)
        for t in range(0, acc_ref.shape[1], lanes):
            acc_ref[:, t:t + lanes] = alpha * acc_ref[:, t:t + lanes] + pv[:, t:t + lanes]
        m_ref[...] = m_new

    sa_ref[...] = scores(lat(k_ref, 0))
    accumulate(scores(kc_ref[0]), vc_ref[0])

    def pair(n):
        sb_ref[...] = scores(lat(k_ref, n + 1))
        accumulate(sa_ref[...], lat(v_ref, n))

    def body(j, carry):
        n = 2 * j
        pair(n)
        sa_ref[...] = scores(lat(k_ref, n + 2))
        accumulate(sb_ref[...], lat(v_ref, n + 1))
        return carry

    lax.fori_loop(0, n_sub // 2 - 1, body, 0)
    pair(n_sub - 2)
    accumulate(sb_ref[...], lat(v_ref, n_sub - 1))
    o_ref[0] = (acc_ref[:, 0:lanes] / acc_ref[:, lanes:2 * lanes]).astype(o_ref.dtype)


def mla_attention(q, kc, vc, k, v):
    B, L, _ = q.shape
    C = kc.shape[1]
    H = MLA_HEADS
    tq = min(2048, L)
    sub = min(512, L // 4)
    assert L % (2 * sub) == 0 and L % tq == 0
    kern = functools.partial(_attn_kernel, sub=sub)
    return pl.pallas_call(
        kern,
        grid=(B, H, L // tq),
        in_specs=[
            pl.BlockSpec((1, tq, MLA_QK_PAD), lambda b, h, i: (b, i, h)),
            pl.BlockSpec((1, C, MLA_QK_PAD), lambda b, h, i: (b, 0, h)),
            pl.BlockSpec((1, C, 2 * MLA_V), lambda b, h, i: (b, 0, h)),
            pl.BlockSpec((1, L, MLA_QK_PAD), lambda b, h, i: (b, 0, h)),
            pl.BlockSpec((1, L, 2 * MLA_V), lambda b, h, i: (b, 0, h)),
        ],
        out_specs=pl.BlockSpec((1, tq, MLA_V), lambda b, h, i: (b, i, h)),
        out_shape=jax.ShapeDtypeStruct((B, L, H * MLA_V), BF16),
        scratch_shapes=[pltpu.VMEM((tq, MLA_V), F32), pltpu.VMEM((tq, 2 * MLA_V), F32),
                        pltpu.VMEM((tq, sub), F32), pltpu.VMEM((tq, sub), F32)],
        compiler_params=pltpu.CompilerParams(
            dimension_semantics=("arbitrary", "arbitrary", "arbitrary"), vmem_limit_bytes=VMEM_LIMIT_BYTES),
        name="mla_attention",
    )(q, kc, vc, k, v)


DFT_N1 = 128
HY_CH_BLOCK = 32
HY_TAP_ROWS = 64


def _dft_constants():
    n1 = DFT_N1
    n = n1 * n1
    idx = np.arange(n1, dtype=np.float64)
    th = 2.0 * np.pi * np.outer(idx, idx) / n1
    cos1, sin1 = np.cos(th), np.sin(th)
    tw = 2.0 * np.pi * np.outer(idx, idx) / n
    fr, fi = cos1, -sin1
    c = dict(
        e1=np.concatenate([cos1, -sin1], axis=0),
        tr=np.cos(tw), ti=-np.sin(tw),
        g=np.block([[fr, fi], [-fi, fr]]),
        gbar=np.block([[fr, -fi], [fi, fr]]),
        e2=np.concatenate([cos1, -sin1], axis=1) / n,
    )
    return {k: jnp.asarray(v, F32) for k, v in c.items()}


def _hy_in_kernel(x_ref, g_ref, mod_ref, wt_ref, b_ref, o_ref, *, row_chunk):
    h = _norm_mod(x_ref[0], g_ref[...], mod_ref[0, 0:1, :], mod_ref[0, 1:2, :]).astype(BF16)
    for r0 in range(0, wt_ref.shape[0], row_chunk):
        z = lax.dot_general(wt_ref[r0:r0 + row_chunk, :], h, (((1,), (1,)), ((), ())),
                            preferred_element_type=F32)
        o_ref[0, r0:r0 + row_chunk, :] = z + b_ref[r0:r0 + row_chunk, :]


def hyena_in_proj(x, norm_g, mod, w_in_t, b_in):
    B, L, D = x.shape
    n_out = w_in_t.shape[0]
    tm = min(512, L)
    per_batch = mod.shape[0] != 1
    return pl.pallas_call(
        functools.partial(_hy_in_kernel, row_chunk=512),
        grid=(B, L // tm),
        in_specs=[
            pl.BlockSpec((1, tm, D), lambda b, i: (b, i, 0)),
            _const_spec((1, D)),
            pl.BlockSpec((1, 6, D), (lambda b, i: (b, 0, 0)) if per_batch else (lambda b, i: (0, 0, 0))),
            _const_spec(w_in_t.shape),
            _const_spec(b_in.shape),
        ],
        out_specs=pl.BlockSpec((1, n_out, tm), lambda b, i: (b, 0, i)),
        out_shape=jax.ShapeDtypeStruct((B, n_out, L), F32),
        compiler_params=pltpu.CompilerParams(
            dimension_semantics=("arbitrary", "arbitrary"), vmem_limit_bytes=VMEM_LIMIT_BYTES),
        name="hyena_in_proj",
    )(x, norm_g.reshape(1, D), mod, w_in_t, b_in)


def _hy_hidden_kernel(zf_ref, w1_ref, b1_ref, w2_ref, b2_ref, w3_ref, b3_ref, fr_ref, o_ref):
    fr = fr_ref[...]
    h = zf_ref[...].astype(BF16)
    for w_ref, b_ref in ((w1_ref, b1_ref), (w2_ref, b2_ref), (w3_ref, b3_ref)):
        h = jnp.sin(fr * (jnp.dot(w_ref[...], h, preferred_element_type=F32) + b_ref[...]))
        out = h
        h = h.astype(BF16)
    o_ref[...] = out


def _hy_taps_kernel(hid_ref, t_ref, w4_ref, dl_ref, o_ref, *, half):
    hid = hid_ref[...].astype(BF16)
    tf = jnp.dot(w4_ref[0, 0].astype(BF16), hid[:, :half], preferred_element_type=F32)
    tb = jnp.dot(w4_ref[0, 1].astype(BF16), hid[:, half:], preferred_element_type=F32)
    taps = jnp.concatenate([tf, tb], axis=1) * jnp.exp(-t_ref[...] * dl_ref[...])
    pos = lax.broadcasted_iota(jnp.int32, taps.shape, 1)
    taps = jnp.where(pos == half, 0.0, taps)
    o_ref[0] = taps / jnp.sum(jnp.abs(taps), axis=1, keepdims=True)


def hyena_filter_taps(L, f_w1, f_b1, f_w2, f_b2, f_w3, f_b3, f_w4, freq):
    width = f_w1.shape[1]
    n = 2 * L
    pos = np.arange(n)
    pos = np.where(pos <= L, np.minimum(pos, L - 1), n - pos).astype(np.float64)
    bands = (HY_EMB - 1) // 2
    t = jnp.asarray(pos / (L - 1), F32)[None, :]
    w = 2.0 * math.pi * jnp.asarray(pos, F32)[None, :] / L
    f = jnp.linspace(1e-4, bands - 1, bands, dtype=F32)[:, None]
    zf = jnp.concatenate([t, jnp.cos(f * w), -jnp.sin(f * w)], axis=0)
    zf = jnp.pad(zf, ((0, width - HY_EMB), (0, 0)))
    w1t = jnp.pad(f_w1.T, ((0, 0), (0, width - HY_EMB))).astype(BF16)
    col = lambda v: v.reshape(width, 1).astype(F32)
    lane_blk = min(2048, n)
    hidden = pl.pallas_call(
        _hy_hidden_kernel,
        grid=(n // lane_blk,),
        in_specs=[pl.BlockSpec((width, lane_blk), lambda i: (0, i))] + [_const_spec((width, width)), _const_spec((width, 1))] * 3
        + [_const_spec((width, 1))],
        out_specs=pl.BlockSpec((width, lane_blk), lambda i: (0, i)),
        out_shape=jax.ShapeDtypeStruct((width, n), F32),
        name="hyena_filter_hidden",
    )(zf, w1t, col(f_b1), f_w2.T.astype(BF16), col(f_b2), f_w3.T.astype(BF16), col(f_b3), col(freq))
    max_decay = math.log(HY_TARGET) / HY_FAST_DECAY
    min_decay = math.log(HY_TARGET) / HY_SLOW_DECAY
    deltas = jnp.abs(jnp.linspace(min_decay, max_decay, D_MODEL, dtype=F32)).reshape(D_MODEL, 1)
    w4t = f_w4.T.reshape(HY_ORDER, 2, D_MODEL, width)
    rows = HY_TAP_ROWS
    return pl.pallas_call(
        functools.partial(_hy_taps_kernel, half=L),
        grid=(HY_ORDER, D_MODEL // rows),
        in_specs=[
            _const_spec((width, n)),
            _const_spec((1, n)),
            pl.BlockSpec((1, 2, rows, width), lambda o, i: (o, 0, i, 0)),
            pl.BlockSpec((rows, 1), lambda o, i: (i, 0)),
        ],
        out_specs=pl.BlockSpec((1, rows, n), lambda o, i: (o, i, 0)),
        out_shape=jax.ShapeDtypeStruct((HY_ORDER, D_MODEL, n), F32),
        compiler_params=pltpu.CompilerParams(
            dimension_semantics=("arbitrary", "arbitrary"), vmem_limit_bytes=VMEM_LIMIT_BYTES),
        name="hyena_filter_taps",
    )(hidden, t, w4t, deltas)


def _dft_stage1(x, c, e1, tr, ti, zb_ref):
    n1 = DFT_N1
    z = jnp.dot(e1, x, preferred_element_type=F32)
    zr, zi = z[:n1], z[n1:]
    r0 = pl.multiple_of(c * n1, n1)
    zb_ref[pl.ds(r0, n1), 0:n1] = (zr * tr - zi * ti).astype(BF16)
    zb_ref[pl.ds(r0, n1), n1:2 * n1] = (zr * ti + zi * tr).astype(BF16)


def _hy_spectrum_kernel(x_ref, e1_ref, tr_ref, ti_ref, g_ref, o_ref, zb_ref):
    n_ch = x_ref.shape[0]
    e1, tr, ti = e1_ref[...], tr_ref[...], ti_ref[...]

    def stage1(c, carry):
        _dft_stage1(x_ref[c].astype(BF16), c, e1, tr, ti, zb_ref)
        return carry

    lax.fori_loop(0, n_ch, stage1, 0, unroll=8)
    xh = jnp.dot(zb_ref[...], g_ref[...], preferred_element_type=F32)
    o_ref[...] = xh.reshape(o_ref.shape).astype(o_ref.dtype)


def hyena_filter_spectrum(taps, consts):
    R = taps.shape[0]
    n1 = DFT_N1
    C = HY_CH_BLOCK
    x = taps.reshape(R, n1, n1)
    return pl.pallas_call(
        _hy_spectrum_kernel,
        grid=(R // C,),
        in_specs=[pl.BlockSpec((C, n1, n1), lambda i: (i, 0, 0)), _const_spec((2 * n1, n1)),
                  _const_spec((n1, n1)), _const_spec((n1, n1)), _const_spec((2 * n1, 2 * n1))],
        out_specs=pl.BlockSpec((C, n1, 2 * n1), lambda i: (i, 0, 0)),
        out_shape=jax.ShapeDtypeStruct((R, n1, 2 * n1), BF16),
        scratch_shapes=[pltpu.VMEM((C * n1, 2 * n1), BF16)],
        compiler_params=pltpu.CompilerParams(
            dimension_semantics=("arbitrary",), vmem_limit_bytes=VMEM_LIMIT_BYTES),
        name="hyena_filter_spectrum",
    )(x, consts["e1"].astype(BF16), consts["tr"], consts["ti"], consts["g"].astype(BF16))


def _hy_conv_kernel(cw_ref, cb_ref, sk_ref, zy_ref, zg1_ref, zg2_ref, hh_ref, e1_ref, tr_ref, ti_ref,
                    g_ref, gb_ref, e2_ref, o_ref, y_sc, g1_sc, g2_sc, zb_sc, u_sc, pad_sc, *, n_ch, d_model):
    n1 = DFT_N1
    rows = zy_ref.shape[2]
    ch0 = pl.program_id(0) * n_ch
    b_idx = lax.broadcasted_iota(jnp.int32, (rows, n1), 1)
    e1, tr, ti, e2 = e1_ref[...], tr_ref[...], ti_ref[...], e2_ref[...]

    pad = pad_sc.shape[1] - rows
    top = pad // 2
    zero_rows = jnp.zeros((top, n1), F32)

    def short_conv(z_ref, c, slot, col):
        z = z_ref[0, c]
        pad_sc[slot, 0:top, :] = zero_rows
        pad_sc[slot, top + rows:pad + rows, :] = zero_rows
        pad_sc[slot, top:top + rows, :] = z
        up = pad_sc[slot, top - 1:top - 1 + rows, :]
        down = pad_sc[slot, top + 1:top + 1 + rows, :]
        prev = pltpu.roll(jnp.where(b_idx == n1 - 1, up, z), 1, axis=1)
        nxt = pltpu.roll(jnp.where(b_idx == 0, down, z), n1 - 1, axis=1)
        n_col = 3 * d_model
        return cw_ref[col] * prev + cw_ref[n_col + col] * z + cw_ref[2 * n_col + col] * nxt + cb_ref[col]

    gate_scs = (g1_sc, g2_sc)

    def spectral_product(order):
        xh = jnp.dot(zb_sc[...], g_ref[...], preferred_element_type=F32)
        hh = hh_ref[order].reshape(n_ch * n1, 2 * n1).astype(F32)
        xr, xi, hr, hi = xh[:, :n1], xh[:, n1:], hh[:, :n1], hh[:, n1:]
        yh = jnp.concatenate([xr * hr - xi * hi, xr * hi + xi * hr], axis=1).astype(BF16)
        u_sc[...] = jnp.dot(yh, gb_ref[...], preferred_element_type=F32)

    def finish(c, order):
        r0 = pl.multiple_of(c * n1, n1)
        ur, ui = u_sc[pl.ds(r0, n1), 0:n1], u_sc[pl.ds(r0, n1), n1:2 * n1]
        stacked = jnp.concatenate([ur * tr + ui * ti, ui * tr - ur * ti], axis=0).astype(BF16)
        conv = jnp.dot(e2, stacked, preferred_element_type=F32)
        return gate_scs[order][c] * (conv + y_sc[c] * sk_ref[order * d_model + ch0 + c])

    def head(c, carry):
        y = short_conv(zy_ref, c, 0, ch0 + c)
        y_sc[c] = y
        g1_sc[c] = short_conv(zg1_ref, c, 1, d_model + ch0 + c)
        g2_sc[c] = short_conv(zg2_ref, c, 2, 2 * d_model + ch0 + c)
        _dft_stage1(y.astype(BF16), c, e1, tr, ti, zb_sc)
        return carry

    lax.fori_loop(0, n_ch, head, 0, unroll=4)
    for order in range(HY_ORDER - 1):
        spectral_product(order)

        def middle(c, carry, order=order):
            y_sc[c] = finish(c, order)
            return carry

        def restart(c, carry):
            _dft_stage1(y_sc[c].astype(BF16), c, e1, tr, ti, zb_sc)
            return carry

        lax.fori_loop(0, n_ch, middle, 0, unroll=8)
        lax.fori_loop(0, n_ch, restart, 0, unroll=8)
    spectral_product(HY_ORDER - 1)

    def tail(c, carry):
        o_ref[0, c] = finish(c, HY_ORDER - 1).astype(o_ref.dtype)
        return carry

    lax.fori_loop(0, n_ch, tail, 0, unroll=8)


def hyena_long_conv(z_t, spectrum, conv_w, conv_b, skip, consts):
    B, n_col, L = z_t.shape
    D = n_col // (HY_ORDER + 1)
    n1 = DFT_N1
    rows = L // n1
    assert 2 * rows == n1 and HY_ORDER == 2
    C = HY_CH_BLOCK
    nblk = D // C
    z4 = z_t.reshape(B, n_col, rows, n1)
    smem = pl.BlockSpec(memory_space=pltpu.SMEM)
    zspec = lambda off: pl.BlockSpec((1, C, rows, n1), lambda i, b: (b, i + off * nblk, 0, 0))
    kern = functools.partial(_hy_conv_kernel, n_ch=C, d_model=D)
    out = pl.pallas_call(
        kern,
        grid=(nblk, B),
        in_specs=[smem, smem, smem, zspec(0), zspec(1), zspec(2),
                  pl.BlockSpec((HY_ORDER, C, n1, 2 * n1), lambda i, b: (0, i, 0, 0)),
                  _const_spec((2 * n1, rows)), _const_spec((n1, n1)), _const_spec((n1, n1)),
                  _const_spec((2 * n1, 2 * n1)), _const_spec((2 * n1, 2 * n1)), _const_spec((rows, 2 * n1))],
        out_specs=pl.BlockSpec((1, C, rows, n1), lambda i, b: (b, i, 0, 0)),
        out_shape=jax.ShapeDtypeStruct((B, D, rows, n1), BF16),
        scratch_shapes=[pltpu.VMEM((C, rows, n1), F32), pltpu.VMEM((C, rows, n1), F32), pltpu.VMEM((C, rows, n1), F32),
                        pltpu.VMEM((C * n1, 2 * n1), BF16), pltpu.VMEM((C * n1, 2 * n1), F32),
                        pltpu.VMEM((HY_ORDER + 1, rows + 2 * SUBLANES, n1), F32)],
        compiler_params=pltpu.CompilerParams(
            dimension_semantics=("arbitrary", "arbitrary"), vmem_limit_bytes=VMEM_LIMIT_BYTES),
        name="hyena_long_conv",
    )(conv_w.reshape(-1), conv_b.reshape(-1), skip.reshape(-1), z4, z4, z4, spectrum,
      consts["e1"][:, :rows].astype(BF16), consts["tr"], consts["ti"], consts["g"].astype(BF16),
      consts["gbar"].astype(BF16), consts["e2"][:rows].astype(BF16))
    return out.reshape(B, D, L)


def _hy_short_seq_kernel(zy_ref, zg1_ref, zg2_ref, taps_ref, par_ref, dfull_ref, dinv_ref, o_ref):
    rows, L = zy_ref.shape[1], zy_ref.shape[2]
    lane = lax.broadcasted_iota(jnp.int32, (rows, L), 1)
    par = par_ref[...]
    nfreq = dinv_ref.shape[0] // 2

    def short_conv(z, grp):
        prev = jnp.where(lane == 0, 0.0, pltpu.roll(z, 1, axis=1))
        nxt = jnp.where(lane == L - 1, 0.0, pltpu.roll(z, L - 1, axis=1))
        c = 4 * grp
        return par[:, c:c + 1] * prev + par[:, c + 1:c + 2] * z + par[:, c + 2:c + 3] * nxt + par[:, c + 3:c + 4]

    y = short_conv(zy_ref[0], 0)
    gates = (short_conv(zg1_ref[0], 1), short_conv(zg2_ref[0], 2))
    d_first = dfull_ref[0:L, :]
    for order in range(HY_ORDER):
        hh = jnp.dot(taps_ref[order].astype(BF16), dfull_ref[...], preferred_element_type=F32)
        xh = jnp.dot(y.astype(BF16), d_first, preferred_element_type=F32)
        xr, xi, hr, hi = xh[:, :nfreq], xh[:, nfreq:], hh[:, :nfreq], hh[:, nfreq:]
        yh = jnp.concatenate([xr * hr - xi * hi, xr * hi + xi * hr], axis=1).astype(BF16)
        conv = jnp.dot(yh, dinv_ref[...], preferred_element_type=F32)
        y = gates[order] * (conv + y * par[:, 12 + order:13 + order])
    o_ref[0] = y.astype(o_ref.dtype)


def hyena_short_seq_conv(z_t, taps, conv_w, conv_b, skip):
    B, n_col, L = z_t.shape
    D = n_col // (HY_ORDER + 1)
    n = 2 * L
    idx = np.arange(n, dtype=np.float64)
    ang = 2.0 * np.pi * np.outer(idx, idx) / n
    dfull = jnp.asarray(np.concatenate([np.cos(ang), -np.sin(ang)], axis=1), F32)
    dinv = jnp.asarray(np.concatenate([np.cos(ang), -np.sin(ang)], axis=0)[:, :L] / n, F32)
    cw = conv_w.reshape(HY_SHORT, HY_ORDER + 1, D)
    cb = conv_b.reshape(1, HY_ORDER + 1, D)
    par = jnp.concatenate([cw, cb], axis=0)
    par = jnp.transpose(par, (2, 1, 0)).reshape(D, 4 * (HY_ORDER + 1))
    par = jnp.concatenate([par, skip.T, jnp.zeros((D, 2), F32)], axis=1)
    rows = 256
    nblk = D // rows
    zspec = lambda off: pl.BlockSpec((1, rows, L), lambda i, b: (b, i + off * nblk, 0))
    return pl.pallas_call(
        _hy_short_seq_kernel,
        grid=(nblk, B),
        in_specs=[zspec(0), zspec(1), zspec(2),
                  pl.BlockSpec((HY_ORDER, rows, n), lambda i, b: (0, i, 0)),
                  pl.BlockSpec((rows, par.shape[1]), lambda i, b: (i, 0)),
                  _const_spec(dfull.shape), _const_spec(dinv.shape)],
        out_specs=pl.BlockSpec((1, rows, L), lambda i, b: (b, i, 0)),
        out_shape=jax.ShapeDtypeStruct((B, D, L), BF16),
        compiler_params=pltpu.CompilerParams(
            dimension_semantics=("arbitrary", "arbitrary"), vmem_limit_bytes=VMEM_LIMIT_BYTES),
        name="hyena_short_seq_conv",
    )(z_t, z_t, z_t, taps, par, dfull.astype(BF16), dinv.astype(BF16))


def hyena_layer(x, mod, norm1_g, w_in, b_in, conv_w, conv_b, f_w1, f_b1, f_w2, f_b2, f_w3, f_b3, f_w4,
                freq, bias, w_out, b_out, dft, tail):
    L = x.shape[1]
    taps = hyena_filter_taps(L, f_w1, f_b1, f_w2, f_b2, f_w3, f_b3, f_w4, freq)
    z_t = hyena_in_proj(x, norm1_g, mod, w_in.T.astype(BF16), b_in.reshape(-1, 1))
    if 2 * L == DFT_N1 * DFT_N1:
        spec = hyena_filter_spectrum(taps.reshape(HY_ORDER * D_MODEL, 2 * L), dft)
        spec = spec.reshape(HY_ORDER, D_MODEL, DFT_N1, 2 * DFT_N1)
        y_t = hyena_long_conv(z_t, spec, conv_w, conv_b, bias, dft)
    else:
        y_t = hyena_short_seq_conv(z_t, taps, conv_w, conv_b, bias)
    return layer_tail("hyena", (y_t, w_out.astype(BF16), b_out), x, mod, *tail)


def _gla_in_kernel(x_ref, g_ref, mod_ref, w_ref, wr_ref, w2_ref, gb_ref, qk_ref, v_ref, og_ref, gate_ref):
    h = _norm_mod(x_ref[0], g_ref[...], mod_ref[0, 0:1, :], mod_ref[0, 1:2, :]).astype(BF16)
    n = qk_ref.shape[2]
    qk = jnp.dot(h, w_ref[:, 0:n], preferred_element_type=F32)
    half = n // 2
    qk_ref[0, :, 0:half] = qk[:, 0:half] * (GLA_HK ** -0.5)
    qk_ref[0, :, half:n] = qk[:, half:n]
    v_ref[0] = jnp.dot(h, w_ref[:, n:2 * n], preferred_element_type=F32).astype(v_ref.dtype)
    og_ref[0] = jnp.dot(h, w_ref[:, 2 * n:3 * n], preferred_element_type=F32)
    r = jnp.dot(h, wr_ref[...], preferred_element_type=F32).astype(BF16)
    gk = jnp.dot(r, w2_ref[...], preferred_element_type=F32) + gb_ref[...]
    gate_ref[0] = -(jnp.maximum(-gk, 0.0) + jnp.log1p(jnp.exp(-jnp.abs(gk)))) * (1.0 / GLA_GATE_NORM)


def gla_in_proj(x, norm_g, mod, w_in, gk_w2, gk_b):
    B, L, D = x.shape
    tm = min(512, L)
    per_batch = mod.shape[0] != 1
    n_main = 2 * GLA_DK + 2 * GLA_DV
    w_main = w_in[:, :n_main].astype(BF16)
    lanes = 128
    w_r = jnp.pad(w_in[:, n_main:], ((0, 0), (0, lanes - 2 * GLA_GATE_RANK))).astype(BF16)
    w2 = jnp.zeros((lanes, 2 * GLA_DK), F32)
    w2 = w2.at[:GLA_GATE_RANK, :GLA_DK].set(gk_w2[0]).at[GLA_GATE_RANK:2 * GLA_GATE_RANK, GLA_DK:].set(gk_w2[1])
    tok = lambda n: pl.BlockSpec((1, tm, n), lambda b, i: (b, i, 0))
    n = 2 * GLA_DK
    assert GLA_DV == n
    return pl.pallas_call(
        _gla_in_kernel,
        grid=(B, L // tm),
        in_specs=[tok(D), _const_spec((1, D)),
                  pl.BlockSpec((1, 6, D), (lambda b, i: (b, 0, 0)) if per_batch else (lambda b, i: (0, 0, 0))),
                  _const_spec(w_main.shape), _const_spec(w_r.shape), _const_spec(w2.shape), _const_spec((1, n))],
        out_specs=[tok(n), tok(n), tok(n), tok(n)],
        out_shape=[jax.ShapeDtypeStruct((B, L, n), F32), jax.ShapeDtypeStruct((B, L, n), BF16),
                   jax.ShapeDtypeStruct((B, L, n), F32), jax.ShapeDtypeStruct((B, L, n), F32)],
        compiler_params=pltpu.CompilerParams(
            dimension_semantics=("arbitrary", "arbitrary"), vmem_limit_bytes=VMEM_LIMIT_BYTES),
        name="gla_in_proj",
    )(x, norm_g.reshape(1, D), mod, w_main, w_r, w2.astype(BF16), gk_b.reshape(1, n))


def _gla_scan_kernel(*refs, reverse, add_prev):
    if add_prev:
        qk_ref, v_ref, g_ref, s0_ref, prev_ref, o_ref, sfin_ref, st_ref = refs
    else:
        qk_ref, v_ref, g_ref, s0_ref, o_ref, sfin_ref, st_ref = refs
        prev_ref = None
    i = pl.program_id(1)
    C, H, dk, dv = GLA_CHUNK, GLA_HEADS, GLA_HK, GLA_HV

    @pl.when(i == 0)
    def _():
        st_ref[...] = s0_ref[0]

    r_idx = lax.broadcasted_iota(jnp.int32, (C, C), 0)
    c_idx = lax.broadcasted_iota(jnp.int32, (C, C), 1)
    keep = (r_idx <= c_idx) if reverse else (r_idx >= c_idx)
    tri = keep.astype(F32)
    n_chunks = qk_ref.shape[1] // C
    order = range(n_chunks - 1, -1, -1) if reverse else range(n_chunks)
    for ci in order:
        rows = slice(ci * C, (ci + 1) * C)
        b = jnp.dot(tri, g_ref[0, rows, :], precision=lax.Precision.HIGHEST, preferred_element_type=F32)
        b_last = b[0:1] if reverse else b[C - 1:C]
        e_pos, e_neg, e_end, dec = jnp.exp(b), jnp.exp(-b), jnp.exp(b_last - b), jnp.exp(b_last)
        for h in range(H):
            kc = slice(h * dk, (h + 1) * dk)
            vc = slice(h * dv, (h + 1) * dv)
            q = qk_ref[0, rows, kc]
            k = qk_ref[0, rows, H * dk + h * dk:H * dk + (h + 1) * dk]
            v = v_ref[0, rows, vc]
            q_t = (q * e_pos[:, kc]).astype(BF16)
            k_t = (k * e_neg[:, kc]).astype(BF16)
            k_end = (k * e_end[:, kc]).astype(BF16)
            att = lax.dot_general(q_t, k_t, (((1,), (1,)), ((), ())), preferred_element_type=F32)
            att = jnp.where(keep, att, 0.0).astype(BF16)
            st = st_ref[h]
            o = jnp.dot(att, v, preferred_element_type=F32) + lax.dot_general(
                q_t, st.astype(BF16), (((1,), (1,)), ((), ())), preferred_element_type=F32)
            st_ref[h] = st * dec[:, kc] + lax.dot_general(
                v, k_end, (((0,), (0,)), ((), ())), preferred_element_type=F32)
            if prev_ref is not None:
                o = o + prev_ref[0, rows, vc]
            o_ref[0, rows, vc] = o

    @pl.when(i == pl.num_programs(1) - 1)
    def _():
        sfin_ref[0] = st_ref[...]


def gla_scan(qk, v, gates, s0, direction, prev=None):
    B, L, _ = qk.shape
    reverse = direction == 1
    T = min(512, L)
    nT = L // T
    H, dk, dv = GLA_HEADS, GLA_HK, GLA_HV
    blk = (lambda i: nT - 1 - i) if reverse else (lambda i: i)
    tok = lambda n, col=0: pl.BlockSpec((1, T, n), lambda b, i: (b, blk(i), col))
    st_spec = pl.BlockSpec((1, H, dv, dk), lambda b, i: (b, 0, 0, 0))
    in_specs = [tok(2 * GLA_DK), tok(GLA_DV), tok(GLA_DK, direction), st_spec]
    args = [qk, v, gates, s0]
    if prev is not None:
        in_specs.append(tok(GLA_DV))
        args.append(prev)
    kern = functools.partial(_gla_scan_kernel, reverse=reverse, add_prev=prev is not None)
    return pl.pallas_call(
        kern,
        grid=(B, nT),
        in_specs=in_specs,
        out_specs=[tok(GLA_DV), st_spec],
        out_shape=[jax.ShapeDtypeStruct((B, L, GLA_DV), F32), jax.ShapeDtypeStruct((B, H, dv, dk), F32)],
        scratch_shapes=[pltpu.VMEM((H, dv, dk), F32)],
        compiler_params=pltpu.CompilerParams(
            dimension_semantics=("arbitrary", "arbitrary"), vmem_limit_bytes=VMEM_LIMIT_BYTES),
        name="gla_scan_bwd" if reverse else "gla_scan_fwd",
    )(*args)


def gla_layer(x_lat, x_ctx, mod_lat, mod_ctx, norm1_g, w_in, gk_w2, gk_b, onorm, wo, tail_lat, tail_ctx):
    B = x_lat.shape[0]
    qk_l, v_l, og_l, g_l = gla_in_proj(x_lat, norm1_g, mod_lat, w_in, gk_w2, gk_b)
    qk_c, v_c, og_c, g_c = gla_in_proj(x_ctx, norm1_g, mod_ctx, w_in, gk_w2, gk_b)
    s0 = jnp.zeros((B, GLA_HEADS, GLA_HV, GLA_HK), F32)
    oc, s_f = gla_scan(qk_c, v_c, g_c, s0, 0)
    oc, s_b = gla_scan(qk_c, v_c, g_c, s0, 1, prev=oc)
    ol, _ = gla_scan(qk_l, v_l, g_l, s_f, 0)
    ol, _ = gla_scan(qk_l, v_l, g_l, s_b, 1, prev=ol)
    wo = wo.astype(BF16)
    x_lat = layer_tail("gla", (ol, og_l, onorm, wo), x_lat, mod_lat, *tail_lat)
    if tail_ctx is not None:
        x_ctx = layer_tail("gla", (oc, og_c, onorm, wo), x_ctx, mod_ctx, *tail_ctx)
    return x_lat, x_ctx


def _rope_swap_perm():
    half = MLA_ROPE // 2
    quarter = half // 2
    p = []
    for base in (0, half):
        p += list(range(base + quarter, base + half)) + list(range(base, base + quarter))
    return np.asarray(p)


def mla_rope_tables(L, rotate):
    half = MLA_ROPE // 2
    zeros = jnp.zeros((L, MLA_ROPE), F32)
    if not rotate:
        return jnp.concatenate([jnp.ones((L, MLA_ROPE), F32), zeros], axis=1), jnp.zeros((L, 2 * MLA_ROPE), F32)
    pos = jnp.arange(L)
    inv_freq = ROPE_THETA ** (-jnp.arange(0, half, 2, dtype=F32) / half)
    ang_row = (pos // GRID_W).astype(F32)[:, None] * inv_freq[None, :]
    ang_col = (pos % GRID_W).astype(F32)[:, None] * inv_freq[None, :]
    cr, sr, cc, sc = jnp.cos(ang_row), jnp.sin(ang_row), jnp.cos(ang_col), jnp.sin(ang_col)
    cos = jnp.concatenate([cr, cr, cc, cc, zeros], axis=1)
    sin = jnp.concatenate([-sr, sr, -sc, sc, zeros], axis=1)
    return cos, sin


def _mla_qkv_kernel(x_ref, g_ref, mod_ref, wd_ref, qn_ref, wq_ref, kn_ref, wkv_ref, cos_ref, sin_ref,
                    q_ref, k_ref, v_ref):
    h = _norm_mod(x_ref[0], g_ref[...], mod_ref[0, 0:1, :], mod_ref[0, 1:2, :]).astype(BF16)
    c = jnp.dot(h, wd_ref[...], preferred_element_type=F32)
    cos, sin = cos_ref[...], sin_ref[...]
    lanes = cos.shape[1]

    def rms(a, g):
        return (a * lax.rsqrt(jnp.mean(a * a, axis=-1, keepdims=True) + NORM_EPS) * g).astype(BF16)

    def rope(tile):
        return tile * cos + pltpu.roll(tile, lanes // 2, axis=1) * sin

    cq = rms(c[:, :MLA_Q_RANK], qn_ref[...])
    ckv = rms(c[:, MLA_Q_RANK:MLA_Q_RANK + MLA_KV_RANK], kn_ref[...])
    k_rope = rope(c[:, MLA_Q_RANK + MLA_KV_RANK:]).astype(k_ref.dtype)
    q = jnp.dot(cq, wq_ref[...], preferred_element_type=F32)
    kv = jnp.dot(ckv, wkv_ref[...], preferred_element_type=F32)
    ones = jnp.ones((x_ref.shape[1], MLA_V), v_ref.dtype)
    for hd in range(MLA_HEADS):
        o = hd * MLA_QK_PAD
        q_ref[0, :, o:o + MLA_NOPE] = (q[:, o:o + MLA_NOPE] * MLA_Q_PRESCALE).astype(q_ref.dtype)
        q_ref[0, :, o + MLA_NOPE:o + MLA_QK_PAD] = (
            rope(q[:, o + MLA_NOPE:o + MLA_QK_PAD]) * MLA_Q_PRESCALE).astype(q_ref.dtype)
        k_ref[0, :, o:o + MLA_NOPE] = kv[:, o:o + MLA_NOPE].astype(k_ref.dtype)
        k_ref[0, :, o + MLA_NOPE:o + MLA_QK_PAD] = k_rope
        v_ref[0, :, 2 * hd * MLA_V:(2 * hd + 1) * MLA_V] = kv[:, o + MLA_NOPE:o + MLA_QK_PAD].astype(v_ref.dtype)
        v_ref[0, :, (2 * hd + 1) * MLA_V:(2 * hd + 2) * MLA_V] = ones


def mla_qkv_proj(x, norm_g, mod, w_down, qnorm, w_uq, kvnorm, w_ukv, rotate):
    B, L, D = x.shape
    tm = min(512, L)
    per_batch = mod.shape[0] != 1
    perm = _rope_swap_perm()
    rope0 = MLA_Q_RANK + MLA_KV_RANK
    wd = jnp.concatenate([w_down, w_down[:, rope0:][:, perm]], axis=1).astype(BF16)
    wq = w_uq.reshape(MLA_Q_RANK, MLA_HEADS, MLA_NOPE + MLA_ROPE)
    wq = jnp.concatenate([wq, wq[:, :, MLA_NOPE:][:, :, perm]], axis=2)
    wq = wq.reshape(MLA_Q_RANK, MLA_HEADS * MLA_QK_PAD).astype(BF16)
    cos, sin = mla_rope_tables(L, rotate)
    tok = lambda n: pl.BlockSpec((1, tm, n), lambda b, i: (b, i, 0))
    nq = MLA_HEADS * MLA_QK_PAD
    nv = MLA_HEADS * 2 * MLA_V
    tab = pl.BlockSpec((tm, 2 * MLA_ROPE), lambda b, i: (i, 0))
    return pl.pallas_call(
        _mla_qkv_kernel,
        grid=(B, L // tm),
        in_specs=[tok(D), _const_spec((1, D)),
                  pl.BlockSpec((1, 6, D), (lambda b, i: (b, 0, 0)) if per_batch else (lambda b, i: (0, 0, 0))),
                  _const_spec(wd.shape), _const_spec((1, MLA_Q_RANK)), _const_spec(wq.shape),
                  _const_spec((1, MLA_KV_RANK)), _const_spec(w_ukv.shape), tab, tab],
        out_specs=[tok(nq), tok(nq), tok(nv)],
        out_shape=[jax.ShapeDtypeStruct((B, L, nq), BF16), jax.ShapeDtypeStruct((B, L, nq), BF16),
                   jax.ShapeDtypeStruct((B, L, nv), BF16)],
        compiler_params=pltpu.CompilerParams(
            dimension_semantics=("arbitrary", "arbitrary"), vmem_limit_bytes=VMEM_LIMIT_BYTES),
        name="mla_qkv_proj",
    )(x, norm_g.reshape(1, D), mod, wd, qnorm.reshape(1, -1), wq, kvnorm.reshape(1, -1),
      w_ukv.astype(BF16), cos, sin)


def mla_layer(x_lat, x_ctx, mod_lat, mod_ctx, norm1_g, w_down, qnorm, w_uq, kvnorm, w_ukv, wo, tail):
    ql, kl, vl = mla_qkv_proj(x_lat, norm1_g, mod_lat, w_down, qnorm, w_uq, kvnorm, w_ukv, True)
    _, kc, vc = mla_qkv_proj(x_ctx, norm1_g, mod_ctx, w_down, qnorm, w_uq, kvnorm, w_ukv, False)
    o = mla_attention(ql, kc, vc, kl, vl)
    return layer_tail("mla", (o, wo.astype(BF16)), x_lat, mod_lat, *tail)


def kernel(x, c, ctx, c_ctx, ada_w, ada_b, norm1_g, norm2_g, mlp_w1, mlp_w2, final_g, hy_w_in, hy_b_in, hy_conv_w, hy_conv_b, hy_f_w1, hy_f_b1, hy_f_w2, hy_f_b2, hy_f_w3, hy_f_b3, hy_f_w4, hy_freq, hy_bias, hy_w_out, hy_b_out, gla_w_in, gla_gk_w2, gla_gk_b, gla_onorm, gla_wo, mla_w_down, mla_qnorm, mla_w_uq, mla_kvnorm, mla_w_ukv, mla_wo):
    x_lat = x
    x_ctx = ctx
    silu_c = jax.nn.silu(c)
    silu_cc = jax.nn.silu(c_ctx)
    dft = _dft_constants()
    for i in range(DEPTH):
        kind = i % N_MIXERS
        j = i // N_MIXERS
        ctx_live = any(l % N_MIXERS != 0 for l in range(i + 1, DEPTH))
        mod_lat = (silu_c @ ada_w[i] + ada_b[i]).reshape(-1, 6, D_MODEL)
        mod_ctx = (silu_cc @ ada_w[i] + ada_b[i]).reshape(1, 6, D_MODEL)
        w1 = mlp_w1[i].astype(BF16)
        w2 = mlp_w2[i].astype(BF16)
        tail_lat = (norm2_g[i], w1, w2, final_g if i == DEPTH - 1 else None)
        tail_ctx = (norm2_g[i], w1, w2)
        if kind == 0:
            hp = (hy_w_in[j], hy_b_in[j], hy_conv_w[j], hy_conv_b[j], hy_f_w1[j], hy_f_b1[j],
                  hy_f_w2[j], hy_f_b2[j], hy_f_w3[j], hy_f_b3[j], hy_f_w4[j], hy_freq[j],
                  hy_bias[j], hy_w_out[j], hy_b_out[j])
            x_lat = hyena_layer(x_lat, mod_lat, norm1_g[i], *hp, dft, tail_lat)
            if ctx_live:
                x_ctx = hyena_layer(x_ctx, mod_ctx, norm1_g[i], *hp, dft, tail_ctx)
        elif kind == 1:
            x_lat, x_ctx = gla_layer(x_lat, x_ctx, mod_lat, mod_ctx, norm1_g[i], gla_w_in[j], gla_gk_w2[j],
                                     gla_gk_b[j], gla_onorm[j], gla_wo[j], tail_lat,
                                     tail_ctx if ctx_live else None)
        else:
            assert not ctx_live
            x_lat = mla_layer(x_lat, x_ctx, mod_lat, mod_ctx, norm1_g[i], mla_w_down[j], mla_qnorm[j],
                              mla_w_uq[j], mla_kvnorm[j], mla_w_ukv[j], mla_wo[j], tail_lat)
    return x_lat
```

```python
import functools
import math

import jax
import jax.numpy as jnp
import numpy as np
from jax import lax
from jax.experimental import pallas as pl
from jax.experimental.pallas import tpu as pltpu

F32 = jnp.float32
BF16 = jnp.bfloat16

D_MODEL = 1024
DEPTH = 4
GRID_W = 64
N_MIXERS = 3
NORM_EPS = 1e-6

HY_ORDER = 2
HY_EMB = 33
HY_SHORT = 3
HY_FAST_DECAY = 0.3
HY_SLOW_DECAY = 1.5
HY_TARGET = 1e-2

GLA_HEADS = 4
GLA_DK = D_MODEL // 2
GLA_DV = D_MODEL
GLA_HK = GLA_DK // GLA_HEADS
GLA_HV = GLA_DV // GLA_HEADS
GLA_GATE_RANK = 16
GLA_GATE_NORM = 16.0
GLA_CHUNK = 64

MLA_HEADS = 8
MLA_Q_RANK = 384
MLA_KV_RANK = 256
MLA_NOPE = 128
MLA_ROPE = 64
MLA_V = 128
ROPE_THETA = 10000.0
Q_BLOCK = 128

VMEM_LIMIT_BYTES = 56 * 1024 * 1024
MLA_QK_PAD = 256
SUBLANES = 8
MLA_Q_PRESCALE = (MLA_NOPE + MLA_ROPE) ** -0.5 * math.log2(math.e)


def _const_spec(shape):
    nd = len(shape)
    return pl.BlockSpec(shape, lambda *_: (0,) * nd, pipeline_mode=pl.Buffered(1))


def _norm_mod(x, g, shift, scale):
    y = x * lax.rsqrt(jnp.mean(x * x, axis=-1, keepdims=True) + NORM_EPS)
    return (y * g) * (1.0 + scale) + shift


def _layer_tail_kernel(*refs, kind, hidden_chunk, final_norm):
    if kind == "hyena":
        y_ref, wo_ref, bo_ref = refs[:3]
        rest = refs[3:]
        y = lax.dot_general(y_ref[0], wo_ref[...], (((0,), (0,)), ((), ())),
                            preferred_element_type=F32) + bo_ref[...]
    elif kind == "gla":
        o_ref_in, og_ref, on_ref, wo_ref = refs[:4]
        rest = refs[4:]
        og = og_ref[0]
        parts = []
        for h in range(GLA_HEADS):
            cols = slice(h * GLA_HV, (h + 1) * GLA_HV)
            o = o_ref_in[0, :, cols]
            n = o * lax.rsqrt(jnp.mean(o * o, axis=-1, keepdims=True) + NORM_EPS) * on_ref[...]
            parts.append((n * (og[:, cols] * jax.nn.sigmoid(og[:, cols]))).astype(BF16))
        y = jnp.dot(jnp.concatenate(parts, axis=1), wo_ref[...], preferred_element_type=F32)
    else:
        a_ref, wo_ref = refs[:2]
        rest = refs[2:]
        y = jnp.dot(a_ref[0], wo_ref[...], preferred_element_type=F32)
    x_ref, g_ref, mod_ref, w1_ref, w2_ref, fg_ref, out_ref = rest
    x = x_ref[0] + mod_ref[0, 2:3, :] * y
    h = _norm_mod(x, g_ref[...], mod_ref[0, 3:4, :], mod_ref[0, 4:5, :]).astype(BF16)
    hidden = w1_ref.shape[1]
    acc = jnp.zeros(x.shape, F32)
    for c0 in range(0, hidden, hidden_chunk):
        a = jnp.dot(h, w1_ref[:, c0:c0 + hidden_chunk], preferred_element_type=F32)
        a = jnp.square(jnp.maximum(a, 0.0)).astype(BF16)
        acc = acc + jnp.dot(a, w2_ref[c0:c0 + hidden_chunk, :], preferred_element_type=F32)
    out = x + mod_ref[0, 5:6, :] * acc
    if final_norm:
        out = (out * lax.rsqrt(jnp.mean(out * out, axis=-1, keepdims=True) + NORM_EPS)) * fg_ref[...]
    out_ref[0] = out


def layer_tail(kind, mixer_args, x, mod, norm_g, w1, w2, final_g=None):
    B, L, D = x.shape
    tm = min(512, L)
    per_batch = mod.shape[0] != 1
    final_norm = final_g is not None
    fg = (final_g if final_norm else norm_g).reshape(1, D)
    tok = lambda n: pl.BlockSpec((1, tm, n), lambda b, i: (b, i, 0))
    if kind == "hyena":
        y_t, w_o, b_o = mixer_args
        m_specs = [pl.BlockSpec((1, D, tm), lambda b, i: (b, 0, i)), _const_spec(w_o.shape), _const_spec((1, D))]
        m_args = [y_t, w_o, b_o.reshape(1, D)]
    elif kind == "gla":
        o, og, onorm, w_o = mixer_args
        m_specs = [tok(D), tok(D), _const_spec((1, GLA_HV)), _const_spec(w_o.shape)]
        m_args = [o, og, onorm.reshape(1, GLA_HV), w_o]
    else:
        a, w_o = mixer_args
        m_specs = [tok(a.shape[2]), _const_spec(w_o.shape)]
        m_args = [a, w_o]
    kern = functools.partial(_layer_tail_kernel, kind=kind, hidden_chunk=1024, final_norm=final_norm)
    return pl.pallas_call(
        kern,
        grid=(B, L // tm),
        in_specs=m_specs + [
            tok(D),
            _const_spec((1, D)),
            pl.BlockSpec((1, 6, D), (lambda b, i: (b, 0, 0)) if per_batch else (lambda b, i: (0, 0, 0))),
            _const_spec(w1.shape),
            _const_spec(w2.shape),
            _const_spec((1, D)),
        ],
        out_specs=tok(D),
        out_shape=jax.ShapeDtypeStruct((B, L, D), F32),
        compiler_params=pltpu.CompilerParams(
            dimension_semantics=("arbitrary", "arbitrary"), vmem_limit_bytes=VMEM_LIMIT_BYTES),
        name="layer_tail_" + kind,
    )(*m_args, x, norm_g.reshape(1, D), mod, w1, w2, fg)


def _attn_kernel(q_ref, kc_ref, vc_ref, k_ref, v_ref, o_ref, m_ref, acc_ref, sa_ref, sb_ref, *, sub):
    q = q_ref[0]
    m_ref[...] = jnp.full(m_ref.shape, -jnp.inf, F32)
    acc_ref[...] = jnp.zeros(acc_ref.shape, F32)
    lanes = m_ref.shape[1]
    n_sub = k_ref.shape[1] // sub

    def scores(k):
        return lax.dot_general(q, k, (((1,), (1,)), ((), ())), preferred_element_type=F32)

    def lat(ref, n):
        return ref[0, pl.ds(pl.multiple_of(n * sub, sub), sub), :]

    def accumulate(s, v):
        m_prev = m_ref[...]
        m_new = jnp.maximum(m_prev, jnp.max(s, axis=-1, keepdims=True))
        alpha = jnp.exp2(m_prev - m_new)
        ps = [jnp.exp2(s[:, t:t + lanes] - m_new) for t in range(0, s.shape[1], lanes)]
        p = jnp.concatenate(ps, axis=1).astype(BF16)
        pv = jnp.dot(p, v, preferred_element_type=F32)
        for t in range(0, acc_ref.shape[1], lanes):
            acc_ref[:, t:t + lanes] = alpha * acc_ref[:, t:t + lanes] + pv[:, t:t + lanes]
        m_ref[...] = m_new

    sa_ref[...] = scores(lat(k_ref, 0))
    accumulate(scores(kc_ref[0]), vc_ref[0])

    def pair(n):
        sb_ref[...] = scores(lat(k_ref, n + 1))
        accumulate(sa_ref[...], lat(v_ref, n))

    def body(j, carry):
        n = 2 * j
        pair(n)
        sa_ref[...] = scores(lat(k_ref, n + 2))
        accumulate(sb_ref[...], lat(v_ref, n + 1))
        return carry

    lax.fori_loop(0, n_sub // 2 - 1, body, 0)
    pair(n_sub - 2)
    accumulate(sb_ref[...], lat(v_ref, n_sub - 1))
    o_ref[0] = (acc_ref[:, 0:lanes] / acc_ref[:, lanes:2 * lanes]).astype(o_ref.dtype)


def mla_attention(q, kc, vc, k, v):
    B, L, _ = q.shape
    C = kc.shape[1]
    H = MLA_HEADS
    tq = min(2048, L)
    sub = min(512, L // 4)
    assert L % (2 * sub) == 0 and L % tq == 0
    kern = functools.partial(_attn_kernel, sub=sub)
    return pl.pallas_call(
        kern,
        grid=(B, H, L // tq),
        in_specs=[
            pl.BlockSpec((1, tq, MLA_QK_PAD), lambda b, h, i: (b, i, h)),
            pl.BlockSpec((1, C, MLA_QK_PAD), lambda b, h, i: (b, 0, h)),
            pl.BlockSpec((1, C, 2 * MLA_V), lambda b, h, i: (b, 0, h)),
            pl.BlockSpec((1, L, MLA_QK_PAD), lambda b, h, i: (b, 0, h)),
            pl.BlockSpec((1, L, 2 * MLA_V), lambda b, h, i: (b, 0, h)),
        ],
        out_specs=pl.BlockSpec((1, tq, MLA_V), lambda b, h, i: (b, i, h)),
        out_shape=jax.ShapeDtypeStruct((B, L, H * MLA_V), BF16),
        scratch_shapes=[pltpu.VMEM((tq, MLA_V), F32), pltpu.VMEM((tq, 2 * MLA_V), F32),
                        pltpu.VMEM((tq, sub), F32), pltpu.VMEM((tq, sub), F32)],
        compiler_params=pltpu.CompilerParams(
            dimension_semantics=("arbitrary", "arbitrary", "arbitrary"), vmem_limit_bytes=VMEM_LIMIT_BYTES),
        name="mla_attention",
    )(q, kc, vc, k, v)


DFT_N1 = 128
HY_CH_BLOCK = 32
HY_HEAD_UNROLL = 8
HY_TAP_ROWS = 64


def _dft_constants():
    n1 = DFT_N1
    n = n1 * n1
    idx = np.arange(n1, dtype=np.float64)
    th = 2.0 * np.pi * np.outer(idx, idx) / n1
    cos1, sin1 = np.cos(th), np.sin(th)
    tw = 2.0 * np.pi * np.outer(idx, idx) / n
    fr, fi = cos1, -sin1
    c = dict(
        e1=np.concatenate([cos1, -sin1], axis=0),
        tr=np.cos(tw), ti=-np.sin(tw),
        g=np.block([[fr, fi], [-fi, fr]]),
        gbar=np.block([[fr, -fi], [fi, fr]]),
        e2=np.concatenate([cos1, -sin1], axis=1) / n,
    )
    return {k: jnp.asarray(v, F32) for k, v in c.items()}


def _hy_in_kernel(x_ref, g_ref, mod_ref, wt_ref, b_ref, o_ref, *, row_chunk):
    h = _norm_mod(x_ref[0], g_ref[...], mod_ref[0, 0:1, :], mod_ref[0, 1:2, :]).astype(BF16)
    for r0 in range(0, wt_ref.shape[0], row_chunk):
        z = lax.dot_general(wt_ref[r0:r0 + row_chunk, :], h, (((1,), (1,)), ((), ())),
                            preferred_element_type=F32)
        o_ref[0, r0:r0 + row_chunk, :] = z + b_ref[r0:r0 + row_chunk, :]


def hyena_in_proj(x, norm_g, mod, w_in_t, b_in):
    B, L, D = x.shape
    n_out = w_in_t.shape[0]
    tm = min(512, L)
    per_batch = mod.shape[0] != 1
    return pl.pallas_call(
        functools.partial(_hy_in_kernel, row_chunk=512),
        grid=(B, L // tm),
        in_specs=[
            pl.BlockSpec((1, tm, D), lambda b, i: (b, i, 0)),
            _const_spec((1, D)),
            pl.BlockSpec((1, 6, D), (lambda b, i: (b, 0, 0)) if per_batch else (lambda b, i: (0, 0, 0))),
            _const_spec(w_in_t.shape),
            _const_spec(b_in.shape),
        ],
        out_specs=pl.BlockSpec((1, n_out, tm), lambda b, i: (b, 0, i)),
        out_shape=jax.ShapeDtypeStruct((B, n_out, L), F32),
        compiler_params=pltpu.CompilerParams(
            dimension_semantics=("arbitrary", "arbitrary"), vmem_limit_bytes=VMEM_LIMIT_BYTES),
        name="hyena_in_proj",
    )(x, norm_g.reshape(1, D), mod, w_in_t, b_in)


def _hy_hidden_kernel(zf_ref, w1_ref, b1_ref, w2_ref, b2_ref, w3_ref, b3_ref, fr_ref, o_ref):
    fr = fr_ref[...]
    h = zf_ref[...].astype(BF16)
    for w_ref, b_ref in ((w1_ref, b1_ref), (w2_ref, b2_ref), (w3_ref, b3_ref)):
        h = jnp.sin(fr * (jnp.dot(w_ref[...], h, preferred_element_type=F32) + b_ref[...]))
        out = h
        h = h.astype(BF16)
    o_ref[...] = out


def _hy_taps_kernel(hid_ref, t_ref, w4_ref, dl_ref, o_ref, *, half):
    hid = hid_ref[...].astype(BF16)
    tf = jnp.dot(w4_ref[0, 0].astype(BF16), hid[:, :half], preferred_element_type=F32)
    tb = jnp.dot(w4_ref[0, 1].astype(BF16), hid[:, half:], preferred_element_type=F32)
    taps = jnp.concatenate([tf, tb], axis=1) * jnp.exp(-t_ref[...] * dl_ref[...])
    pos = lax.broadcasted_iota(jnp.int32, taps.shape, 1)
    taps = jnp.where(pos == half, 0.0, taps)
    o_ref[0] = (taps / jnp.sum(jnp.abs(taps), axis=1, keepdims=True)).astype(o_ref.dtype)


def hyena_filter_taps(L, f_w1, f_b1, f_w2, f_b2, f_w3, f_b3, f_w4, freq):
    width = f_w1.shape[1]
    n = 2 * L
    pos = np.arange(n)
    pos = np.where(pos <= L, np.minimum(pos, L - 1), n - pos).astype(np.float64)
    bands = (HY_EMB - 1) // 2
    t = jnp.asarray(pos / (L - 1), F32)[None, :]
    w = 2.0 * math.pi * jnp.asarray(pos, F32)[None, :] / L
    f = jnp.linspace(1e-4, bands - 1, bands, dtype=F32)[:, None]
    zf = jnp.concatenate([t, jnp.cos(f * w), -jnp.sin(f * w)], axis=0)
    zf = jnp.pad(zf, ((0, width - HY_EMB), (0, 0)))
    w1t = jnp.pad(f_w1.T, ((0, 0), (0, width - HY_EMB))).astype(BF16)
    col = lambda v: v.reshape(width, 1).astype(F32)
    lane_blk = min(2048, n)
    hidden = pl.pallas_call(
        _hy_hidden_kernel,
        grid=(n // lane_blk,),
        in_specs=[pl.BlockSpec((width, lane_blk), lambda i: (0, i))] + [_const_spec((width, width)), _const_spec((width, 1))] * 3
        + [_const_spec((width, 1))],
        out_specs=pl.BlockSpec((width, lane_blk), lambda i: (0, i)),
        out_shape=jax.ShapeDtypeStruct((width, n), F32),
        name="hyena_filter_hidden",
    )(zf, w1t, col(f_b1), f_w2.T.astype(BF16), col(f_b2), f_w3.T.astype(BF16), col(f_b3), col(freq))
    max_decay = math.log(HY_TARGET) / HY_FAST_DECAY
    min_decay = math.log(HY_TARGET) / HY_SLOW_DECAY
    deltas = jnp.abs(jnp.linspace(min_decay, max_decay, D_MODEL, dtype=F32)).reshape(D_MODEL, 1)
    w4t = f_w4.T.reshape(HY_ORDER, 2, D_MODEL, width)
    rows = HY_TAP_ROWS
    return pl.pallas_call(
        functools.partial(_hy_taps_kernel, half=L),
        grid=(HY_ORDER, D_MODEL // rows),
        in_specs=[
            _const_spec((width, n)),
            _const_spec((1, n)),
            pl.BlockSpec((1, 2, rows, width), lambda o, i: (o, 0, i, 0)),
            pl.BlockSpec((rows, 1), lambda o, i: (i, 0)),
        ],
        out_specs=pl.BlockSpec((1, rows, n), lambda o, i: (o, i, 0)),
        out_shape=jax.ShapeDtypeStruct((HY_ORDER, D_MODEL, n), BF16),
        compiler_params=pltpu.CompilerParams(
            dimension_semantics=("arbitrary", "arbitrary"), vmem_limit_bytes=VMEM_LIMIT_BYTES),
        name="hyena_filter_taps",
    )(hidden, t, w4t, deltas)


def _dft_stage1(x, c, e1, tr, ti, zb_ref):
    n1 = DFT_N1
    z = jnp.dot(e1, x, preferred_element_type=F32)
    zr, zi = z[:n1], z[n1:]
    r0 = pl.multiple_of(c * n1, n1)
    zb_ref[pl.ds(r0, n1), 0:n1] = (zr * tr - zi * ti).astype(BF16)
    zb_ref[pl.ds(r0, n1), n1:2 * n1] = (zr * ti + zi * tr).astype(BF16)


def _hy_spectrum_kernel(x_ref, e1_ref, tr_ref, ti_ref, g_ref, o_ref, zb_ref):
    n_ch = x_ref.shape[0]
    e1, tr, ti = e1_ref[...], tr_ref[...], ti_ref[...]

    def stage1(c, carry):
        _dft_stage1(x_ref[c].astype(BF16), c, e1, tr, ti, zb_ref)
        return carry

    lax.fori_loop(0, n_ch, stage1, 0, unroll=8)
    xh = jnp.dot(zb_ref[...], g_ref[...], preferred_element_type=F32)
    o_ref[...] = xh.reshape(o_ref.shape).astype(o_ref.dtype)


def hyena_filter_spectrum(taps, consts):
    R = taps.shape[0]
    n1 = DFT_N1
    C = HY_CH_BLOCK
    x = taps.reshape(R, n1, n1)
    return pl.pallas_call(
        _hy_spectrum_kernel,
        grid=(R // C,),
        in_specs=[pl.BlockSpec((C, n1, n1), lambda i: (i, 0, 0)), _const_spec((2 * n1, n1)),
                  _const_spec((n1, n1)), _const_spec((n1, n1)), _const_spec((2 * n1, 2 * n1))],
        out_specs=pl.BlockSpec((C, n1, 2 * n1), lambda i: (i, 0, 0)),
        out_shape=jax.ShapeDtypeStruct((R, n1, 2 * n1), BF16),
        scratch_shapes=[pltpu.VMEM((C * n1, 2 * n1), BF16)],
        compiler_params=pltpu.CompilerParams(
            dimension_semantics=("arbitrary",), vmem_limit_bytes=VMEM_LIMIT_BYTES),
        name="hyena_filter_spectrum",
    )(x, consts["e1"].astype(BF16), consts["tr"], consts["ti"], consts["g"].astype(BF16))


def _hy_conv_kernel(cw_ref, cb_ref, sk_ref, zy_ref, zg1_ref, zg2_ref, hh_ref, e1_ref, tr_ref, ti_ref,
                    g_ref, gb_ref, e2_ref, o_ref, y_sc, g1_sc, g2_sc, zb_sc, u_sc, pad_sc, *, n_ch, d_model):
    n1 = DFT_N1
    rows = zy_ref.shape[2]
    ch0 = pl.program_id(0) * n_ch
    b_idx = lax.broadcasted_iota(jnp.int32, (rows, n1), 1)
    e1, tr, ti, e2 = e1_ref[...], tr_ref[...], ti_ref[...], e2_ref[...]

    pad = pad_sc.shape[1] - rows
    top = pad // 2
    zero_rows = jnp.zeros((top, n1), F32)

    def short_conv(z_ref, c, slot, col):
        z = z_ref[0, c]
        pad_sc[slot, 0:top, :] = zero_rows
        pad_sc[slot, top + rows:pad + rows, :] = zero_rows
        pad_sc[slot, top:top + rows, :] = z
        up = pad_sc[slot, top - 1:top - 1 + rows, :]
        down = pad_sc[slot, top + 1:top + 1 + rows, :]
        prev = pltpu.roll(jnp.where(b_idx == n1 - 1, up, z), 1, axis=1)
        nxt = pltpu.roll(jnp.where(b_idx == 0, down, z), n1 - 1, axis=1)
        n_col = 3 * d_model
        return cw_ref[col] * prev + cw_ref[n_col + col] * z + cw_ref[2 * n_col + col] * nxt + cb_ref[col]

    gate_scs = (g1_sc, g2_sc)

    def spectral_product(order):
        xh = jnp.dot(zb_sc[...], g_ref[...], preferred_element_type=F32)
        hh = hh_ref[order].reshape(n_ch * n1, 2 * n1).astype(F32)
        xr, xi, hr, hi = xh[:, :n1], xh[:, n1:], hh[:, :n1], hh[:, n1:]
        yh = jnp.concatenate([xr * hr - xi * hi, xr * hi + xi * hr], axis=1).astype(BF16)
        u_sc[...] = jnp.dot(yh, gb_ref[...], preferred_element_type=F32)

    def finish(c, order):
        r0 = pl.multiple_of(c * n1, n1)
        ur, ui = u_sc[pl.ds(r0, n1), 0:n1], u_sc[pl.ds(r0, n1), n1:2 * n1]
        stacked = jnp.concatenate([ur * tr + ui * ti, ui * tr - ur * ti], axis=0).astype(BF16)
        conv = jnp.dot(e2, stacked, preferred_element_type=F32)
        return gate_scs[order][c] * (conv + y_sc[c] * sk_ref[order * d_model + ch0 + c])

    n_slot = pad_sc.shape[0] // (HY_ORDER + 1)

    def head(q, carry):
        for j in range(n_slot):
            c = q * n_slot + j
            s0 = j * (HY_ORDER + 1)
            y = short_conv(zy_ref, c, s0, ch0 + c)
            y_sc[c] = y
            g1_sc[c] = short_conv(zg1_ref, c, s0 + 1, d_model + ch0 + c)
            g2_sc[c] = short_conv(zg2_ref, c, s0 + 2, 2 * d_model + ch0 + c)
            _dft_stage1(y.astype(BF16), c, e1, tr, ti, zb_sc)
        return carry

    lax.fori_loop(0, n_ch // n_slot, head, 0)
    for order in range(HY_ORDER - 1):
        spectral_product(order)

        def middle(c, carry, order=order):
            y_sc[c] = finish(c, order)
            return carry

        def restart(c, carry):
            _dft_stage1(y_sc[c].astype(BF16), c, e1, tr, ti, zb_sc)
            return carry

        lax.fori_loop(0, n_ch, middle, 0, unroll=8)
        lax.fori_loop(0, n_ch, restart, 0, unroll=8)
    spectral_product(HY_ORDER - 1)

    def tail(c, carry):
        o_ref[0, c] = finish(c, HY_ORDER - 1).astype(o_ref.dtype)
        return carry

    lax.fori_loop(0, n_ch, tail, 0, unroll=8)


def hyena_long_conv(z_t, spectrum, conv_w, conv_b, skip, consts):
    B, n_col, L = z_t.shape
    D = n_col // (HY_ORDER + 1)
    n1 = DFT_N1
    rows = L // n1
    assert 2 * rows == n1 and HY_ORDER == 2
    C = HY_CH_BLOCK
    nblk = D // C
    z4 = z_t.reshape(B, n_col, rows, n1)
    smem = pl.BlockSpec(memory_space=pltpu.SMEM)
    zspec = lambda off: pl.BlockSpec((1, C, rows, n1), lambda i, b: (b, i + off * nblk, 0, 0))
    kern = functools.partial(_hy_conv_kernel, n_ch=C, d_model=D)
    out = pl.pallas_call(
        kern,
        grid=(nblk, B),
        in_specs=[smem, smem, smem, zspec(0), zspec(1), zspec(2),
                  pl.BlockSpec((HY_ORDER, C, n1, 2 * n1), lambda i, b: (0, i, 0, 0)),
                  _const_spec((2 * n1, rows)), _const_spec((n1, n1)), _const_spec((n1, n1)),
                  _const_spec((2 * n1, 2 * n1)), _const_spec((2 * n1, 2 * n1)), _const_spec((rows, 2 * n1))],
        out_specs=pl.BlockSpec((1, C, rows, n1), lambda i, b: (b, i, 0, 0)),
        out_shape=jax.ShapeDtypeStruct((B, D, rows, n1), BF16),
        scratch_shapes=[pltpu.VMEM((C, rows, n1), F32), pltpu.VMEM((C, rows, n1), F32), pltpu.VMEM((C, rows, n1), F32),
                        pltpu.VMEM((C * n1, 2 * n1), BF16), pltpu.VMEM((C * n1, 2 * n1), F32),
                        pltpu.VMEM((HY_HEAD_UNROLL * (HY_ORDER + 1), rows + 2 * SUBLANES, n1), F32)],
        compiler_params=pltpu.CompilerParams(
            dimension_semantics=("arbitrary", "arbitrary"), vmem_limit_bytes=VMEM_LIMIT_BYTES),
        name="hyena_long_conv",
    )(conv_w.reshape(-1), conv_b.reshape(-1), skip.reshape(-1), z4, z4, z4, spectrum,
      consts["e1"][:, :rows].astype(BF16), consts["tr"], consts["ti"], consts["g"].astype(BF16),
      consts["gbar"].astype(BF16), consts["e2"][:rows].astype(BF16))
    return out.reshape(B, D, L)


def _hy_short_seq_kernel(zy_ref, zg1_ref, zg2_ref, taps_ref, par_ref, dfull_ref, dinv_ref, o_ref):
    rows, L = zy_ref.shape[1], zy_ref.shape[2]
    lane = lax.broadcasted_iota(jnp.int32, (rows, L), 1)
    par = par_ref[...]
    nfreq = dinv_ref.shape[0] // 2

    def short_conv(z, grp):
        prev = jnp.where(lane == 0, 0.0, pltpu.roll(z, 1, axis=1))
        nxt = jnp.where(lane == L - 1, 0.0, pltpu.roll(z, L - 1, axis=1))
        c = 4 * grp
        return par[:, c:c + 1] * prev + par[:, c + 1:c + 2] * z + par[:, c + 2:c + 3] * nxt + par[:, c + 3:c + 4]

    y = short_conv(zy_ref[0], 0)
    gates = (short_conv(zg1_ref[0], 1), short_conv(zg2_ref[0], 2))
    d_first = dfull_ref[0:L, :]
    for order in range(HY_ORDER):
        hh = jnp.dot(taps_ref[order].astype(BF16), dfull_ref[...], preferred_element_type=F32)
        xh = jnp.dot(y.astype(BF16), d_first, preferred_element_type=F32)
        xr, xi, hr, hi = xh[:, :nfreq], xh[:, nfreq:], hh[:, :nfreq], hh[:, nfreq:]
        yh = jnp.concatenate([xr * hr - xi * hi, xr * hi + xi * hr], axis=1).astype(BF16)
        conv = jnp.dot(yh, dinv_ref[...], preferred_element_type=F32)
        y = gates[order] * (conv + y * par[:, 12 + order:13 + order])
    o_ref[0] = y.astype(o_ref.dtype)


def hyena_short_seq_conv(z_t, taps, conv_w, conv_b, skip):
    B, n_col, L = z_t.shape
    D = n_col // (HY_ORDER + 1)
    n = 2 * L
    idx = np.arange(n, dtype=np.float64)
    ang = 2.0 * np.pi * np.outer(idx, idx) / n
    dfull = jnp.asarray(np.concatenate([np.cos(ang), -np.sin(ang)], axis=1), F32)
    dinv = jnp.asarray(np.concatenate([np.cos(ang), -np.sin(ang)], axis=0)[:, :L] / n, F32)
    cw = conv_w.reshape(HY_SHORT, HY_ORDER + 1, D)
    cb = conv_b.reshape(1, HY_ORDER + 1, D)
    par = jnp.concatenate([cw, cb], axis=0)
    par = jnp.transpose(par, (2, 1, 0)).reshape(D, 4 * (HY_ORDER + 1))
    par = jnp.concatenate([par, skip.T, jnp.zeros((D, 2), F32)], axis=1)
    rows = 256
    nblk = D // rows
    zspec = lambda off: pl.BlockSpec((1, rows, L), lambda i, b: (b, i + off * nblk, 0))
    return pl.pallas_call(
        _hy_short_seq_kernel,
        grid=(nblk, B),
        in_specs=[zspec(0), zspec(1), zspec(2),
                  pl.BlockSpec((HY_ORDER, rows, n), lambda i, b: (0, i, 0)),
                  pl.BlockSpec((rows, par.shape[1]), lambda i, b: (i, 0)),
                  _const_spec(dfull.shape), _const_spec(dinv.shape)],
        out_specs=pl.BlockSpec((1, rows, L), lambda i, b: (b, i, 0)),
        out_shape=jax.ShapeDtypeStruct((B, D, L), BF16),
        compiler_params=pltpu.CompilerParams(
            dimension_semantics=("arbitrary", "arbitrary"), vmem_limit_bytes=VMEM_LIMIT_BYTES),
        name="hyena_short_seq_conv",
    )(z_t, z_t, z_t, taps, par, dfull.astype(BF16), dinv.astype(BF16))


def hyena_layer(x, mod, norm1_g, w_in, b_in, conv_w, conv_b, f_w1, f_b1, f_w2, f_b2, f_w3, f_b3, f_w4,
                freq, bias, w_out, b_out, dft, tail):
    L = x.shape[1]
    taps = hyena_filter_taps(L, f_w1, f_b1, f_w2, f_b2, f_w3, f_b3, f_w4, freq)
    z_t = hyena_in_proj(x, norm1_g, mod, w_in.T.astype(BF16), b_in.reshape(-1, 1))
    if 2 * L == DFT_N1 * DFT_N1:
        spec = hyena_filter_spectrum(taps.reshape(HY_ORDER * D_MODEL, 2 * L), dft)
        spec = spec.reshape(HY_ORDER, D_MODEL, DFT_N1, 2 * DFT_N1)
        y_t = hyena_long_conv(z_t, spec, conv_w, conv_b, bias, dft)
    else:
        y_t = hyena_short_seq_conv(z_t, taps, conv_w, conv_b, bias)
    return layer_tail("hyena", (y_t, w_out.astype(BF16), b_out), x, mod, *tail)


def _gla_in_kernel(x_ref, g_ref, mod_ref, w_ref, wr_ref, w2_ref, gb_ref, qk_ref, v_ref, og_ref, gate_ref):
    h = _norm_mod(x_ref[0], g_ref[...], mod_ref[0, 0:1, :], mod_ref[0, 1:2, :]).astype(BF16)
    n = qk_ref.shape[2]
    qk = jnp.dot(h, w_ref[:, 0:n], preferred_element_type=F32)
    half = n // 2
    qk_ref[0, :, 0:half] = qk[:, 0:half] * (GLA_HK ** -0.5)
    qk_ref[0, :, half:n] = qk[:, half:n]
    v_ref[0] = jnp.dot(h, w_ref[:, n:2 * n], preferred_element_type=F32).astype(v_ref.dtype)
    og_ref[0] = jnp.dot(h, w_ref[:, 2 * n:3 * n], preferred_element_type=F32)
    r = jnp.dot(h, wr_ref[...], preferred_element_type=F32).astype(BF16)
    gk = jnp.dot(r, w2_ref[...], preferred_element_type=F32) + gb_ref[...]
    gate_ref[0] = -(jnp.maximum(-gk, 0.0) + jnp.log1p(jnp.exp(-jnp.abs(gk)))) * (1.0 / GLA_GATE_NORM)


def gla_in_proj(x, norm_g, mod, w_in, gk_w2, gk_b):
    B, L, D = x.shape
    tm = min(512, L)
    per_batch = mod.shape[0] != 1
    n_main = 2 * GLA_DK + 2 * GLA_DV
    w_main = w_in[:, :n_main].astype(BF16)
    lanes = 128
    w_r = jnp.pad(w_in[:, n_main:], ((0, 0), (0, lanes - 2 * GLA_GATE_RANK))).astype(BF16)
    w2 = jnp.zeros((lanes, 2 * GLA_DK), F32)
    w2 = w2.at[:GLA_GATE_RANK, :GLA_DK].set(gk_w2[0]).at[GLA_GATE_RANK:2 * GLA_GATE_RANK, GLA_DK:].set(gk_w2[1])
    tok = lambda n: pl.BlockSpec((1, tm, n), lambda b, i: (b, i, 0))
    n = 2 * GLA_DK
    assert GLA_DV == n
    return pl.pallas_call(
        _gla_in_kernel,
        grid=(B, L // tm),
        in_specs=[tok(D), _const_spec((1, D)),
                  pl.BlockSpec((1, 6, D), (lambda b, i: (b, 0, 0)) if per_batch else (lambda b, i: (0, 0, 0))),
                  _const_spec(w_main.shape), _const_spec(w_r.shape), _const_spec(w2.shape), _const_spec((1, n))],
        out_specs=[tok(n), tok(n), tok(n), tok(n)],
        out_shape=[jax.ShapeDtypeStruct((B, L, n), F32), jax.ShapeDtypeStruct((B, L, n), BF16),
                   jax.ShapeDtypeStruct((B, L, n), F32), jax.ShapeDtypeStruct((B, L, n), F32)],
        compiler_params=pltpu.CompilerParams(
            dimension_semantics=("arbitrary", "arbitrary"), vmem_limit_bytes=VMEM_LIMIT_BYTES),
        name="gla_in_proj",
    )(x, norm_g.reshape(1, D), mod, w_main, w_r, w2.astype(BF16), gk_b.reshape(1, n))


def _gla_scan_kernel(*refs, reverse, add_prev):
    if add_prev:
        qk_ref, v_ref, g_ref, s0_ref, prev_ref, o_ref, sfin_ref, st_ref = refs
    else:
        qk_ref, v_ref, g_ref, s0_ref, o_ref, sfin_ref, st_ref = refs
        prev_ref = None
    i = pl.program_id(1)
    C, H, dk, dv = GLA_CHUNK, GLA_HEADS, GLA_HK, GLA_HV

    @pl.when(i == 0)
    def _():
        st_ref[...] = s0_ref[0]

    r_idx = lax.broadcasted_iota(jnp.int32, (C, C), 0)
    c_idx = lax.broadcasted_iota(jnp.int32, (C, C), 1)
    keep = (r_idx <= c_idx) if reverse else (r_idx >= c_idx)
    tri = keep.astype(F32)
    n_chunks = qk_ref.shape[1] // C
    order = range(n_chunks - 1, -1, -1) if reverse else range(n_chunks)
    for ci in order:
        rows = slice(ci * C, (ci + 1) * C)
        b = jnp.dot(tri, g_ref[0, rows, :], precision=lax.Precision.HIGHEST, preferred_element_type=F32)
        b_last = b[0:1] if reverse else b[C - 1:C]
        e_pos, e_neg, e_end, dec = jnp.exp(b), jnp.exp(-b), jnp.exp(b_last - b), jnp.exp(b_last)
        for h in range(H):
            kc = slice(h * dk, (h + 1) * dk)
            vc = slice(h * dv, (h + 1) * dv)
            q = qk_ref[0, rows, kc]
            k = qk_ref[0, rows, H * dk + h * dk:H * dk + (h + 1) * dk]
            v = v_ref[0, rows, vc]
            q_t = (q * e_pos[:, kc]).astype(BF16)
            k_t = (k * e_neg[:, kc]).astype(BF16)
            k_end = (k * e_end[:, kc]).astype(BF16)
            att = lax.dot_general(q_t, k_t, (((1,), (1,)), ((), ())), preferred_element_type=F32)
            att = jnp.where(keep, att, 0.0).astype(BF16)
            st = st_ref[h]
            o = jnp.dot(att, v, preferred_element_type=F32) + lax.dot_general(
                q_t, st.astype(BF16), (((1,), (1,)), ((), ())), preferred_element_type=F32)
            st_ref[h] = st * dec[:, kc] + lax.dot_general(
                v, k_end, (((0,), (0,)), ((), ())), preferred_element_type=F32)
            if prev_ref is not None:
                o = o + prev_ref[0, rows, vc]
            o_ref[0, rows, vc] = o

    @pl.when(i == pl.num_programs(1) - 1)
    def _():
        sfin_ref[0] = st_ref[...]


def gla_scan(qk, v, gates, s0, direction, prev=None):
    B, L, _ = qk.shape
    reverse = direction == 1
    T = min(512, L)
    nT = L // T
    H, dk, dv = GLA_HEADS, GLA_HK, GLA_HV
    blk = (lambda i: nT - 1 - i) if reverse else (lambda i: i)
    tok = lambda n, col=0: pl.BlockSpec((1, T, n), lambda b, i: (b, blk(i), col))
    st_spec = pl.BlockSpec((1, H, dv, dk), lambda b, i: (b, 0, 0, 0))
    in_specs = [tok(2 * GLA_DK), tok(GLA_DV), tok(GLA_DK, direction), st_spec]
    args = [qk, v, gates, s0]
    if prev is not None:
        in_specs.append(tok(GLA_DV))
        args.append(prev)
    kern = functools.partial(_gla_scan_kernel, reverse=reverse, add_prev=prev is not None)
    return pl.pallas_call(
        kern,
        grid=(B, nT),
        in_specs=in_specs,
        out_specs=[tok(GLA_DV), st_spec],
        out_shape=[jax.ShapeDtypeStruct((B, L, GLA_DV), F32), jax.ShapeDtypeStruct((B, H, dv, dk), F32)],
        scratch_shapes=[pltpu.VMEM((H, dv, dk), F32)],
        compiler_params=pltpu.CompilerParams(
            dimension_semantics=("arbitrary", "arbitrary"), vmem_limit_bytes=VMEM_LIMIT_BYTES),
        name="gla_scan_bwd" if reverse else "gla_scan_fwd",
    )(*args)


def gla_layer(x_lat, x_ctx, mod_lat, mod_ctx, norm1_g, w_in, gk_w2, gk_b, onorm, wo, tail_lat, tail_ctx):
    B = x_lat.shape[0]
    qk_l, v_l, og_l, g_l = gla_in_proj(x_lat, norm1_g, mod_lat, w_in, gk_w2, gk_b)
    qk_c, v_c, og_c, g_c = gla_in_proj(x_ctx, norm1_g, mod_ctx, w_in, gk_w2, gk_b)
    s0 = jnp.zeros((B, GLA_HEADS, GLA_HV, GLA_HK), F32)
    oc, s_f = gla_scan(qk_c, v_c, g_c, s0, 0)
    oc, s_b = gla_scan(qk_c, v_c, g_c, s0, 1, prev=oc)
    ol, _ = gla_scan(qk_l, v_l, g_l, s_f, 0)
    ol, _ = gla_scan(qk_l, v_l, g_l, s_b, 1, prev=ol)
    wo = wo.astype(BF16)
    x_lat = layer_tail("gla", (ol, og_l, onorm, wo), x_lat, mod_lat, *tail_lat)
    if tail_ctx is not None:
        x_ctx = layer_tail("gla", (oc, og_c, onorm, wo), x_ctx, mod_ctx, *tail_ctx)
    return x_lat, x_ctx


def _rope_swap_perm():
    half = MLA_ROPE // 2
    quarter = half // 2
    p = []
    for base in (0, half):
        p += list(range(base + quarter, base + half)) + list(range(base, base + quarter))
    return np.asarray(p)


def mla_rope_tables(L, rotate):
    half = MLA_ROPE // 2
    zeros = jnp.zeros((L, MLA_ROPE), F32)
    if not rotate:
        return jnp.concatenate([jnp.ones((L, MLA_ROPE), F32), zeros], axis=1), jnp.zeros((L, 2 * MLA_ROPE), F32)
    pos = jnp.arange(L)
    inv_freq = ROPE_THETA ** (-jnp.arange(0, half, 2, dtype=F32) / half)
    ang_row = (pos // GRID_W).astype(F32)[:, None] * inv_freq[None, :]
    ang_col = (pos % GRID_W).astype(F32)[:, None] * inv_freq[None, :]
    cr, sr, cc, sc = jnp.cos(ang_row), jnp.sin(ang_row), jnp.cos(ang_col), jnp.sin(ang_col)
    cos = jnp.concatenate([cr, cr, cc, cc, zeros], axis=1)
    sin = jnp.concatenate([-sr, sr, -sc, sc, zeros], axis=1)
    return cos, sin


def _mla_qkv_kernel(x_ref, g_ref, mod_ref, wd_ref, qn_ref, wq_ref, kn_ref, wkv_ref, cos_ref, sin_ref,
                    q_ref, k_ref, v_ref):
    h = _norm_mod(x_ref[0], g_ref[...], mod_ref[0, 0:1, :], mod_ref[0, 1:2, :]).astype(BF16)
    c = jnp.dot(h, wd_ref[...], preferred_element_type=F32)
    cos, sin = cos_ref[...], sin_ref[...]
    lanes = cos.shape[1]

    def rms(a, g):
        return (a * lax.rsqrt(jnp.mean(a * a, axis=-1, keepdims=True) + NORM_EPS) * g).astype(BF16)

    def rope(tile):
        return tile * cos + pltpu.roll(tile, lanes // 2, axis=1) * sin

    cq = rms(c[:, :MLA_Q_RANK], qn_ref[...])
    ckv = rms(c[:, MLA_Q_RANK:MLA_Q_RANK + MLA_KV_RANK], kn_ref[...])
    k_rope = rope(c[:, MLA_Q_RANK + MLA_KV_RANK:]).astype(k_ref.dtype)
    q = jnp.dot(cq, wq_ref[...], preferred_element_type=F32)
    kv = jnp.dot(ckv, wkv_ref[...], preferred_element_type=F32)
    ones = jnp.ones((x_ref.shape[1], MLA_V), v_ref.dtype)
    for hd in range(MLA_HEADS):
        o = hd * MLA_QK_PAD
        q_ref[0, :, o:o + MLA_NOPE] = (q[:, o:o + MLA_NOPE] * MLA_Q_PRESCALE).astype(q_ref.dtype)
        q_ref[0, :, o + MLA_NOPE:o + MLA_QK_PAD] = (
            rope(q[:, o + MLA_NOPE:o + MLA_QK_PAD]) * MLA_Q_PRESCALE).astype(q_ref.dtype)
        k_ref[0, :, o:o + MLA_NOPE] = kv[:, o:o + MLA_NOPE].astype(k_ref.dtype)
        k_ref[0, :, o + MLA_NOPE:o + MLA_QK_PAD] = k_rope
        v_ref[0, :, 2 * hd * MLA_V:(2 * hd + 1) * MLA_V] = kv[:, o + MLA_NOPE:o + MLA_QK_PAD].astype(v_ref.dtype)
        v_ref[0, :, (2 * hd + 1) * MLA_V:(2 * hd + 2) * MLA_V] = ones


def mla_qkv_proj(x, norm_g, mod, w_down, qnorm, w_uq, kvnorm, w_ukv, rotate):
    B, L, D = x.shape
    tm = min(512, L)
    per_batch = mod.shape[0] != 1
    perm = _rope_swap_perm()
    rope0 = MLA_Q_RANK + MLA_KV_RANK
    wd = jnp.concatenate([w_down, w_down[:, rope0:][:, perm]], axis=1).astype(BF16)
    wq = w_uq.reshape(MLA_Q_RANK, MLA_HEADS, MLA_NOPE + MLA_ROPE)
    wq = jnp.concatenate([wq, wq[:, :, MLA_NOPE:][:, :, perm]], axis=2)
    wq = wq.reshape(MLA_Q_RANK, MLA_HEADS * MLA_QK_PAD).astype(BF16)
    cos, sin = mla_rope_tables(L, rotate)
    tok = lambda n: pl.BlockSpec((1, tm, n), lambda b, i: (b, i, 0))
    nq = MLA_HEADS * MLA_QK_PAD
    nv = MLA_HEADS * 2 * MLA_V
    tab = pl.BlockSpec((tm, 2 * MLA_ROPE), lambda b, i: (i, 0))
    return pl.pallas_call(
        _mla_qkv_kernel,
        grid=(B, L // tm),
        in_specs=[tok(D), _const_spec((1, D)),
                  pl.BlockSpec((1, 6, D), (lambda b, i: (b, 0, 0)) if per_batch else (lambda b, i: (0, 0, 0))),
                  _const_spec(wd.shape), _const_spec((1, MLA_Q_RANK)), _const_spec(wq.shape),
                  _const_spec((1, MLA_KV_RANK)), _const_spec(w_ukv.shape), tab, tab],
        out_specs=[tok(nq), tok(nq), tok(nv)],
        out_shape=[jax.ShapeDtypeStruct((B, L, nq), BF16), jax.ShapeDtypeStruct((B, L, nq), BF16),
                   jax.ShapeDtypeStruct((B, L, nv), BF16)],
        compiler_params=pltpu.CompilerParams(
            dimension_semantics=("arbitrary", "arbitrary"), vmem_limit_bytes=VMEM_LIMIT_BYTES),
        name="mla_qkv_proj",
    )(x, norm_g.reshape(1, D), mod, wd, qnorm.reshape(1, -1), wq, kvnorm.reshape(1, -1),
      w_ukv.astype(BF16), cos, sin)


def mla_layer(x_lat, x_ctx, mod_lat, mod_ctx, norm1_g, w_down, qnorm, w_uq, kvnorm, w_ukv, wo, tail):
    ql, kl, vl = mla_qkv_proj(x_lat, norm1_g, mod_lat, w_down, qnorm, w_uq, kvnorm, w_ukv, True)
    _, kc, vc = mla_qkv_proj(x_ctx, norm1_g, mod_ctx, w_down, qnorm, w_uq, kvnorm, w_ukv, False)
    o = mla_attention(ql, kc, vc, kl, vl)
    return layer_tail("mla", (o, wo.astype(BF16)), x_lat, mod_lat, *tail)


def _adaln_kernel(c_ref, w_ref, b_ref, o_ref):
    cond = c_ref[...]
    s = (cond * jax.nn.sigmoid(cond)).astype(BF16)
    o_ref[0] = jnp.dot(s, w_ref[0].astype(BF16), preferred_element_type=F32) + b_ref[0]


def adaln_modulation(cond, ada_w, ada_b):
    R, D = cond.shape
    depth, _, n_out = ada_w.shape
    tn = 1024
    return pl.pallas_call(
        _adaln_kernel,
        grid=(depth, n_out // tn),
        in_specs=[_const_spec((R, D)),
                  pl.BlockSpec((1, D, tn), lambda i, j: (i, 0, j)),
                  pl.BlockSpec((1, 1, tn), lambda i, j: (i, 0, j))],
        out_specs=pl.BlockSpec((1, R, tn), lambda i, j: (i, 0, j)),
        out_shape=jax.ShapeDtypeStruct((depth, R, n_out), F32),
        compiler_params=pltpu.CompilerParams(dimension_semantics=("arbitrary", "arbitrary")),
        name="adaln_modulation",
    )(cond, ada_w, ada_b.reshape(depth, 1, n_out))


def kernel(x, c, ctx, c_ctx, ada_w, ada_b, norm1_g, norm2_g, mlp_w1, mlp_w2, final_g, hy_w_in, hy_b_in, hy_conv_w, hy_conv_b, hy_f_w1, hy_f_b1, hy_f_w2, hy_f_b2, hy_f_w3, hy_f_b3, hy_f_w4, hy_freq, hy_bias, hy_w_out, hy_b_out, gla_w_in, gla_gk_w2, gla_gk_b, gla_onorm, gla_wo, mla_w_down, mla_qnorm, mla_w_uq, mla_kvnorm, mla_w_ukv, mla_wo):
    x_lat = x
    x_ctx = ctx
    B = x.shape[0]
    cond = jnp.concatenate([c, c_ctx[None, :], jnp.zeros((SUBLANES - (B + 1) % SUBLANES, D_MODEL), F32)], axis=0)
    mod_all = adaln_modulation(cond, ada_w, ada_b)
    dft = _dft_constants()
    for i in range(DEPTH):
        kind = i % N_MIXERS
        j = i // N_MIXERS
        ctx_live = any(l % N_MIXERS != 0 for l in range(i + 1, DEPTH))
        mod_lat = mod_all[i, :B].reshape(B, 6, D_MODEL)
        mod_ctx = mod_all[i, B:B + 1].reshape(1, 6, D_MODEL)
        w1 = mlp_w1[i].astype(BF16)
        w2 = mlp_w2[i].astype(BF16)
        tail_lat = (norm2_g[i], w1, w2, final_g if i == DEPTH - 1 else None)
        tail_ctx = (norm2_g[i], w1, w2)
        if kind == 0:
            hp = (hy_w_in[j], hy_b_in[j], hy_conv_w[j], hy_conv_b[j], hy_f_w1[j], hy_f_b1[j],
                  hy_f_w2[j], hy_f_b2[j], hy_f_w3[j], hy_f_b3[j], hy_f_w4[j], hy_freq[j],
                  hy_bias[j], hy_w_out[j], hy_b_out[j])
            x_lat = hyena_layer(x_lat, mod_lat, norm1_g[i], *hp, dft, tail_lat)
            if ctx_live:
                x_ctx = hyena_layer(x_ctx, mod_ctx, norm1_g[i], *hp, dft, tail_ctx)
        elif kind == 1:
            x_lat, x_ctx = gla_layer(x_lat, x_ctx, mod_lat, mod_ctx, norm1_g[i], gla_w_in[j], gla_gk_w2[j],
                                     gla_gk_b[j], gla_onorm[j], gla_wo[j], tail_lat,
                                     tail_ctx if ctx_live else None)
        else:
            assert not ctx_live
            x_lat = mla_layer(x_lat, x_ctx, mod_lat, mod_ctx, norm1_g[i], mla_w_down[j], mla_qnorm[j],
                              mla_w_uq[j], mla_kvnorm[j], mla_w_ukv[j], mla_wo[j], tail_lat)
    return x_lat
```

```python
import functools
import math

import jax
import jax.numpy as jnp
import numpy as np
from jax import lax
from jax.experimental import pallas as pl
from jax.experimental.pallas import tpu as pltpu

F32 = jnp.float32
BF16 = jnp.bfloat16

D_MODEL = 1024
DEPTH = 4
GRID_W = 64
N_MIXERS = 3
NORM_EPS = 1e-6

HY_ORDER = 2
HY_EMB = 33
HY_SHORT = 3
HY_FAST_DECAY = 0.3
HY_SLOW_DECAY = 1.5
HY_TARGET = 1e-2

GLA_HEADS = 4
GLA_DK = D_MODEL // 2
GLA_DV = D_MODEL
GLA_HK = GLA_DK // GLA_HEADS
GLA_HV = GLA_DV // GLA_HEADS
GLA_GATE_RANK = 16
GLA_GATE_NORM = 16.0
GLA_CHUNK = 64

MLA_HEADS = 8
MLA_Q_RANK = 384
MLA_KV_RANK = 256
MLA_NOPE = 128
MLA_ROPE = 64
MLA_V = 128
ROPE_THETA = 10000.0

VMEM_LIMIT_BYTES = 56 * 1024 * 1024
MLA_QK_PAD = 256
SUBLANES = 8
LANES = 128
PROJ_TOKENS = 1024
TAIL_TOKENS = 512
MLA_Q_PRESCALE = (MLA_NOPE + MLA_ROPE) ** -0.5 * math.log2(math.e)


def _const_spec(shape):
    nd = len(shape)
    return pl.BlockSpec(shape, lambda *_: (0,) * nd, pipeline_mode=pl.Buffered(1))


def _norm_mod(x, g, shift, scale):
    y = x * lax.rsqrt(jnp.mean(x * x, axis=-1, keepdims=True) + NORM_EPS)
    return (y * g) * (1.0 + scale) + shift


def _layer_tail_kernel(*refs, kind, hidden_chunk, final_norm):
    if kind == "hyena":
        y_ref, wo_ref, bo_ref = refs[:3]
        rest = refs[3:]
        y = lax.dot_general(y_ref[0], wo_ref[...], (((0,), (0,)), ((), ())),
                            preferred_element_type=F32) + bo_ref[...]
    elif kind == "gla":
        o_ref_in, og_ref, on_ref, wo_ref = refs[:4]
        rest = refs[4:]
        og = og_ref[0]
        parts = []
        for h in range(GLA_HEADS):
            cols = slice(h * GLA_HV, (h + 1) * GLA_HV)
            o = o_ref_in[0, :, cols]
            n = o * lax.rsqrt(jnp.mean(o * o, axis=-1, keepdims=True) + NORM_EPS) * on_ref[...]
            parts.append((n * (og[:, cols] * jax.nn.sigmoid(og[:, cols]))).astype(BF16))
        y = jnp.dot(jnp.concatenate(parts, axis=1), wo_ref[...], preferred_element_type=F32)
    else:
        a_ref, wo_ref = refs[:2]
        rest = refs[2:]
        y = jnp.dot(a_ref[0], wo_ref[...], preferred_element_type=F32)
    x_ref, g_ref, mod_ref, w1_ref, w2_ref, fg_ref, out_ref = rest
    x = x_ref[0] + mod_ref[0, 2:3, :] * y
    h = _norm_mod(x, g_ref[...], mod_ref[0, 3:4, :], mod_ref[0, 4:5, :]).astype(BF16)
    hidden = w1_ref.shape[1]
    acc = jnp.zeros(x.shape, F32)
    for c0 in range(0, hidden, hidden_chunk):
        a = jnp.dot(h, w1_ref[:, c0:c0 + hidden_chunk], preferred_element_type=F32)
        a = jnp.square(jnp.maximum(a, 0.0)).astype(BF16)
        acc = acc + jnp.dot(a, w2_ref[c0:c0 + hidden_chunk, :], preferred_element_type=F32)
    out = x + mod_ref[0, 5:6, :] * acc
    if final_norm:
        out = (out * lax.rsqrt(jnp.mean(out * out, axis=-1, keepdims=True) + NORM_EPS)) * fg_ref[...]
    out_ref[0] = out


def layer_tail(kind, mixer_args, x, mod, norm_g, w1, w2, final_g=None):
    B, L, D = x.shape
    tm = min(TAIL_TOKENS, L)
    per_batch = mod.shape[0] != 1
    final_norm = final_g is not None
    fg = (final_g if final_norm else norm_g).reshape(1, D)
    tok = lambda n: pl.BlockSpec((1, tm, n), lambda b, i: (b, i, 0))
    if kind == "hyena":
        y_t, w_o, b_o = mixer_args
        m_specs = [pl.BlockSpec((1, D, tm), lambda b, i: (b, 0, i)), _const_spec(w_o.shape), _const_spec((1, D))]
        m_args = [y_t, w_o, b_o.reshape(1, D)]
    elif kind == "gla":
        o, og, onorm, w_o = mixer_args
        m_specs = [tok(D), tok(D), _const_spec((1, GLA_HV)), _const_spec(w_o.shape)]
        m_args = [o, og, onorm.reshape(1, GLA_HV), w_o]
    else:
        a, w_o = mixer_args
        m_specs = [tok(a.shape[2]), _const_spec(w_o.shape)]
        m_args = [a, w_o]
    kern = functools.partial(_layer_tail_kernel, kind=kind, hidden_chunk=1024, final_norm=final_norm)
    return pl.pallas_call(
        kern,
        grid=(B, L // tm),
        in_specs=m_specs + [
            tok(D),
            _const_spec((1, D)),
            pl.BlockSpec((1, 6, D), (lambda b, i: (b, 0, 0)) if per_batch else (lambda b, i: (0, 0, 0))),
            _const_spec(w1.shape),
            _const_spec(w2.shape),
            _const_spec((1, D)),
        ],
        out_specs=tok(D),
        out_shape=jax.ShapeDtypeStruct((B, L, D), F32),
        compiler_params=pltpu.CompilerParams(
            dimension_semantics=("arbitrary", "arbitrary"), vmem_limit_bytes=VMEM_LIMIT_BYTES),
        name="layer_tail_" + kind,
    )(*m_args, x, norm_g.reshape(1, D), mod, w1, w2, fg)


def _attn_kernel(q_ref, kc_ref, vc_ref, k_ref, v_ref, o_ref, m_ref, acc_ref, sa_ref, sb_ref, *, sub):
    q = q_ref[0]
    m_ref[...] = jnp.full(m_ref.shape, -jnp.inf, F32)
    acc_ref[...] = jnp.zeros(acc_ref.shape, F32)
    lanes = m_ref.shape[1]
    n_sub = k_ref.shape[1] // sub

    def scores(k):
        return lax.dot_general(q, k, (((1,), (1,)), ((), ())), preferred_element_type=F32)

    def lat(ref, n):
        return ref[0, pl.ds(pl.multiple_of(n * sub, sub), sub), :]

    def accumulate(s, v):
        m_prev = m_ref[...]
        m_new = jnp.maximum(m_prev, jnp.max(s, axis=-1, keepdims=True))
        alpha = jnp.exp2(m_prev - m_new)
        ps = [jnp.exp2(s[:, t:t + lanes] - m_new) for t in range(0, s.shape[1], lanes)]
        p = jnp.concatenate(ps, axis=1).astype(BF16)
        pv = jnp.dot(p, v, preferred_element_type=F32)
        for t in range(0, acc_ref.shape[1], lanes):
            acc_ref[:, t:t + lanes] = alpha * acc_ref[:, t:t + lanes] + pv[:, t:t + lanes]
        m_ref[...] = m_new

    sa_ref[...] = scores(lat(k_ref, 0))
    accumulate(scores(kc_ref[0]), vc_ref[0])

    def pair(n):
        sb_ref[...] = scores(lat(k_ref, n + 1))
        accumulate(sa_ref[...], lat(v_ref, n))

    def body(j, carry):
        n = 2 * j
        pair(n)
        sa_ref[...] = scores(lat(k_ref, n + 2))
        accumulate(sb_ref[...], lat(v_ref, n + 1))
        return carry

    lax.fori_loop(0, n_sub // 2 - 1, body, 0)
    pair(n_sub - 2)
    accumulate(sb_ref[...], lat(v_ref, n_sub - 1))
    o_ref[0] = (acc_ref[:, 0:lanes] / acc_ref[:, lanes:2 * lanes]).astype(o_ref.dtype)


def mla_attention(q, kc, vc, k, v):
    B, L, _ = q.shape
    C = kc.shape[1]
    H = MLA_HEADS
    tq = min(2048, L)
    sub = min(512, L // 4)
    assert L % (2 * sub) == 0 and L % tq == 0
    kern = functools.partial(_attn_kernel, sub=sub)
    return pl.pallas_call(
        kern,
        grid=(B, H, L // tq),
        in_specs=[
            pl.BlockSpec((1, tq, MLA_QK_PAD), lambda b, h, i: (b, i, h)),
            pl.BlockSpec((1, C, MLA_QK_PAD), lambda b, h, i: (b, 0, h)),
            pl.BlockSpec((1, C, 2 * MLA_V), lambda b, h, i: (b, 0, h)),
            pl.BlockSpec((1, L, MLA_QK_PAD), lambda b, h, i: (b, 0, h)),
            pl.BlockSpec((1, L, 2 * MLA_V), lambda b, h, i: (b, 0, h)),
        ],
        out_specs=pl.BlockSpec((1, tq, MLA_V), lambda b, h, i: (b, i, h)),
        out_shape=jax.ShapeDtypeStruct((B, L, H * MLA_V), BF16),
        scratch_shapes=[pltpu.VMEM((tq, MLA_V), F32), pltpu.VMEM((tq, 2 * MLA_V), F32),
                        pltpu.VMEM((tq, sub), F32), pltpu.VMEM((tq, sub), F32)],
        compiler_params=pltpu.CompilerParams(
            dimension_semantics=("arbitrary", "arbitrary", "arbitrary"), vmem_limit_bytes=VMEM_LIMIT_BYTES),
        name="mla_attention",
    )(q, kc, vc, k, v)


DFT_N1 = 128
HY_CH_BLOCK = 32
HY_HEAD_UNROLL = 8
HY_TAP_ROWS = 64


def _dft_constants():
    n1 = DFT_N1
    n = n1 * n1
    idx = np.arange(n1, dtype=np.float64)
    th = 2.0 * np.pi * np.outer(idx, idx) / n1
    cos1, sin1 = np.cos(th), np.sin(th)
    tw = 2.0 * np.pi * np.outer(idx, idx) / n
    fr, fi = cos1, -sin1
    c = dict(
        e1=np.concatenate([cos1, -sin1], axis=0),
        tr=np.cos(tw), ti=-np.sin(tw),
        g=np.block([[fr, fi], [-fi, fr]]),
        gbar=np.block([[fr, -fi], [fi, fr]]),
        e2=np.concatenate([cos1, -sin1], axis=1) / n,
    )
    return {k: jnp.asarray(v, F32) for k, v in c.items()}


def _hy_in_kernel(x_ref, g_ref, mod_ref, wt_ref, b_ref, o_ref, *, row_chunk):
    h = _norm_mod(x_ref[0], g_ref[...], mod_ref[0, 0:1, :], mod_ref[0, 1:2, :]).astype(BF16)
    for r0 in range(0, wt_ref.shape[0], row_chunk):
        z = lax.dot_general(wt_ref[r0:r0 + row_chunk, :], h, (((1,), (1,)), ((), ())),
                            preferred_element_type=F32)
        o_ref[0, r0:r0 + row_chunk, :] = z + b_ref[r0:r0 + row_chunk, :]


def hyena_in_proj(x, norm_g, mod, w_in_t, b_in):
    B, L, D = x.shape
    n_out = w_in_t.shape[0]
    tm = min(PROJ_TOKENS, L)
    per_batch = mod.shape[0] != 1
    return pl.pallas_call(
        functools.partial(_hy_in_kernel, row_chunk=512),
        grid=(B, L // tm),
        in_specs=[
            pl.BlockSpec((1, tm, D), lambda b, i: (b, i, 0)),
            _const_spec((1, D)),
            pl.BlockSpec((1, 6, D), (lambda b, i: (b, 0, 0)) if per_batch else (lambda b, i: (0, 0, 0))),
            _const_spec(w_in_t.shape),
            _const_spec(b_in.shape),
        ],
        out_specs=pl.BlockSpec((1, n_out, tm), lambda b, i: (b, 0, i)),
        out_shape=jax.ShapeDtypeStruct((B, n_out, L), F32),
        compiler_params=pltpu.CompilerParams(
            dimension_semantics=("arbitrary", "arbitrary"), vmem_limit_bytes=VMEM_LIMIT_BYTES),
        name="hyena_in_proj",
    )(x, norm_g.reshape(1, D), mod, w_in_t, b_in)


def _hy_hidden_kernel(zf_ref, w1_ref, b1_ref, w2_ref, b2_ref, w3_ref, b3_ref, fr_ref, o_ref):
    fr = fr_ref[...]
    h = zf_ref[...].astype(BF16)
    for w_ref, b_ref in ((w1_ref, b1_ref), (w2_ref, b2_ref), (w3_ref, b3_ref)):
        h = jnp.sin(fr * (jnp.dot(w_ref[...], h, preferred_element_type=F32) + b_ref[...]))
        out = h
        h = h.astype(BF16)
    o_ref[...] = out


def _hy_taps_kernel(hid_ref, t_ref, w4_ref, dl_ref, o_ref, *, half):
    hid = hid_ref[...].astype(BF16)
    tf = jnp.dot(w4_ref[0, 0].astype(BF16), hid[:, :half], preferred_element_type=F32)
    tb = jnp.dot(w4_ref[0, 1].astype(BF16), hid[:, half:], preferred_element_type=F32)
    taps = jnp.concatenate([tf, tb], axis=1) * jnp.exp(-t_ref[...] * dl_ref[...])
    pos = lax.broadcasted_iota(jnp.int32, taps.shape, 1)
    taps = jnp.where(pos == half, 0.0, taps)
    o_ref[0] = (taps / jnp.sum(jnp.abs(taps), axis=1, keepdims=True)).astype(o_ref.dtype)


def hyena_filter_taps(L, f_w1, f_b1, f_w2, f_b2, f_w3, f_b3, f_w4, freq):
    width = f_w1.shape[1]
    n = 2 * L
    pos = np.arange(n)
    pos = np.where(pos <= L, np.minimum(pos, L - 1), n - pos).astype(np.float64)
    bands = (HY_EMB - 1) // 2
    t_np = (pos / (L - 1))[None, :]
    w_np = (2.0 * math.pi * pos / L)[None, :]
    f_np = np.linspace(1e-4, bands - 1, bands)[:, None]
    zf_np = np.concatenate([t_np, np.cos(f_np * w_np), -np.sin(f_np * w_np),
                            np.zeros((width - HY_EMB, n))], axis=0)
    t = jnp.asarray(t_np, F32)
    zf = jnp.asarray(zf_np, F32)
    w1t = jnp.pad(f_w1.T, ((0, 0), (0, width - HY_EMB))).astype(BF16)
    col = lambda v: v.reshape(width, 1).astype(F32)
    lane_blk = min(2048, n)
    hidden = pl.pallas_call(
        _hy_hidden_kernel,
        grid=(n // lane_blk,),
        in_specs=[pl.BlockSpec((width, lane_blk), lambda i: (0, i))] + [_const_spec((width, width)), _const_spec((width, 1))] * 3
        + [_const_spec((width, 1))],
        out_specs=pl.BlockSpec((width, lane_blk), lambda i: (0, i)),
        out_shape=jax.ShapeDtypeStruct((width, n), F32),
        name="hyena_filter_hidden",
    )(zf, w1t, col(f_b1), f_w2.T.astype(BF16), col(f_b2), f_w3.T.astype(BF16), col(f_b3), col(freq))
    max_decay = math.log(HY_TARGET) / HY_FAST_DECAY
    min_decay = math.log(HY_TARGET) / HY_SLOW_DECAY
    deltas = jnp.abs(jnp.linspace(min_decay, max_decay, D_MODEL, dtype=F32)).reshape(D_MODEL, 1)
    w4t = f_w4.T.reshape(HY_ORDER, 2, D_MODEL, width)
    rows = HY_TAP_ROWS
    return pl.pallas_call(
        functools.partial(_hy_taps_kernel, half=L),
        grid=(HY_ORDER, D_MODEL // rows),
        in_specs=[
            _const_spec((width, n)),
            _const_spec((1, n)),
            pl.BlockSpec((1, 2, rows, width), lambda o, i: (o, 0, i, 0)),
            pl.BlockSpec((rows, 1), lambda o, i: (i, 0)),
        ],
        out_specs=pl.BlockSpec((1, rows, n), lambda o, i: (o, i, 0)),
        out_shape=jax.ShapeDtypeStruct((HY_ORDER, D_MODEL, n), BF16),
        compiler_params=pltpu.CompilerParams(
            dimension_semantics=("arbitrary", "arbitrary"), vmem_limit_bytes=VMEM_LIMIT_BYTES),
        name="hyena_filter_taps",
    )(hidden, t, w4t, deltas)


def _dft_stage1(x, c, e1, tr, ti, zb_ref):
    n1 = DFT_N1
    z = jnp.dot(e1, x, preferred_element_type=F32)
    zr, zi = z[:n1], z[n1:]
    r0 = pl.multiple_of(c * n1, n1)
    zb_ref[pl.ds(r0, n1), 0:n1] = (zr * tr - zi * ti).astype(BF16)
    zb_ref[pl.ds(r0, n1), n1:2 * n1] = (zr * ti + zi * tr).astype(BF16)


def _hy_spectrum_kernel(x_ref, e1_ref, tr_ref, ti_ref, g_ref, o_ref, zb_ref):
    n_ch = x_ref.shape[0]
    e1, tr, ti = e1_ref[...], tr_ref[...], ti_ref[...]

    def stage1(c, carry):
        _dft_stage1(x_ref[c].astype(BF16), c, e1, tr, ti, zb_ref)
        return carry

    lax.fori_loop(0, n_ch, stage1, 0, unroll=8)
    xh = jnp.dot(zb_ref[...], g_ref[...], preferred_element_type=F32)
    o_ref[...] = xh.reshape(o_ref.shape).astype(o_ref.dtype)


def hyena_filter_spectrum(taps, consts):
    R = taps.shape[0]
    n1 = DFT_N1
    C = HY_CH_BLOCK
    x = taps.reshape(R, n1, n1)
    return pl.pallas_call(
        _hy_spectrum_kernel,
        grid=(R // C,),
        in_specs=[pl.BlockSpec((C, n1, n1), lambda i: (i, 0, 0)), _const_spec((2 * n1, n1)),
                  _const_spec((n1, n1)), _const_spec((n1, n1)), _const_spec((2 * n1, 2 * n1))],
        out_specs=pl.BlockSpec((C, n1, 2 * n1), lambda i: (i, 0, 0)),
        out_shape=jax.ShapeDtypeStruct((R, n1, 2 * n1), BF16),
        scratch_shapes=[pltpu.VMEM((C * n1, 2 * n1), BF16)],
        compiler_params=pltpu.CompilerParams(
            dimension_semantics=("arbitrary",), vmem_limit_bytes=VMEM_LIMIT_BYTES),
        name="hyena_filter_spectrum",
    )(x, consts["e1"].astype(BF16), consts["tr"], consts["ti"], consts["g"].astype(BF16))


def _hy_conv_kernel(cw_ref, cb_ref, sk_ref, zy_ref, zg1_ref, zg2_ref, hh_ref, e1_ref, tr_ref, ti_ref,
                    g_ref, gb_ref, e2_ref, o_ref, y_sc, g1_sc, g2_sc, zb_sc, u_sc, pad_sc, *, n_ch, d_model):
    n1 = DFT_N1
    rows = zy_ref.shape[2]
    ch0 = pl.program_id(0) * n_ch
    b_idx = lax.broadcasted_iota(jnp.int32, (rows, n1), 1)
    e1, tr, ti, e2 = e1_ref[...], tr_ref[...], ti_ref[...], e2_ref[...]

    pad = pad_sc.shape[1] - rows
    top = pad // 2
    zero_rows = jnp.zeros((top, n1), F32)

    def short_conv(z_ref, c, slot, col):
        z = z_ref[0, c]
        pad_sc[slot, 0:top, :] = zero_rows
        pad_sc[slot, top + rows:pad + rows, :] = zero_rows
        pad_sc[slot, top:top + rows, :] = z
        up = pad_sc[slot, top - 1:top - 1 + rows, :]
        down = pad_sc[slot, top + 1:top + 1 + rows, :]
        prev = pltpu.roll(jnp.where(b_idx == n1 - 1, up, z), 1, axis=1)
        nxt = pltpu.roll(jnp.where(b_idx == 0, down, z), n1 - 1, axis=1)
        n_col = 3 * d_model
        return cw_ref[col] * prev + cw_ref[n_col + col] * z + cw_ref[2 * n_col + col] * nxt + cb_ref[col]

    gate_scs = (g1_sc, g2_sc)

    def spectral_product(order):
        xh = jnp.dot(zb_sc[...], g_ref[...], preferred_element_type=F32)
        hh = hh_ref[order].reshape(n_ch * n1, 2 * n1).astype(F32)
        xr, xi, hr, hi = xh[:, :n1], xh[:, n1:], hh[:, :n1], hh[:, n1:]
        yh = jnp.concatenate([xr * hr - xi * hi, xr * hi + xi * hr], axis=1).astype(BF16)
        u_sc[...] = jnp.dot(yh, gb_ref[...], preferred_element_type=F32)

    def finish(c, order):
        r0 = pl.multiple_of(c * n1, n1)
        ur, ui = u_sc[pl.ds(r0, n1), 0:n1], u_sc[pl.ds(r0, n1), n1:2 * n1]
        stacked = jnp.concatenate([ur * tr + ui * ti, ui * tr - ur * ti], axis=0).astype(BF16)
        conv = jnp.dot(e2, stacked, preferred_element_type=F32)
        return gate_scs[order][c] * (conv + y_sc[c] * sk_ref[order * d_model + ch0 + c])

    n_slot = pad_sc.shape[0] // (HY_ORDER + 1)

    def head(q, carry):
        for j in range(n_slot):
            c = q * n_slot + j
            s0 = j * (HY_ORDER + 1)
            y = short_conv(zy_ref, c, s0, ch0 + c)
            y_sc[c] = y
            g1_sc[c] = short_conv(zg1_ref, c, s0 + 1, d_model + ch0 + c)
            g2_sc[c] = short_conv(zg2_ref, c, s0 + 2, 2 * d_model + ch0 + c)
            _dft_stage1(y.astype(BF16), c, e1, tr, ti, zb_sc)
        return carry

    lax.fori_loop(0, n_ch // n_slot, head, 0)
    for order in range(HY_ORDER - 1):
        spectral_product(order)

        def middle(c, carry, order=order):
            y_sc[c] = finish(c, order)
            return carry

        def restart(c, carry):
            _dft_stage1(y_sc[c].astype(BF16), c, e1, tr, ti, zb_sc)
            return carry

        lax.fori_loop(0, n_ch, middle, 0, unroll=8)
        lax.fori_loop(0, n_ch, restart, 0, unroll=8)
    spectral_product(HY_ORDER - 1)

    def tail(c, carry):
        o_ref[0, c] = finish(c, HY_ORDER - 1).astype(o_ref.dtype)
        return carry

    lax.fori_loop(0, n_ch, tail, 0, unroll=8)


def hyena_long_conv(z_t, spectrum, conv_w, conv_b, skip, consts):
    B, n_col, L = z_t.shape
    D = n_col // (HY_ORDER + 1)
    n1 = DFT_N1
    rows = L // n1
    assert 2 * rows == n1 and HY_ORDER == 2
    C = HY_CH_BLOCK
    nblk = D // C
    z4 = z_t.reshape(B, n_col, rows, n1)
    smem = pl.BlockSpec(memory_space=pltpu.SMEM)
    zspec = lambda off: pl.BlockSpec((1, C, rows, n1), lambda i, b: (b, i + off * nblk, 0, 0))
    kern = functools.partial(_hy_conv_kernel, n_ch=C, d_model=D)
    out = pl.pallas_call(
        kern,
        grid=(nblk, B),
        in_specs=[smem, smem, smem, zspec(0), zspec(1), zspec(2),
                  pl.BlockSpec((HY_ORDER, C, n1, 2 * n1), lambda i, b: (0, i, 0, 0)),
                  _const_spec((2 * n1, rows)), _const_spec((n1, n1)), _const_spec((n1, n1)),
                  _const_spec((2 * n1, 2 * n1)), _const_spec((2 * n1, 2 * n1)), _const_spec((rows, 2 * n1))],
        out_specs=pl.BlockSpec((1, C, rows, n1), lambda i, b: (b, i, 0, 0)),
        out_shape=jax.ShapeDtypeStruct((B, D, rows, n1), BF16),
        scratch_shapes=[pltpu.VMEM((C, rows, n1), F32), pltpu.VMEM((C, rows, n1), F32), pltpu.VMEM((C, rows, n1), F32),
                        pltpu.VMEM((C * n1, 2 * n1), BF16), pltpu.VMEM((C * n1, 2 * n1), F32),
                        pltpu.VMEM((HY_HEAD_UNROLL * (HY_ORDER + 1), rows + 2 * SUBLANES, n1), F32)],
        compiler_params=pltpu.CompilerParams(
            dimension_semantics=("arbitrary", "arbitrary"), vmem_limit_bytes=VMEM_LIMIT_BYTES),
        name="hyena_long_conv",
    )(conv_w.reshape(-1), conv_b.reshape(-1), skip.reshape(-1), z4, z4, z4, spectrum,
      consts["e1"][:, :rows].astype(BF16), consts["tr"], consts["ti"], consts["g"].astype(BF16),
      consts["gbar"].astype(BF16), consts["e2"][:rows].astype(BF16))
    return out.reshape(B, D, L)


def _hy_short_seq_kernel(zy_ref, zg1_ref, zg2_ref, taps_ref, par_ref, dfull_ref, dinv_ref, o_ref):
    rows, L = zy_ref.shape[1], zy_ref.shape[2]
    lane = lax.broadcasted_iota(jnp.int32, (rows, L), 1)
    par = par_ref[...]
    nfreq = dinv_ref.shape[0] // 2

    def short_conv(z, grp):
        prev = jnp.where(lane == 0, 0.0, pltpu.roll(z, 1, axis=1))
        nxt = jnp.where(lane == L - 1, 0.0, pltpu.roll(z, L - 1, axis=1))
        c = 4 * grp
        return par[:, c:c + 1] * prev + par[:, c + 1:c + 2] * z + par[:, c + 2:c + 3] * nxt + par[:, c + 3:c + 4]

    y = short_conv(zy_ref[0], 0)
    gates = (short_conv(zg1_ref[0], 1), short_conv(zg2_ref[0], 2))
    d_first = dfull_ref[0:L, :]
    for order in range(HY_ORDER):
        hh = jnp.dot(taps_ref[order].astype(BF16), dfull_ref[...], preferred_element_type=F32)
        xh = jnp.dot(y.astype(BF16), d_first, preferred_element_type=F32)
        xr, xi, hr, hi = xh[:, :nfreq], xh[:, nfreq:], hh[:, :nfreq], hh[:, nfreq:]
        yh = jnp.concatenate([xr * hr - xi * hi, xr * hi + xi * hr], axis=1).astype(BF16)
        conv = jnp.dot(yh, dinv_ref[...], preferred_element_type=F32)
        y = gates[order] * (conv + y * par[:, 12 + order:13 + order])
    o_ref[0] = y.astype(o_ref.dtype)


def hyena_short_seq_conv(z_t, taps, conv_w, conv_b, skip):
    B, n_col, L = z_t.shape
    D = n_col // (HY_ORDER + 1)
    n = 2 * L
    idx = np.arange(n, dtype=np.float64)
    ang = 2.0 * np.pi * np.outer(idx, idx) / n
    dfull = jnp.asarray(np.concatenate([np.cos(ang), -np.sin(ang)], axis=1), F32)
    dinv = jnp.asarray(np.concatenate([np.cos(ang), -np.sin(ang)], axis=0)[:, :L] / n, F32)
    cw = conv_w.reshape(HY_SHORT, HY_ORDER + 1, D)
    cb = conv_b.reshape(1, HY_ORDER + 1, D)
    par = jnp.concatenate([cw, cb], axis=0)
    par = jnp.transpose(par, (2, 1, 0)).reshape(D, 4 * (HY_ORDER + 1))
    par = jnp.concatenate([par, skip.T, jnp.zeros((D, 2), F32)], axis=1)
    rows = 256
    nblk = D // rows
    zspec = lambda off: pl.BlockSpec((1, rows, L), lambda i, b: (b, i + off * nblk, 0))
    return pl.pallas_call(
        _hy_short_seq_kernel,
        grid=(nblk, B),
        in_specs=[zspec(0), zspec(1), zspec(2),
                  pl.BlockSpec((HY_ORDER, rows, n), lambda i, b: (0, i, 0)),
                  pl.BlockSpec((rows, par.shape[1]), lambda i, b: (i, 0)),
                  _const_spec(dfull.shape), _const_spec(dinv.shape)],
        out_specs=pl.BlockSpec((1, rows, L), lambda i, b: (b, i, 0)),
        out_shape=jax.ShapeDtypeStruct((B, D, L), BF16),
        compiler_params=pltpu.CompilerParams(
            dimension_semantics=("arbitrary", "arbitrary"), vmem_limit_bytes=VMEM_LIMIT_BYTES),
        name="hyena_short_seq_conv",
    )(z_t, z_t, z_t, taps, par, dfull.astype(BF16), dinv.astype(BF16))


def hyena_layer(x, mod, norm1_g, w_in, b_in, conv_w, conv_b, f_w1, f_b1, f_w2, f_b2, f_w3, f_b3, f_w4,
                freq, bias, w_out, b_out, dft, tail):
    L = x.shape[1]
    taps = hyena_filter_taps(L, f_w1, f_b1, f_w2, f_b2, f_w3, f_b3, f_w4, freq)
    z_t = hyena_in_proj(x, norm1_g, mod, w_in.T.astype(BF16), b_in.reshape(-1, 1))
    if 2 * L == DFT_N1 * DFT_N1:
        spec = hyena_filter_spectrum(taps.reshape(HY_ORDER * D_MODEL, 2 * L), dft)
        spec = spec.reshape(HY_ORDER, D_MODEL, DFT_N1, 2 * DFT_N1)
        y_t = hyena_long_conv(z_t, spec, conv_w, conv_b, bias, dft)
    else:
        y_t = hyena_short_seq_conv(z_t, taps, conv_w, conv_b, bias)
    return layer_tail("hyena", (y_t, w_out.astype(BF16), b_out), x, mod, *tail)


def _gla_in_kernel(x_ref, g_ref, mod_ref, w_ref, wr_ref, w2_ref, gb_ref, qk_ref, v_ref, og_ref, gate_ref):
    h = _norm_mod(x_ref[0], g_ref[...], mod_ref[0, 0:1, :], mod_ref[0, 1:2, :]).astype(BF16)
    n = qk_ref.shape[2]
    qk = jnp.dot(h, w_ref[:, 0:n], preferred_element_type=F32)
    half = n // 2
    qk_ref[0, :, 0:half] = qk[:, 0:half] * (GLA_HK ** -0.5)
    qk_ref[0, :, half:n] = qk[:, half:n]
    v_ref[0] = jnp.dot(h, w_ref[:, n:2 * n], preferred_element_type=F32).astype(v_ref.dtype)
    og_ref[0] = jnp.dot(h, w_ref[:, 2 * n:3 * n], preferred_element_type=F32)
    r = jnp.dot(h, wr_ref[...], preferred_element_type=F32).astype(BF16)
    gk = jnp.dot(r, w2_ref[...], preferred_element_type=F32) + gb_ref[...]
    gate_ref[0] = -(jnp.maximum(-gk, 0.0) + jnp.log1p(jnp.exp(-jnp.abs(gk)))) * (1.0 / GLA_GATE_NORM)


def gla_in_proj(x, norm_g, mod, w_in, gk_w2, gk_b):
    B, L, D = x.shape
    tm = min(PROJ_TOKENS, L)
    per_batch = mod.shape[0] != 1
    n_main = 2 * GLA_DK + 2 * GLA_DV
    w_main = w_in[:, :n_main].astype(BF16)
    lanes = LANES
    w_r = jnp.pad(w_in[:, n_main:], ((0, 0), (0, lanes - 2 * GLA_GATE_RANK))).astype(BF16)
    w2 = jnp.zeros((lanes, 2 * GLA_DK), F32)
    w2 = w2.at[:GLA_GATE_RANK, :GLA_DK].set(gk_w2[0]).at[GLA_GATE_RANK:2 * GLA_GATE_RANK, GLA_DK:].set(gk_w2[1])
    tok = lambda n: pl.BlockSpec((1, tm, n), lambda b, i: (b, i, 0))
    n = 2 * GLA_DK
    assert GLA_DV == n
    return pl.pallas_call(
        _gla_in_kernel,
        grid=(B, L // tm),
        in_specs=[tok(D), _const_spec((1, D)),
                  pl.BlockSpec((1, 6, D), (lambda b, i: (b, 0, 0)) if per_batch else (lambda b, i: (0, 0, 0))),
                  _const_spec(w_main.shape), _const_spec(w_r.shape), _const_spec(w2.shape), _const_spec((1, n))],
        out_specs=[tok(n), tok(n), tok(n), tok(n)],
        out_shape=[jax.ShapeDtypeStruct((B, L, n), F32), jax.ShapeDtypeStruct((B, L, n), BF16),
                   jax.ShapeDtypeStruct((B, L, n), F32), jax.ShapeDtypeStruct((B, L, n), F32)],
        compiler_params=pltpu.CompilerParams(
            dimension_semantics=("arbitrary", "arbitrary"), vmem_limit_bytes=VMEM_LIMIT_BYTES),
        name="gla_in_proj",
    )(x, norm_g.reshape(1, D), mod, w_main, w_r, w2.astype(BF16), gk_b.reshape(1, n))


def _gla_scan_kernel(*refs, reverse, add_prev):
    if add_prev:
        qk_ref, v_ref, g_ref, s0_ref, prev_ref, o_ref, sfin_ref, st_ref = refs
    else:
        qk_ref, v_ref, g_ref, s0_ref, o_ref, sfin_ref, st_ref = refs
        prev_ref = None
    i = pl.program_id(1)
    C, H, dk, dv = GLA_CHUNK, GLA_HEADS, GLA_HK, GLA_HV

    @pl.when(i == 0)
    def _():
        st_ref[...] = s0_ref[0]

    r_idx = lax.broadcasted_iota(jnp.int32, (C, C), 0)
    c_idx = lax.broadcasted_iota(jnp.int32, (C, C), 1)
    keep = (r_idx <= c_idx) if reverse else (r_idx >= c_idx)
    tri = keep.astype(F32)
    n_chunks = qk_ref.shape[1] // C
    order = range(n_chunks - 1, -1, -1) if reverse else range(n_chunks)
    for ci in order:
        rows = slice(ci * C, (ci + 1) * C)
        b = jnp.dot(tri, g_ref[0, rows, :], precision=lax.Precision.HIGHEST, preferred_element_type=F32)
        b_last = b[0:1] if reverse else b[C - 1:C]
        e_pos, e_neg, e_end, dec = jnp.exp(b), jnp.exp(-b), jnp.exp(b_last - b), jnp.exp(b_last)
        for h in range(H):
            kc = slice(h * dk, (h + 1) * dk)
            vc = slice(h * dv, (h + 1) * dv)
            q = qk_ref[0, rows, kc]
            k = qk_ref[0, rows, H * dk + h * dk:H * dk + (h + 1) * dk]
            v = v_ref[0, rows, vc]
            q_t = (q * e_pos[:, kc]).astype(BF16)
            k_t = (k * e_neg[:, kc]).astype(BF16)
            k_end = (k * e_end[:, kc]).astype(BF16)
            att = lax.dot_general(q_t, k_t, (((1,), (1,)), ((), ())), preferred_element_type=F32)
            att = jnp.where(keep, att, 0.0).astype(BF16)
            st = st_ref[h]
            o = jnp.dot(att, v, preferred_element_type=F32) + lax.dot_general(
                q_t, st.astype(BF16), (((1,), (1,)), ((), ())), preferred_element_type=F32)
            st_ref[h] = st * dec[:, kc] + lax.dot_general(
                v, k_end, (((0,), (0,)), ((), ())), preferred_element_type=F32)
            if prev_ref is not None:
                o = o + prev_ref[0, rows, vc]
            o_ref[0, rows, vc] = o

    @pl.when(i == pl.num_programs(1) - 1)
    def _():
        sfin_ref[0] = st_ref[...]


def gla_scan(qk, v, gates, s0, direction, prev=None):
    B, L, _ = qk.shape
    reverse = direction == 1
    T = min(512, L)
    nT = L // T
    H, dk, dv = GLA_HEADS, GLA_HK, GLA_HV
    blk = (lambda i: nT - 1 - i) if reverse else (lambda i: i)
    tok = lambda n, col=0: pl.BlockSpec((1, T, n), lambda b, i: (b, blk(i), col))
    st_spec = pl.BlockSpec((1, H, dv, dk), lambda b, i: (b, 0, 0, 0))
    in_specs = [tok(2 * GLA_DK), tok(GLA_DV), tok(GLA_DK, direction), st_spec]
    args = [qk, v, gates, s0]
    if prev is not None:
        in_specs.append(tok(GLA_DV))
        args.append(prev)
    kern = functools.partial(_gla_scan_kernel, reverse=reverse, add_prev=prev is not None)
    return pl.pallas_call(
        kern,
        grid=(B, nT),
        in_specs=in_specs,
        out_specs=[tok(GLA_DV), st_spec],
        out_shape=[jax.ShapeDtypeStruct((B, L, GLA_DV), F32), jax.ShapeDtypeStruct((B, H, dv, dk), F32)],
        scratch_shapes=[pltpu.VMEM((H, dv, dk), F32)],
        compiler_params=pltpu.CompilerParams(
            dimension_semantics=("arbitrary", "arbitrary"), vmem_limit_bytes=VMEM_LIMIT_BYTES),
        name="gla_scan_bwd" if reverse else "gla_scan_fwd",
    )(*args)


def gla_layer(x_lat, x_ctx, mod_lat, mod_ctx, norm1_g, w_in, gk_w2, gk_b, onorm, wo, tail_lat, tail_ctx):
    B = x_lat.shape[0]
    qk_l, v_l, og_l, g_l = gla_in_proj(x_lat, norm1_g, mod_lat, w_in, gk_w2, gk_b)
    qk_c, v_c, og_c, g_c = gla_in_proj(x_ctx, norm1_g, mod_ctx, w_in, gk_w2, gk_b)
    s0 = jnp.zeros((B, GLA_HEADS, GLA_HV, GLA_HK), F32)
    oc, s_f = gla_scan(qk_c, v_c, g_c, s0, 0)
    oc, s_b = gla_scan(qk_c, v_c, g_c, s0, 1, prev=oc)
    ol, _ = gla_scan(qk_l, v_l, g_l, s_f, 0)
    ol, _ = gla_scan(qk_l, v_l, g_l, s_b, 1, prev=ol)
    wo = wo.astype(BF16)
    x_lat = layer_tail("gla", (ol, og_l, onorm, wo), x_lat, mod_lat, *tail_lat)
    if tail_ctx is not None:
        x_ctx = layer_tail("gla", (oc, og_c, onorm, wo), x_ctx, mod_ctx, *tail_ctx)
    return x_lat, x_ctx


def _rope_swap_perm():
    half = MLA_ROPE // 2
    quarter = half // 2
    p = []
    for base in (0, half):
        p += list(range(base + quarter, base + half)) + list(range(base, base + quarter))
    return np.asarray(p)


def mla_rope_tables(L, rotate):
    half = MLA_ROPE // 2
    zeros = jnp.zeros((L, MLA_ROPE), F32)
    if not rotate:
        return jnp.concatenate([jnp.ones((L, MLA_ROPE), F32), zeros], axis=1), jnp.zeros((L, 2 * MLA_ROPE), F32)
    pos = jnp.arange(L)
    inv_freq = ROPE_THETA ** (-jnp.arange(0, half, 2, dtype=F32) / half)
    ang_row = (pos // GRID_W).astype(F32)[:, None] * inv_freq[None, :]
    ang_col = (pos % GRID_W).astype(F32)[:, None] * inv_freq[None, :]
    cr, sr, cc, sc = jnp.cos(ang_row), jnp.sin(ang_row), jnp.cos(ang_col), jnp.sin(ang_col)
    cos = jnp.concatenate([cr, cr, cc, cc, zeros], axis=1)
    sin = jnp.concatenate([-sr, sr, -sc, sc, zeros], axis=1)
    return cos, sin


def _mla_qkv_kernel(x_ref, g_ref, mod_ref, wd_ref, qn_ref, wq_ref, kn_ref, wkv_ref, cos_ref, sin_ref,
                    q_ref, k_ref, v_ref):
    h = _norm_mod(x_ref[0], g_ref[...], mod_ref[0, 0:1, :], mod_ref[0, 1:2, :]).astype(BF16)
    c = jnp.dot(h, wd_ref[...], preferred_element_type=F32)
    cos, sin = cos_ref[...], sin_ref[...]
    lanes = cos.shape[1]

    def rms(a, g):
        return (a * lax.rsqrt(jnp.mean(a * a, axis=-1, keepdims=True) + NORM_EPS) * g).astype(BF16)

    def rope(tile):
        return tile * cos + pltpu.roll(tile, lanes // 2, axis=1) * sin

    cq = rms(c[:, :MLA_Q_RANK], qn_ref[...])
    ckv = rms(c[:, MLA_Q_RANK:MLA_Q_RANK + MLA_KV_RANK], kn_ref[...])
    k_rope = rope(c[:, MLA_Q_RANK + MLA_KV_RANK:]).astype(k_ref.dtype)
    q = jnp.dot(cq, wq_ref[...], preferred_element_type=F32)
    kv = jnp.dot(ckv, wkv_ref[...], preferred_element_type=F32)
    ones = jnp.ones((x_ref.shape[1], MLA_V), v_ref.dtype)
    for hd in range(MLA_HEADS):
        o = hd * MLA_QK_PAD
        q_ref[0, :, o:o + MLA_NOPE] = (q[:, o:o + MLA_NOPE] * MLA_Q_PRESCALE).astype(q_ref.dtype)
        q_ref[0, :, o + MLA_NOPE:o + MLA_QK_PAD] = (
            rope(q[:, o + MLA_NOPE:o + MLA_QK_PAD]) * MLA_Q_PRESCALE).astype(q_ref.dtype)
        k_ref[0, :, o:o + MLA_NOPE] = kv[:, o:o + MLA_NOPE].astype(k_ref.dtype)
        k_ref[0, :, o + MLA_NOPE:o + MLA_QK_PAD] = k_rope
        v_ref[0, :, 2 * hd * MLA_V:(2 * hd + 1) * MLA_V] = kv[:, o + MLA_NOPE:o + MLA_QK_PAD].astype(v_ref.dtype)
        v_ref[0, :, (2 * hd + 1) * MLA_V:(2 * hd + 2) * MLA_V] = ones


def mla_qkv_proj(x, norm_g, mod, w_down, qnorm, w_uq, kvnorm, w_ukv, rotate):
    B, L, D = x.shape
    tm = min(PROJ_TOKENS, L)
    per_batch = mod.shape[0] != 1
    perm = _rope_swap_perm()
    rope0 = MLA_Q_RANK + MLA_KV_RANK
    wd = jnp.concatenate([w_down, w_down[:, rope0:][:, perm]], axis=1).astype(BF16)
    wq = w_uq.reshape(MLA_Q_RANK, MLA_HEADS, MLA_NOPE + MLA_ROPE)
    wq = jnp.concatenate([wq, wq[:, :, MLA_NOPE:][:, :, perm]], axis=2)
    wq = wq.reshape(MLA_Q_RANK, MLA_HEADS * MLA_QK_PAD).astype(BF16)
    cos, sin = mla_rope_tables(L, rotate)
    tok = lambda n: pl.BlockSpec((1, tm, n), lambda b, i: (b, i, 0))
    nq = MLA_HEADS * MLA_QK_PAD
    nv = MLA_HEADS * 2 * MLA_V
    tab = pl.BlockSpec((tm, 2 * MLA_ROPE), lambda b, i: (i, 0))
    return pl.pallas_call(
        _mla_qkv_kernel,
        grid=(B, L // tm),
        in_specs=[tok(D), _const_spec((1, D)),
                  pl.BlockSpec((1, 6, D), (lambda b, i: (b, 0, 0)) if per_batch else (lambda b, i: (0, 0, 0))),
                  _const_spec(wd.shape), _const_spec((1, MLA_Q_RANK)), _const_spec(wq.shape),
                  _const_spec((1, MLA_KV_RANK)), _const_spec(w_ukv.shape), tab, tab],
        out_specs=[tok(nq), tok(nq), tok(nv)],
        out_shape=[jax.ShapeDtypeStruct((B, L, nq), BF16), jax.ShapeDtypeStruct((B, L, nq), BF16),
                   jax.ShapeDtypeStruct((B, L, nv), BF16)],
        compiler_params=pltpu.CompilerParams(
            dimension_semantics=("arbitrary", "arbitrary"), vmem_limit_bytes=VMEM_LIMIT_BYTES),
        name="mla_qkv_proj",
    )(x, norm_g.reshape(1, D), mod, wd, qnorm.reshape(1, -1), wq, kvnorm.reshape(1, -1),
      w_ukv.astype(BF16), cos, sin)


def mla_layer(x_lat, x_ctx, mod_lat, mod_ctx, norm1_g, w_down, qnorm, w_uq, kvnorm, w_ukv, wo, tail):
    ql, kl, vl = mla_qkv_proj(x_lat, norm1_g, mod_lat, w_down, qnorm, w_uq, kvnorm, w_ukv, True)
    _, kc, vc = mla_qkv_proj(x_ctx, norm1_g, mod_ctx, w_down, qnorm, w_uq, kvnorm, w_ukv, False)
    o = mla_attention(ql, kc, vc, kl, vl)
    return layer_tail("mla", (o, wo.astype(BF16)), x_lat, mod_lat, *tail)


def _adaln_kernel(c_ref, w_ref, b_ref, o_ref):
    cond = c_ref[...]
    s = (cond * jax.nn.sigmoid(cond)).astype(BF16)
    o_ref[0] = jnp.dot(s, w_ref[0].astype(BF16), preferred_element_type=F32) + b_ref[0]


def adaln_modulation(cond, ada_w, ada_b):
    R, D = cond.shape
    depth, _, n_out = ada_w.shape
    tn = 1024
    return pl.pallas_call(
        _adaln_kernel,
        grid=(depth, n_out // tn),
        in_specs=[_const_spec((R, D)),
                  pl.BlockSpec((1, D, tn), lambda i, j: (i, 0, j)),
                  pl.BlockSpec((1, 1, tn), lambda i, j: (i, 0, j))],
        out_specs=pl.BlockSpec((1, R, tn), lambda i, j: (i, 0, j)),
        out_shape=jax.ShapeDtypeStruct((depth, R, n_out), F32),
        compiler_params=pltpu.CompilerParams(dimension_semantics=("arbitrary", "arbitrary")),
        name="adaln_modulation",
    )(cond, ada_w, ada_b.reshape(depth, 1, n_out))


def kernel(x, c, ctx, c_ctx, ada_w, ada_b, norm1_g, norm2_g, mlp_w1, mlp_w2, final_g, hy_w_in, hy_b_in, hy_conv_w, hy_conv_b, hy_f_w1, hy_f_b1, hy_f_w2, hy_f_b2, hy_f_w3, hy_f_b3, hy_f_w4, hy_freq, hy_bias, hy_w_out, hy_b_out, gla_w_in, gla_gk_w2, gla_gk_b, gla_onorm, gla_wo, mla_w_down, mla_qnorm, mla_w_uq, mla_kvnorm, mla_w_ukv, mla_wo):
    x_lat = x
    x_ctx = ctx
    B = x.shape[0]
    cond = jnp.concatenate([c, c_ctx[None, :], jnp.zeros((SUBLANES - (B + 1) % SUBLANES, D_MODEL), F32)], axis=0)
    mod_all = adaln_modulation(cond, ada_w, ada_b)
    dft = _dft_constants()
    for i in range(DEPTH):
        kind = i % N_MIXERS
        j = i // N_MIXERS
        ctx_live = any(l % N_MIXERS != 0 for l in range(i + 1, DEPTH))
        mod_lat = mod_all[i, :B].reshape(B, 6, D_MODEL)
        mod_ctx = mod_all[i, B:B + 1].reshape(1, 6, D_MODEL)
        w1 = mlp_w1[i].astype(BF16)
        w2 = mlp_w2[i].astype(BF16)
        tail_lat = (norm2_g[i], w1, w2, final_g if i == DEPTH - 1 else None)
        tail_ctx = (norm2_g[i], w1, w2)
        if kind == 0:
            hp = (hy_w_in[j], hy_b_in[j], hy_conv_w[j], hy_conv_b[j], hy_f_w1[j], hy_f_b1[j],
                  hy_f_w2[j], hy_f_b2[j], hy_f_w3[j], hy_f_b3[j], hy_f_w4[j], hy_freq[j],
                  hy_bias[j], hy_w_out[j], hy_b_out[j])
            x_lat = hyena_layer(x_lat, mod_lat, norm1_g[i], *hp, dft, tail_lat)
            if ctx_live:
                x_ctx = hyena_layer(x_ctx, mod_ctx, norm1_g[i], *hp, dft, tail_ctx)
        elif kind == 1:
            x_lat, x_ctx = gla_layer(x_lat, x_ctx, mod_lat, mod_ctx, norm1_g[i], gla_w_in[j], gla_gk_w2[j],
                                     gla_gk_b[j], gla_onorm[j], gla_wo[j], tail_lat,
                                     tail_ctx if ctx_live else None)
        else:
            assert not ctx_live
            x_lat = mla_layer(x_lat, x_ctx, mod_lat, mod_ctx, norm1_g[i], mla_w_down[j], mla_qnorm[j],
                              mla_w_uq[j], mla_kvnorm[j], mla_w_ukv[j], mla_wo[j], tail_lat)
    return x_lat
```

```python
import functools
import math

import jax
import jax.numpy as jnp
import numpy as np
from jax import lax
from jax.experimental import pallas as pl
from jax.experimental.pallas import tpu as pltpu

F32 = jnp.float32
BF16 = jnp.bfloat16

D_MODEL = 1024
DEPTH = 4
GRID_W = 64
N_MIXERS = 3
NORM_EPS = 1e-6

HY_ORDER = 2
HY_EMB = 33
HY_SHORT = 3
HY_FAST_DECAY = 0.3
HY_SLOW_DECAY = 1.5
HY_TARGET = 1e-2

GLA_HEADS = 4
GLA_DK = D_MODEL // 2
GLA_DV = D_MODEL
GLA_HK = GLA_DK // GLA_HEADS
GLA_HV = GLA_DV // GLA_HEADS
GLA_GATE_RANK = 16
GLA_GATE_NORM = 16.0
GLA_CHUNK = 64

MLA_HEADS = 8
MLA_Q_RANK = 384
MLA_KV_RANK = 256
MLA_NOPE = 128
MLA_ROPE = 64
MLA_V = 128
ROPE_THETA = 10000.0

VMEM_LIMIT_BYTES = 56 * 1024 * 1024
MLA_QK_PAD = 256
SUBLANES = 8
LANES = 128
PROJ_TOKENS = 1024
TAIL_TOKENS = 512
MLA_Q_PRESCALE = (MLA_NOPE + MLA_ROPE) ** -0.5 * math.log2(math.e)


def _const_spec(shape):
    nd = len(shape)
    return pl.BlockSpec(shape, lambda *_: (0,) * nd, pipeline_mode=pl.Buffered(1))


def _norm_mod(x, g, shift, scale):
    y = x * lax.rsqrt(jnp.mean(x * x, axis=-1, keepdims=True) + NORM_EPS)
    return (y * g) * (1.0 + scale) + shift


def _layer_tail_kernel(*refs, kind, hidden_chunk, final_norm):
    if kind == "hyena":
        y_ref, wo_ref, bo_ref = refs[:3]
        rest = refs[3:]
        y = lax.dot_general(y_ref[0], wo_ref[...], (((0,), (0,)), ((), ())),
                            preferred_element_type=F32) + bo_ref[...]
    elif kind == "gla":
        o_ref_in, og_ref, on_ref, wo_ref = refs[:4]
        rest = refs[4:]
        og = og_ref[0]
        parts = []
        for h in range(GLA_HEADS):
            cols = slice(h * GLA_HV, (h + 1) * GLA_HV)
            o = o_ref_in[0, :, cols]
            n = o * lax.rsqrt(jnp.mean(o * o, axis=-1, keepdims=True) + NORM_EPS) * on_ref[...]
            parts.append((n * (og[:, cols] * jax.nn.sigmoid(og[:, cols]))).astype(BF16))
        y = jnp.dot(jnp.concatenate(parts, axis=1), wo_ref[...], preferred_element_type=F32)
    else:
        a_ref, wo_ref = refs[:2]
        rest = refs[2:]
        y = jnp.dot(a_ref[0], wo_ref[...], preferred_element_type=F32)
    x_ref, g_ref, mod_ref, w1_ref, w2_ref, fg_ref, out_ref = rest
    x = x_ref[0] + mod_ref[0, 2:3, :] * y
    h = _norm_mod(x, g_ref[...], mod_ref[0, 3:4, :], mod_ref[0, 4:5, :]).astype(BF16)
    hidden = w1_ref.shape[1]
    acc = jnp.zeros(x.shape, F32)
    for c0 in range(0, hidden, hidden_chunk):
        a = jnp.dot(h, w1_ref[:, c0:c0 + hidden_chunk], preferred_element_type=F32)
        a = jnp.square(jnp.maximum(a, 0.0)).astype(BF16)
        acc = acc + jnp.dot(a, w2_ref[c0:c0 + hidden_chunk, :], preferred_element_type=F32)
    out = x + mod_ref[0, 5:6, :] * acc
    if final_norm:
        out = (out * lax.rsqrt(jnp.mean(out * out, axis=-1, keepdims=True) + NORM_EPS)) * fg_ref[...]
    out_ref[0] = out


def layer_tail(kind, mixer_args, x, mod, norm_g, w1, w2, final_g=None):
    B, L, D = x.shape
    tm = min(TAIL_TOKENS, L)
    per_batch = mod.shape[0] != 1
    final_norm = final_g is not None
    fg = (final_g if final_norm else norm_g).reshape(1, D)
    tok = lambda n: pl.BlockSpec((1, tm, n), lambda b, i: (b, i, 0))
    if kind == "hyena":
        y_t, w_o, b_o = mixer_args
        m_specs = [pl.BlockSpec((1, D, tm), lambda b, i: (b, 0, i)), _const_spec(w_o.shape), _const_spec((1, D))]
        m_args = [y_t, w_o, b_o.reshape(1, D)]
    elif kind == "gla":
        o, og, onorm, w_o = mixer_args
        m_specs = [tok(D), tok(D), _const_spec((1, GLA_HV)), _const_spec(w_o.shape)]
        m_args = [o, og, onorm.reshape(1, GLA_HV), w_o]
    else:
        a, w_o = mixer_args
        m_specs = [tok(a.shape[2]), _const_spec(w_o.shape)]
        m_args = [a, w_o]
    kern = functools.partial(_layer_tail_kernel, kind=kind, hidden_chunk=1024, final_norm=final_norm)
    return pl.pallas_call(
        kern,
        grid=(B, L // tm),
        in_specs=m_specs + [
            tok(D),
            _const_spec((1, D)),
            pl.BlockSpec((1, 6, D), (lambda b, i: (b, 0, 0)) if per_batch else (lambda b, i: (0, 0, 0))),
            _const_spec(w1.shape),
            _const_spec(w2.shape),
            _const_spec((1, D)),
        ],
        out_specs=tok(D),
        out_shape=jax.ShapeDtypeStruct((B, L, D), F32),
        compiler_params=pltpu.CompilerParams(
            dimension_semantics=("arbitrary", "arbitrary"), vmem_limit_bytes=VMEM_LIMIT_BYTES),
        name="layer_tail_" + kind,
    )(*m_args, x, norm_g.reshape(1, D), mod, w1, w2, fg)


def _attn_kernel(q_ref, kc_ref, vc_ref, k_ref, v_ref, o_ref, m_ref, acc_ref, sa_ref, sb_ref, *, sub):
    q = q_ref[0]
    m_ref[...] = jnp.full(m_ref.shape, -jnp.inf, F32)
    acc_ref[...] = jnp.zeros(acc_ref.shape, F32)
    lanes = m_ref.shape[1]
    n_sub = k_ref.shape[1] // sub

    def scores(k):
        return lax.dot_general(q, k, (((1,), (1,)), ((), ())), preferred_element_type=F32)

    def lat(ref, n):
        return ref[0, pl.ds(pl.multiple_of(n * sub, sub), sub), :]

    def accumulate(s, v):
        m_prev = m_ref[...]
        m_new = jnp.maximum(m_prev, jnp.max(s, axis=-1, keepdims=True))
        alpha = jnp.exp2(m_prev - m_new)
        ps = [jnp.exp2(s[:, t:t + lanes] - m_new) for t in range(0, s.shape[1], lanes)]
        p = jnp.concatenate(ps, axis=1).astype(BF16)
        pv = jnp.dot(p, v, preferred_element_type=F32)
        for t in range(0, acc_ref.shape[1], lanes):
            acc_ref[:, t:t + lanes] = alpha * acc_ref[:, t:t + lanes] + pv[:, t:t + lanes]
        m_ref[...] = m_new

    sa_ref[...] = scores(lat(k_ref, 0))
    accumulate(scores(kc_ref[0]), vc_ref[0])

    def pair(n):
        sb_ref[...] = scores(lat(k_ref, n + 1))
        accumulate(sa_ref[...], lat(v_ref, n))

    def body(j, carry):
        n = 2 * j
        pair(n)
        sa_ref[...] = scores(lat(k_ref, n + 2))
        accumulate(sb_ref[...], lat(v_ref, n + 1))
        return carry

    lax.fori_loop(0, n_sub // 2 - 1, body, 0)
    pair(n_sub - 2)
    accumulate(sb_ref[...], lat(v_ref, n_sub - 1))
    o_ref[0] = (acc_ref[:, 0:lanes] / acc_ref[:, lanes:2 * lanes]).astype(o_ref.dtype)


def mla_attention(q, kc, vc, k, v):
    B, L, _ = q.shape
    C = kc.shape[1]
    H = MLA_HEADS
    tq = min(2048, L)
    sub = min(512, L // 4)
    assert L % (2 * sub) == 0 and L % tq == 0
    kern = functools.partial(_attn_kernel, sub=sub)
    return pl.pallas_call(
        kern,
        grid=(B, H, L // tq),
        in_specs=[
            pl.BlockSpec((1, tq, MLA_QK_PAD), lambda b, h, i: (b, i, h)),
            pl.BlockSpec((1, C, MLA_QK_PAD), lambda b, h, i: (b, 0, h)),
            pl.BlockSpec((1, C, 2 * MLA_V), lambda b, h, i: (b, 0, h)),
            pl.BlockSpec((1, L, MLA_QK_PAD), lambda b, h, i: (b, 0, h)),
            pl.BlockSpec((1, L, 2 * MLA_V), lambda b, h, i: (b, 0, h)),
        ],
        out_specs=pl.BlockSpec((1, tq, MLA_V), lambda b, h, i: (b, i, h)),
        out_shape=jax.ShapeDtypeStruct((B, L, H * MLA_V), BF16),
        scratch_shapes=[pltpu.VMEM((tq, MLA_V), F32), pltpu.VMEM((tq, 2 * MLA_V), F32),
                        pltpu.VMEM((tq, sub), F32), pltpu.VMEM((tq, sub), F32)],
        compiler_params=pltpu.CompilerParams(
            dimension_semantics=("arbitrary", "arbitrary", "arbitrary"), vmem_limit_bytes=VMEM_LIMIT_BYTES),
        name="mla_attention",
    )(q, kc, vc, k, v)


DFT_N1 = 128
HY_CH_BLOCK = 32
HY_HEAD_UNROLL = 8
HY_TAP_ROWS = 64


def _dft_constants():
    n1 = DFT_N1
    n = n1 * n1
    idx = np.arange(n1, dtype=np.float64)
    th = 2.0 * np.pi * np.outer(idx, idx) / n1
    cos1, sin1 = np.cos(th), np.sin(th)
    tw = 2.0 * np.pi * np.outer(idx, idx) / n
    fr, fi = cos1, -sin1
    c = dict(
        e1=np.concatenate([cos1, -sin1], axis=0),
        tr=np.cos(tw), ti=-np.sin(tw),
        g=np.block([[fr, fi], [-fi, fr]]),
        gbar=np.block([[fr, -fi], [fi, fr]]),
        e2=np.concatenate([cos1, -sin1], axis=1) / n,
    )
    return {k: jnp.asarray(v, F32) for k, v in c.items()}


def _hy_in_kernel(x_ref, g_ref, mod_ref, wt_ref, b_ref, o_ref, *, row_chunk):
    h = _norm_mod(x_ref[0], g_ref[...], mod_ref[0, 0:1, :], mod_ref[0, 1:2, :]).astype(BF16)
    for r0 in range(0, wt_ref.shape[0], row_chunk):
        z = lax.dot_general(wt_ref[r0:r0 + row_chunk, :], h, (((1,), (1,)), ((), ())),
                            preferred_element_type=F32)
        o_ref[0, r0:r0 + row_chunk, :] = z + b_ref[r0:r0 + row_chunk, :]


def hyena_in_proj(x, norm_g, mod, w_in_t, b_in):
    B, L, D = x.shape
    n_out = w_in_t.shape[0]
    tm = min(PROJ_TOKENS, L)
    per_batch = mod.shape[0] != 1
    return pl.pallas_call(
        functools.partial(_hy_in_kernel, row_chunk=512),
        grid=(B, L // tm),
        in_specs=[
            pl.BlockSpec((1, tm, D), lambda b, i: (b, i, 0)),
            _const_spec((1, D)),
            pl.BlockSpec((1, 6, D), (lambda b, i: (b, 0, 0)) if per_batch else (lambda b, i: (0, 0, 0))),
            _const_spec(w_in_t.shape),
            _const_spec(b_in.shape),
        ],
        out_specs=pl.BlockSpec((1, n_out, tm), lambda b, i: (b, 0, i)),
        out_shape=jax.ShapeDtypeStruct((B, n_out, L), F32),
        compiler_params=pltpu.CompilerParams(
            dimension_semantics=("arbitrary", "arbitrary"), vmem_limit_bytes=VMEM_LIMIT_BYTES),
        name="hyena_in_proj",
    )(x, norm_g.reshape(1, D), mod, w_in_t, b_in)


def _hy_hidden_kernel(zf_ref, w1_ref, b1_ref, w2_ref, b2_ref, w3_ref, b3_ref, fr_ref, o_ref):
    fr = fr_ref[...]
    h = zf_ref[...].astype(BF16)
    for w_ref, b_ref in ((w1_ref, b1_ref), (w2_ref, b2_ref), (w3_ref, b3_ref)):
        h = jnp.sin(fr * (jnp.dot(w_ref[...], h, preferred_element_type=F32) + b_ref[...]))
        out = h
        h = h.astype(BF16)
    o_ref[...] = out


def _hy_taps_kernel(hid_ref, t_ref, w4_ref, dl_ref, o_ref, *, half):
    hid = hid_ref[...].astype(BF16)
    tf = jnp.dot(w4_ref[0, 0].astype(BF16), hid[:, :half], preferred_element_type=F32)
    tb = jnp.dot(w4_ref[0, 1].astype(BF16), hid[:, half:], preferred_element_type=F32)
    taps = jnp.concatenate([tf, tb], axis=1) * jnp.exp(-t_ref[...] * dl_ref[...])
    pos = lax.broadcasted_iota(jnp.int32, taps.shape, 1)
    taps = jnp.where(pos == half, 0.0, taps)
    o_ref[0] = (taps / jnp.sum(jnp.abs(taps), axis=1, keepdims=True)).astype(o_ref.dtype)


def hyena_filter_taps(L, f_w1, f_b1, f_w2, f_b2, f_w3, f_b3, f_w4, freq):
    width = f_w1.shape[1]
    n = 2 * L
    pos = np.arange(n)
    pos = np.where(pos <= L, np.minimum(pos, L - 1), n - pos).astype(np.float64)
    bands = (HY_EMB - 1) // 2
    t_np = (pos / (L - 1))[None, :]
    w_np = (2.0 * math.pi * pos / L)[None, :]
    f_np = np.linspace(1e-4, bands - 1, bands)[:, None]
    zf_np = np.concatenate([t_np, np.cos(f_np * w_np), -np.sin(f_np * w_np),
                            np.zeros((width - HY_EMB, n))], axis=0)
    t = jnp.asarray(t_np, F32)
    zf = jnp.asarray(zf_np, F32)
    w1t = jnp.pad(f_w1.T, ((0, 0), (0, width - HY_EMB))).astype(BF16)
    col = lambda v: v.reshape(width, 1).astype(F32)
    lane_blk = min(2048, n)
    hidden = pl.pallas_call(
        _hy_hidden_kernel,
        grid=(n // lane_blk,),
        in_specs=[pl.BlockSpec((width, lane_blk), lambda i: (0, i))] + [_const_spec((width, width)), _const_spec((width, 1))] * 3
        + [_const_spec((width, 1))],
        out_specs=pl.BlockSpec((width, lane_blk), lambda i: (0, i)),
        out_shape=jax.ShapeDtypeStruct((width, n), F32),
        name="hyena_filter_hidden",
    )(zf, w1t, col(f_b1), f_w2.T.astype(BF16), col(f_b2), f_w3.T.astype(BF16), col(f_b3), col(freq))
    max_decay = math.log(HY_TARGET) / HY_FAST_DECAY
    min_decay = math.log(HY_TARGET) / HY_SLOW_DECAY
    deltas = jnp.abs(jnp.linspace(min_decay, max_decay, D_MODEL, dtype=F32)).reshape(D_MODEL, 1)
    w4t = f_w4.T.reshape(HY_ORDER, 2, D_MODEL, width)
    rows = HY_TAP_ROWS
    return pl.pallas_call(
        functools.partial(_hy_taps_kernel, half=L),
        grid=(HY_ORDER, D_MODEL // rows),
        in_specs=[
            _const_spec((width, n)),
            _const_spec((1, n)),
            pl.BlockSpec((1, 2, rows, width), lambda o, i: (o, 0, i, 0)),
            pl.BlockSpec((rows, 1), lambda o, i: (i, 0)),
        ],
        out_specs=pl.BlockSpec((1, rows, n), lambda o, i: (o, i, 0)),
        out_shape=jax.ShapeDtypeStruct((HY_ORDER, D_MODEL, n), BF16),
        compiler_params=pltpu.CompilerParams(
            dimension_semantics=("arbitrary", "arbitrary"), vmem_limit_bytes=VMEM_LIMIT_BYTES),
        name="hyena_filter_taps",
    )(hidden, t, w4t, deltas)


def _dft_stage1(x, c, e1, tr, ti, zb_ref):
    n1 = DFT_N1
    z = jnp.dot(e1, x, preferred_element_type=F32)
    zr, zi = z[:n1], z[n1:]
    r0 = pl.multiple_of(c * n1, n1)
    zb_ref[pl.ds(r0, n1), 0:n1] = (zr * tr - zi * ti).astype(BF16)
    zb_ref[pl.ds(r0, n1), n1:2 * n1] = (zr * ti + zi * tr).astype(BF16)


def _hy_spectrum_kernel(x_ref, e1_ref, tr_ref, ti_ref, g_ref, o_ref, zb_ref):
    n_ch = x_ref.shape[0]
    e1, tr, ti = e1_ref[...], tr_ref[...], ti_ref[...]

    def stage1(c, carry):
        _dft_stage1(x_ref[c].astype(BF16), c, e1, tr, ti, zb_ref)
        return carry

    lax.fori_loop(0, n_ch, stage1, 0, unroll=8)
    xh = jnp.dot(zb_ref[...], g_ref[...], preferred_element_type=F32)
    o_ref[...] = xh.reshape(o_ref.shape).astype(o_ref.dtype)


def hyena_filter_spectrum(taps, consts):
    R = taps.shape[0]
    n1 = DFT_N1
    C = HY_CH_BLOCK
    x = taps.reshape(R, n1, n1)
    return pl.pallas_call(
        _hy_spectrum_kernel,
        grid=(R // C,),
        in_specs=[pl.BlockSpec((C, n1, n1), lambda i: (i, 0, 0)), _const_spec((2 * n1, n1)),
                  _const_spec((n1, n1)), _const_spec((n1, n1)), _const_spec((2 * n1, 2 * n1))],
        out_specs=pl.BlockSpec((C, n1, 2 * n1), lambda i: (i, 0, 0)),
        out_shape=jax.ShapeDtypeStruct((R, n1, 2 * n1), BF16),
        scratch_shapes=[pltpu.VMEM((C * n1, 2 * n1), BF16)],
        compiler_params=pltpu.CompilerParams(
            dimension_semantics=("arbitrary",), vmem_limit_bytes=VMEM_LIMIT_BYTES),
        name="hyena_filter_spectrum",
    )(x, consts["e1"].astype(BF16), consts["tr"], consts["ti"], consts["g"].astype(BF16))


def _hy_conv_kernel(cw_ref, cb_ref, sk_ref, zy_ref, zg1_ref, zg2_ref, hh_ref, e1_ref, tr_ref, ti_ref,
                    g_ref, gb_ref, e2_ref, o_ref, y_sc, zb_sc, u_sc, pad_sc, *, n_ch, d_model):
    n1 = DFT_N1
    rows = zy_ref.shape[2]
    ch0 = pl.program_id(0) * n_ch
    b_idx = lax.broadcasted_iota(jnp.int32, (rows, n1), 1)
    e1, tr, ti, e2 = e1_ref[...], tr_ref[...], ti_ref[...], e2_ref[...]

    pad = pad_sc.shape[1] - rows
    top = pad // 2
    zero_rows = jnp.zeros((top, n1), F32)

    def short_conv(z_ref, c, slot, col):
        z = z_ref[0, c]
        pad_sc[slot, 0:top, :] = zero_rows
        pad_sc[slot, top + rows:pad + rows, :] = zero_rows
        pad_sc[slot, top:top + rows, :] = z
        up = pad_sc[slot, top - 1:top - 1 + rows, :]
        down = pad_sc[slot, top + 1:top + 1 + rows, :]
        prev = pltpu.roll(jnp.where(b_idx == n1 - 1, up, z), 1, axis=1)
        nxt = pltpu.roll(jnp.where(b_idx == 0, down, z), n1 - 1, axis=1)
        n_col = 3 * d_model
        return cw_ref[col] * prev + cw_ref[n_col + col] * z + cw_ref[2 * n_col + col] * nxt + cb_ref[col]

    gate_refs = (zg1_ref, zg2_ref)

    def spectral_product(order):
        xh = jnp.dot(zb_sc[...], g_ref[...], preferred_element_type=F32)
        hh = hh_ref[order].reshape(n_ch * n1, 2 * n1).astype(F32)
        xr, xi, hr, hi = xh[:, :n1], xh[:, n1:], hh[:, :n1], hh[:, n1:]
        yh = jnp.concatenate([xr * hr - xi * hi, xr * hi + xi * hr], axis=1).astype(BF16)
        u_sc[...] = jnp.dot(yh, gb_ref[...], preferred_element_type=F32)

    def finish(c, order, slot):
        r0 = pl.multiple_of(c * n1, n1)
        ur, ui = u_sc[pl.ds(r0, n1), 0:n1], u_sc[pl.ds(r0, n1), n1:2 * n1]
        stacked = jnp.concatenate([ur * tr + ui * ti, ui * tr - ur * ti], axis=0).astype(BF16)
        conv = jnp.dot(e2, stacked, preferred_element_type=F32)
        gate = short_conv(gate_refs[order], c, slot, (order + 1) * d_model + ch0 + c)
        return gate * (conv + y_sc[c] * sk_ref[order * d_model + ch0 + c])

    n_slot = pad_sc.shape[0]

    def head(q, carry):
        for j in range(n_slot):
            c = q * n_slot + j
            y = short_conv(zy_ref, c, j, ch0 + c)
            y_sc[c] = y
            _dft_stage1(y.astype(BF16), c, e1, tr, ti, zb_sc)
        return carry

    lax.fori_loop(0, n_ch // n_slot, head, 0)
    for order in range(HY_ORDER - 1):
        spectral_product(order)

        def middle(q, carry, order=order):
            for j in range(n_slot):
                c = q * n_slot + j
                y_sc[c] = finish(c, order, j)
            return carry

        def restart(c, carry):
            _dft_stage1(y_sc[c].astype(BF16), c, e1, tr, ti, zb_sc)
            return carry

        lax.fori_loop(0, n_ch // n_slot, middle, 0)
        lax.fori_loop(0, n_ch, restart, 0, unroll=8)
    spectral_product(HY_ORDER - 1)

    def tail(q, carry):
        for j in range(n_slot):
            c = q * n_slot + j
            o_ref[0, c] = finish(c, HY_ORDER - 1, j).astype(o_ref.dtype)
        return carry

    lax.fori_loop(0, n_ch // n_slot, tail, 0)


def hyena_long_conv(z_t, spectrum, conv_w, conv_b, skip, consts):
    B, n_col, L = z_t.shape
    D = n_col // (HY_ORDER + 1)
    n1 = DFT_N1
    rows = L // n1
    assert 2 * rows == n1 and HY_ORDER == 2
    C = HY_CH_BLOCK
    nblk = D // C
    z4 = z_t.reshape(B, n_col, rows, n1)
    smem = pl.BlockSpec(memory_space=pltpu.SMEM)
    zspec = lambda off: pl.BlockSpec((1, C, rows, n1), lambda i, b: (b, i + off * nblk, 0, 0))
    kern = functools.partial(_hy_conv_kernel, n_ch=C, d_model=D)
    out = pl.pallas_call(
        kern,
        grid=(nblk, B),
        in_specs=[smem, smem, smem, zspec(0), zspec(1), zspec(2),
                  pl.BlockSpec((HY_ORDER, C, n1, 2 * n1), lambda i, b: (0, i, 0, 0)),
                  _const_spec((2 * n1, rows)), _const_spec((n1, n1)), _const_spec((n1, n1)),
                  _const_spec((2 * n1, 2 * n1)), _const_spec((2 * n1, 2 * n1)), _const_spec((rows, 2 * n1))],
        out_specs=pl.BlockSpec((1, C, rows, n1), lambda i, b: (b, i, 0, 0)),
        out_shape=jax.ShapeDtypeStruct((B, D, rows, n1), BF16),
        scratch_shapes=[pltpu.VMEM((C, rows, n1), F32),
                        pltpu.VMEM((C * n1, 2 * n1), BF16), pltpu.VMEM((C * n1, 2 * n1), F32),
                        pltpu.VMEM((HY_HEAD_UNROLL, rows + 2 * SUBLANES, n1), F32)],
        compiler_params=pltpu.CompilerParams(
            dimension_semantics=("arbitrary", "arbitrary"), vmem_limit_bytes=VMEM_LIMIT_BYTES),
        name="hyena_long_conv",
    )(conv_w.reshape(-1), conv_b.reshape(-1), skip.reshape(-1), z4, z4, z4, spectrum,
      consts["e1"][:, :rows].astype(BF16), consts["tr"], consts["ti"], consts["g"].astype(BF16),
      consts["gbar"].astype(BF16), consts["e2"][:rows].astype(BF16))
    return out.reshape(B, D, L)


def _hy_short_seq_kernel(zy_ref, zg1_ref, zg2_ref, taps_ref, par_ref, dfull_ref, dinv_ref, o_ref):
    rows, L = zy_ref.shape[1], zy_ref.shape[2]
    lane = lax.broadcasted_iota(jnp.int32, (rows, L), 1)
    par = par_ref[...]
    nfreq = dinv_ref.shape[0] // 2

    def short_conv(z, grp):
        prev = jnp.where(lane == 0, 0.0, pltpu.roll(z, 1, axis=1))
        nxt = jnp.where(lane == L - 1, 0.0, pltpu.roll(z, L - 1, axis=1))
        c = 4 * grp
        return par[:, c:c + 1] * prev + par[:, c + 1:c + 2] * z + par[:, c + 2:c + 3] * nxt + par[:, c + 3:c + 4]

    y = short_conv(zy_ref[0], 0)
    gates = (short_conv(zg1_ref[0], 1), short_conv(zg2_ref[0], 2))
    d_first = dfull_ref[0:L, :]
    for order in range(HY_ORDER):
        hh = jnp.dot(taps_ref[order].astype(BF16), dfull_ref[...], preferred_element_type=F32)
        xh = jnp.dot(y.astype(BF16), d_first, preferred_element_type=F32)
        xr, xi, hr, hi = xh[:, :nfreq], xh[:, nfreq:], hh[:, :nfreq], hh[:, nfreq:]
        yh = jnp.concatenate([xr * hr - xi * hi, xr * hi + xi * hr], axis=1).astype(BF16)
        conv = jnp.dot(yh, dinv_ref[...], preferred_element_type=F32)
        y = gates[order] * (conv + y * par[:, 12 + order:13 + order])
    o_ref[0] = y.astype(o_ref.dtype)


def hyena_short_seq_conv(z_t, taps, conv_w, conv_b, skip):
    B, n_col, L = z_t.shape
    D = n_col // (HY_ORDER + 1)
    n = 2 * L
    idx = np.arange(n, dtype=np.float64)
    ang = 2.0 * np.pi * np.outer(idx, idx) / n
    dfull = jnp.asarray(np.concatenate([np.cos(ang), -np.sin(ang)], axis=1), F32)
    dinv = jnp.asarray(np.concatenate([np.cos(ang), -np.sin(ang)], axis=0)[:, :L] / n, F32)
    cw = conv_w.reshape(HY_SHORT, HY_ORDER + 1, D)
    cb = conv_b.reshape(1, HY_ORDER + 1, D)
    par = jnp.concatenate([cw, cb], axis=0)
    par = jnp.transpose(par, (2, 1, 0)).reshape(D, 4 * (HY_ORDER + 1))
    par = jnp.concatenate([par, skip.T, jnp.zeros((D, 2), F32)], axis=1)
    rows = 256
    nblk = D // rows
    zspec = lambda off: pl.BlockSpec((1, rows, L), lambda i, b: (b, i + off * nblk, 0))
    return pl.pallas_call(
        _hy_short_seq_kernel,
        grid=(nblk, B),
        in_specs=[zspec(0), zspec(1), zspec(2),
                  pl.BlockSpec((HY_ORDER, rows, n), lambda i, b: (0, i, 0)),
                  pl.BlockSpec((rows, par.shape[1]), lambda i, b: (i, 0)),
                  _const_spec(dfull.shape), _const_spec(dinv.shape)],
        out_specs=pl.BlockSpec((1, rows, L), lambda i, b: (b, i, 0)),
        out_shape=jax.ShapeDtypeStruct((B, D, L), BF16),
        compiler_params=pltpu.CompilerParams(
            dimension_semantics=("arbitrary", "arbitrary"), vmem_limit_bytes=VMEM_LIMIT_BYTES),
        name="hyena_short_seq_conv",
    )(z_t, z_t, z_t, taps, par, dfull.astype(BF16), dinv.astype(BF16))


def hyena_layer(x, mod, norm1_g, w_in, b_in, conv_w, conv_b, f_w1, f_b1, f_w2, f_b2, f_w3, f_b3, f_w4,
                freq, bias, w_out, b_out, dft, tail):
    L = x.shape[1]
    taps = hyena_filter_taps(L, f_w1, f_b1, f_w2, f_b2, f_w3, f_b3, f_w4, freq)
    z_t = hyena_in_proj(x, norm1_g, mod, w_in.T.astype(BF16), b_in.reshape(-1, 1))
    if 2 * L == DFT_N1 * DFT_N1:
        spec = hyena_filter_spectrum(taps.reshape(HY_ORDER * D_MODEL, 2 * L), dft)
        spec = spec.reshape(HY_ORDER, D_MODEL, DFT_N1, 2 * DFT_N1)
        y_t = hyena_long_conv(z_t, spec, conv_w, conv_b, bias, dft)
    else:
        y_t = hyena_short_seq_conv(z_t, taps, conv_w, conv_b, bias)
    return layer_tail("hyena", (y_t, w_out.astype(BF16), b_out), x, mod, *tail)


def _gla_in_kernel(x_ref, g_ref, mod_ref, w_ref, wr_ref, w2_ref, gb_ref, qk_ref, v_ref, og_ref, gate_ref):
    h = _norm_mod(x_ref[0], g_ref[...], mod_ref[0, 0:1, :], mod_ref[0, 1:2, :]).astype(BF16)
    n = qk_ref.shape[2]
    qk = jnp.dot(h, w_ref[:, 0:n], preferred_element_type=F32)
    half = n // 2
    qk_ref[0, :, 0:half] = qk[:, 0:half] * (GLA_HK ** -0.5)
    qk_ref[0, :, half:n] = qk[:, half:n]
    v_ref[0] = jnp.dot(h, w_ref[:, n:2 * n], preferred_element_type=F32).astype(v_ref.dtype)
    og_ref[0] = jnp.dot(h, w_ref[:, 2 * n:3 * n], preferred_element_type=F32)
    r = jnp.dot(h, wr_ref[...], preferred_element_type=F32).astype(BF16)
    gk = jnp.dot(r, w2_ref[...], preferred_element_type=F32) + gb_ref[...]
    gate_ref[0] = -(jnp.maximum(-gk, 0.0) + jnp.log1p(jnp.exp(-jnp.abs(gk)))) * (1.0 / GLA_GATE_NORM)


def gla_in_proj(x, norm_g, mod, w_in, gk_w2, gk_b):
    B, L, D = x.shape
    tm = min(PROJ_TOKENS, L)
    per_batch = mod.shape[0] != 1
    n_main = 2 * GLA_DK + 2 * GLA_DV
    w_main = w_in[:, :n_main].astype(BF16)
    lanes = LANES
    w_r = jnp.pad(w_in[:, n_main:], ((0, 0), (0, lanes - 2 * GLA_GATE_RANK))).astype(BF16)
    w2 = jnp.zeros((lanes, 2 * GLA_DK), F32)
    w2 = w2.at[:GLA_GATE_RANK, :GLA_DK].set(gk_w2[0]).at[GLA_GATE_RANK:2 * GLA_GATE_RANK, GLA_DK:].set(gk_w2[1])
    tok = lambda n: pl.BlockSpec((1, tm, n), lambda b, i: (b, i, 0))
    n = 2 * GLA_DK
    assert GLA_DV == n
    return pl.pallas_call(
        _gla_in_kernel,
        grid=(B, L // tm),
        in_specs=[tok(D), _const_spec((1, D)),
                  pl.BlockSpec((1, 6, D), (lambda b, i: (b, 0, 0)) if per_batch else (lambda b, i: (0, 0, 0))),
                  _const_spec(w_main.shape), _const_spec(w_r.shape), _const_spec(w2.shape), _const_spec((1, n))],
        out_specs=[tok(n), tok(n), tok(n), tok(n)],
        out_shape=[jax.ShapeDtypeStruct((B, L, n), F32), jax.ShapeDtypeStruct((B, L, n), BF16),
                   jax.ShapeDtypeStruct((B, L, n), F32), jax.ShapeDtypeStruct((B, L, n), F32)],
        compiler_params=pltpu.CompilerParams(
            dimension_semantics=("arbitrary", "arbitrary"), vmem_limit_bytes=VMEM_LIMIT_BYTES),
        name="gla_in_proj",
    )(x, norm_g.reshape(1, D), mod, w_main, w_r, w2.astype(BF16), gk_b.reshape(1, n))


def _gla_scan_kernel(*refs, reverse, add_prev):
    if add_prev:
        qk_ref, v_ref, g_ref, s0_ref, prev_ref, o_ref, sfin_ref, st_ref = refs
    else:
        qk_ref, v_ref, g_ref, s0_ref, o_ref, sfin_ref, st_ref = refs
        prev_ref = None
    i = pl.program_id(1)
    C, H, dk, dv = GLA_CHUNK, GLA_HEADS, GLA_HK, GLA_HV

    @pl.when(i == 0)
    def _():
        st_ref[...] = s0_ref[0]

    r_idx = lax.broadcasted_iota(jnp.int32, (C, C), 0)
    c_idx = lax.broadcasted_iota(jnp.int32, (C, C), 1)
    keep = (r_idx <= c_idx) if reverse else (r_idx >= c_idx)
    tri = keep.astype(F32)
    n_chunks = qk_ref.shape[1] // C
    order = range(n_chunks - 1, -1, -1) if reverse else range(n_chunks)
    for ci in order:
        rows = slice(ci * C, (ci + 1) * C)
        b = jnp.dot(tri, g_ref[0, rows, :], precision=lax.Precision.HIGHEST, preferred_element_type=F32)
        b_last = b[0:1] if reverse else b[C - 1:C]
        e_pos, e_neg, e_end, dec = jnp.exp(b), jnp.exp(-b), jnp.exp(b_last - b), jnp.exp(b_last)
        for h in range(H):
            kc = slice(h * dk, (h + 1) * dk)
            vc = slice(h * dv, (h + 1) * dv)
            q = qk_ref[0, rows, kc]
            k = qk_ref[0, rows, H * dk + h * dk:H * dk + (h + 1) * dk]
            v = v_ref[0, rows, vc]
            q_t = (q * e_pos[:, kc]).astype(BF16)
            k_t = (k * e_neg[:, kc]).astype(BF16)
            k_end = (k * e_end[:, kc]).astype(BF16)
            att = lax.dot_general(q_t, k_t, (((1,), (1,)), ((), ())), preferred_element_type=F32)
            att = jnp.where(keep, att, 0.0).astype(BF16)
            st = st_ref[h]
            o = jnp.dot(att, v, preferred_element_type=F32) + lax.dot_general(
                q_t, st.astype(BF16), (((1,), (1,)), ((), ())), preferred_element_type=F32)
            st_ref[h] = st * dec[:, kc] + lax.dot_general(
                v, k_end, (((0,), (0,)), ((), ())), preferred_element_type=F32)
            if prev_ref is not None:
                o = o + prev_ref[0, rows, vc]
            o_ref[0, rows, vc] = o

    @pl.when(i == pl.num_programs(1) - 1)
    def _():
        sfin_ref[0] = st_ref[...]


def gla_scan(qk, v, gates, s0, direction, prev=None):
    B, L, _ = qk.shape
    reverse = direction == 1
    T = min(512, L)
    nT = L // T
    H, dk, dv = GLA_HEADS, GLA_HK, GLA_HV
    blk = (lambda i: nT - 1 - i) if reverse else (lambda i: i)
    tok = lambda n, col=0: pl.BlockSpec((1, T, n), lambda b, i: (b, blk(i), col))
    st_spec = pl.BlockSpec((1, H, dv, dk), lambda b, i: (b, 0, 0, 0))
    in_specs = [tok(2 * GLA_DK), tok(GLA_DV), tok(GLA_DK, direction), st_spec]
    args = [qk, v, gates, s0]
    if prev is not None:
        in_specs.append(tok(GLA_DV))
        args.append(prev)
    kern = functools.partial(_gla_scan_kernel, reverse=reverse, add_prev=prev is not None)
    return pl.pallas_call(
        kern,
        grid=(B, nT),
        in_specs=in_specs,
        out_specs=[tok(GLA_DV), st_spec],
        out_shape=[jax.ShapeDtypeStruct((B, L, GLA_DV), F32), jax.ShapeDtypeStruct((B, H, dv, dk), F32)],
        scratch_shapes=[pltpu.VMEM((H, dv, dk), F32)],
        compiler_params=pltpu.CompilerParams(
            dimension_semantics=("arbitrary", "arbitrary"), vmem_limit_bytes=VMEM_LIMIT_BYTES),
        name="gla_scan_bwd" if reverse else "gla_scan_fwd",
    )(*args)


def gla_layer(x_lat, x_ctx, mod_lat, mod_ctx, norm1_g, w_in, gk_w2, gk_b, onorm, wo, tail_lat, tail_ctx):
    B = x_lat.shape[0]
    qk_l, v_l, og_l, g_l = gla_in_proj(x_lat, norm1_g, mod_lat, w_in, gk_w2, gk_b)
    qk_c, v_c, og_c, g_c = gla_in_proj(x_ctx, norm1_g, mod_ctx, w_in, gk_w2, gk_b)
    s0 = jnp.zeros((B, GLA_HEADS, GLA_HV, GLA_HK), F32)
    oc, s_f = gla_scan(qk_c, v_c, g_c, s0, 0)
    oc, s_b = gla_scan(qk_c, v_c, g_c, s0, 1, prev=oc)
    ol, _ = gla_scan(qk_l, v_l, g_l, s_f, 0)
    ol, _ = gla_scan(qk_l, v_l, g_l, s_b, 1, prev=ol)
    wo = wo.astype(BF16)
    x_lat = layer_tail("gla", (ol, og_l, onorm, wo), x_lat, mod_lat, *tail_lat)
    if tail_ctx is not None:
        x_ctx = layer_tail("gla", (oc, og_c, onorm, wo), x_ctx, mod_ctx, *tail_ctx)
    return x_lat, x_ctx


def _rope_swap_perm():
    half = MLA_ROPE // 2
    quarter = half // 2
    p = []
    for base in (0, half):
        p += list(range(base + quarter, base + half)) + list(range(base, base + quarter))
    return np.asarray(p)


def mla_rope_tables(L, rotate):
    half = MLA_ROPE // 2
    zeros = jnp.zeros((L, MLA_ROPE), F32)
    if not rotate:
        return jnp.concatenate([jnp.ones((L, MLA_ROPE), F32), zeros], axis=1), jnp.zeros((L, 2 * MLA_ROPE), F32)
    pos = jnp.arange(L)
    inv_freq = ROPE_THETA ** (-jnp.arange(0, half, 2, dtype=F32) / half)
    ang_row = (pos // GRID_W).astype(F32)[:, None] * inv_freq[None, :]
    ang_col = (pos % GRID_W).astype(F32)[:, None] * inv_freq[None, :]
    cr, sr, cc, sc = jnp.cos(ang_row), jnp.sin(ang_row), jnp.cos(ang_col), jnp.sin(ang_col)
    cos = jnp.concatenate([cr, cr, cc, cc, zeros], axis=1)
    sin = jnp.concatenate([-sr, sr, -sc, sc, zeros], axis=1)
    return cos, sin


def _mla_qkv_kernel(x_ref, g_ref, mod_ref, wd_ref, qn_ref, wq_ref, kn_ref, wkv_ref, cos_ref, sin_ref,
                    q_ref, k_ref, v_ref):
    h = _norm_mod(x_ref[0], g_ref[...], mod_ref[0, 0:1, :], mod_ref[0, 1:2, :]).astype(BF16)
    c = jnp.dot(h, wd_ref[...], preferred_element_type=F32)
    cos, sin = cos_ref[...], sin_ref[...]
    lanes = cos.shape[1]

    def rms(a, g):
        return (a * lax.rsqrt(jnp.mean(a * a, axis=-1, keepdims=True) + NORM_EPS) * g).astype(BF16)

    def rope(tile):
        return tile * cos + pltpu.roll(tile, lanes // 2, axis=1) * sin

    cq = rms(c[:, :MLA_Q_RANK], qn_ref[...])
    ckv = rms(c[:, MLA_Q_RANK:MLA_Q_RANK + MLA_KV_RANK], kn_ref[...])
    k_rope = rope(c[:, MLA_Q_RANK + MLA_KV_RANK:]).astype(k_ref.dtype)
    q = jnp.dot(cq, wq_ref[...], preferred_element_type=F32)
    kv = jnp.dot(ckv, wkv_ref[...], preferred_element_type=F32)
    ones = jnp.ones((x_ref.shape[1], MLA_V), v_ref.dtype)
    for hd in range(MLA_HEADS):
        o = hd * MLA_QK_PAD
        q_ref[0, :, o:o + MLA_NOPE] = (q[:, o:o + MLA_NOPE] * MLA_Q_PRESCALE).astype(q_ref.dtype)
        q_ref[0, :, o + MLA_NOPE:o + MLA_QK_PAD] = (
            rope(q[:, o + MLA_NOPE:o + MLA_QK_PAD]) * MLA_Q_PRESCALE).astype(q_ref.dtype)
        k_ref[0, :, o:o + MLA_NOPE] = kv[:, o:o + MLA_NOPE].astype(k_ref.dtype)
        k_ref[0, :, o + MLA_NOPE:o + MLA_QK_PAD] = k_rope
        v_ref[0, :, 2 * hd * MLA_V:(2 * hd + 1) * MLA_V] = kv[:, o + MLA_NOPE:o + MLA_QK_PAD].astype(v_ref.dtype)
        v_ref[0, :, (2 * hd + 1) * MLA_V:(2 * hd + 2) * MLA_V] = ones


def mla_qkv_proj(x, norm_g, mod, w_down, qnorm, w_uq, kvnorm, w_ukv, rotate):
    B, L, D = x.shape
    tm = min(PROJ_TOKENS, L)
    per_batch = mod.shape[0] != 1
    perm = _rope_swap_perm()
    rope0 = MLA_Q_RANK + MLA_KV_RANK
    wd = jnp.concatenate([w_down, w_down[:, rope0:][:, perm]], axis=1).astype(BF16)
    wq = w_uq.reshape(MLA_Q_RANK, MLA_HEADS, MLA_NOPE + MLA_ROPE)
    wq = jnp.concatenate([wq, wq[:, :, MLA_NOPE:][:, :, perm]], axis=2)
    wq = wq.reshape(MLA_Q_RANK, MLA_HEADS * MLA_QK_PAD).astype(BF16)
    cos, sin = mla_rope_tables(L, rotate)
    tok = lambda n: pl.BlockSpec((1, tm, n), lambda b, i: (b, i, 0))
    nq = MLA_HEADS * MLA_QK_PAD
    nv = MLA_HEADS * 2 * MLA_V
    tab = pl.BlockSpec((tm, 2 * MLA_ROPE), lambda b, i: (i, 0))
    return pl.pallas_call(
        _mla_qkv_kernel,
        grid=(B, L // tm),
        in_specs=[tok(D), _const_spec((1, D)),
                  pl.BlockSpec((1, 6, D), (lambda b, i: (b, 0, 0)) if per_batch else (lambda b, i: (0, 0, 0))),
                  _const_spec(wd.shape), _const_spec((1, MLA_Q_RANK)), _const_spec(wq.shape),
                  _const_spec((1, MLA_KV_RANK)), _const_spec(w_ukv.shape), tab, tab],
        out_specs=[tok(nq), tok(nq), tok(nv)],
        out_shape=[jax.ShapeDtypeStruct((B, L, nq), BF16), jax.ShapeDtypeStruct((B, L, nq), BF16),
                   jax.ShapeDtypeStruct((B, L, nv), BF16)],
        compiler_params=pltpu.CompilerParams(
            dimension_semantics=("arbitrary", "arbitrary"), vmem_limit_bytes=VMEM_LIMIT_BYTES),
        name="mla_qkv_proj",
    )(x, norm_g.reshape(1, D), mod, wd, qnorm.reshape(1, -1), wq, kvnorm.reshape(1, -1),
      w_ukv.astype(BF16), cos, sin)


def mla_layer(x_lat, x_ctx, mod_lat, mod_ctx, norm1_g, w_down, qnorm, w_uq, kvnorm, w_ukv, wo, tail):
    ql, kl, vl = mla_qkv_proj(x_lat, norm1_g, mod_lat, w_down, qnorm, w_uq, kvnorm, w_ukv, True)
    _, kc, vc = mla_qkv_proj(x_ctx, norm1_g, mod_ctx, w_down, qnorm, w_uq, kvnorm, w_ukv, False)
    o = mla_attention(ql, kc, vc, kl, vl)
    return layer_tail("mla", (o, wo.astype(BF16)), x_lat, mod_lat, *tail)


def _adaln_kernel(c_ref, w_ref, b_ref, o_ref):
    cond = c_ref[...]
    s = (cond * jax.nn.sigmoid(cond)).astype(BF16)
    o_ref[0] = jnp.dot(s, w_ref[0].astype(BF16), preferred_element_type=F32) + b_ref[0]


def adaln_modulation(cond, ada_w, ada_b):
    R, D = cond.shape
    depth, _, n_out = ada_w.shape
    tn = 1024
    return pl.pallas_call(
        _adaln_kernel,
        grid=(depth, n_out // tn),
        in_specs=[_const_spec((R, D)),
                  pl.BlockSpec((1, D, tn), lambda i, j: (i, 0, j)),
                  pl.BlockSpec((1, 1, tn), lambda i, j: (i, 0, j))],
        out_specs=pl.BlockSpec((1, R, tn), lambda i, j: (i, 0, j)),
        out_shape=jax.ShapeDtypeStruct((depth, R, n_out), F32),
        compiler_params=pltpu.CompilerParams(dimension_semantics=("arbitrary", "arbitrary")),
        name="adaln_modulation",
    )(cond, ada_w, ada_b.reshape(depth, 1, n_out))


def kernel(x, c, ctx, c_ctx, ada_w, ada_b, norm1_g, norm2_g, mlp_w1, mlp_w2, final_g, hy_w_in, hy_b_in, hy_conv_w, hy_conv_b, hy_f_w1, hy_f_b1, hy_f_w2, hy_f_b2, hy_f_w3, hy_f_b3, hy_f_w4, hy_freq, hy_bias, hy_w_out, hy_b_out, gla_w_in, gla_gk_w2, gla_gk_b, gla_onorm, gla_wo, mla_w_down, mla_qnorm, mla_w_uq, mla_kvnorm, mla_w_ukv, mla_wo):
    x_lat = x
    x_ctx = ctx
    B = x.shape[0]
    cond = jnp.concatenate([c, c_ctx[None, :], jnp.zeros((SUBLANES - (B + 1) % SUBLANES, D_MODEL), F32)], axis=0)
    mod_all = adaln_modulation(cond, ada_w, ada_b)
    dft = _dft_constants()
    for i in range(DEPTH):
        kind = i % N_MIXERS
        j = i // N_MIXERS
        ctx_live = any(l % N_MIXERS != 0 for l in range(i + 1, DEPTH))
        mod_lat = mod_all[i, :B].reshape(B, 6, D_MODEL)
        mod_ctx = mod_all[i, B:B + 1].reshape(1, 6, D_MODEL)
        w1 = mlp_w1[i].astype(BF16)
        w2 = mlp_w2[i].astype(BF16)
        tail_lat = (norm2_g[i], w1, w2, final_g if i == DEPTH - 1 else None)
        tail_ctx = (norm2_g[i], w1, w2)
        if kind == 0:
            hp = (hy_w_in[j], hy_b_in[j], hy_conv_w[j], hy_conv_b[j], hy_f_w1[j], hy_f_b1[j],
                  hy_f_w2[j], hy_f_b2[j], hy_f_w3[j], hy_f_b3[j], hy_f_w4[j], hy_freq[j],
                  hy_bias[j], hy_w_out[j], hy_b_out[j])
            x_lat = hyena_layer(x_lat, mod_lat, norm1_g[i], *hp, dft, tail_lat)
            if ctx_live:
                x_ctx = hyena_layer(x_ctx, mod_ctx, norm1_g[i], *hp, dft, tail_ctx)
        elif kind == 1:
            x_lat, x_ctx = gla_layer(x_lat, x_ctx, mod_lat, mod_ctx, norm1_g[i], gla_w_in[j], gla_gk_w2[j],
                                     gla_gk_b[j], gla_onorm[j], gla_wo[j], tail_lat,
                                     tail_ctx if ctx_live else None)
        else:
            assert not ctx_live
            x_lat = mla_layer(x_lat, x_ctx, mod_lat, mod_ctx, norm1_g[i], mla_w_down[j], mla_qnorm[j],
                              mla_w_uq[j], mla_kvnorm[j], mla_w_ukv[j], mla_wo[j], tail_lat)
    return x_lat
```

```python
import functools
import math

import jax
import jax.numpy as jnp
import numpy as np
from jax import lax
from jax.experimental import pallas as pl
from jax.experimental.pallas import tpu as pltpu

F32 = jnp.float32
BF16 = jnp.bfloat16

D_MODEL = 1024
DEPTH = 4
GRID_W = 64
N_MIXERS = 3
NORM_EPS = 1e-6

HY_ORDER = 2
HY_EMB = 33
HY_SHORT = 3
HY_FAST_DECAY = 0.3
HY_SLOW_DECAY = 1.5
HY_TARGET = 1e-2

GLA_HEADS = 4
GLA_DK = D_MODEL // 2
GLA_DV = D_MODEL
GLA_HK = GLA_DK // GLA_HEADS
GLA_HV = GLA_DV // GLA_HEADS
GLA_GATE_RANK = 16
GLA_GATE_NORM = 16.0
GLA_CHUNK = 64

MLA_HEADS = 8
MLA_Q_RANK = 384
MLA_KV_RANK = 256
MLA_NOPE = 128
MLA_ROPE = 64
MLA_V = 128
ROPE_THETA = 10000.0

VMEM_LIMIT_BYTES = 56 * 1024 * 1024
MLA_QK_PAD = 256
SUBLANES = 8
LANES = 128
PROJ_TOKENS = 1024
TAIL_TOKENS = 512
MLA_Q_PRESCALE = (MLA_NOPE + MLA_ROPE) ** -0.5 * math.log2(math.e)


def _const_spec(shape):
    nd = len(shape)
    return pl.BlockSpec(shape, lambda *_: (0,) * nd, pipeline_mode=pl.Buffered(1))


def _norm_mod(x, g, shift, scale):
    y = x * lax.rsqrt(jnp.mean(x * x, axis=-1, keepdims=True) + NORM_EPS)
    return (y * g) * (1.0 + scale) + shift


def _layer_tail_kernel(*refs, kind, hidden_chunk, final_norm):
    if kind == "hyena":
        y_ref, wo_ref, bo_ref = refs[:3]
        rest = refs[3:]
        y = lax.dot_general(y_ref[0], wo_ref[...], (((0,), (0,)), ((), ())),
                            preferred_element_type=F32) + bo_ref[...]
    elif kind == "gla":
        o_ref_in, og_ref, on_ref, wo_ref = refs[:4]
        rest = refs[4:]
        og = og_ref[0]
        parts = []
        for h in range(GLA_HEADS):
            cols = slice(h * GLA_HV, (h + 1) * GLA_HV)
            o = o_ref_in[0, :, cols]
            n = o * lax.rsqrt(jnp.mean(o * o, axis=-1, keepdims=True) + NORM_EPS) * on_ref[...]
            parts.append((n * (og[:, cols] * jax.nn.sigmoid(og[:, cols]))).astype(BF16))
        y = jnp.dot(jnp.concatenate(parts, axis=1), wo_ref[...], preferred_element_type=F32)
    else:
        a_ref, wo_ref = refs[:2]
        rest = refs[2:]
        y = jnp.dot(a_ref[0], wo_ref[...], preferred_element_type=F32)
    x_ref, g_ref, mod_ref, w1_ref, w2_ref, fg_ref, out_ref = rest
    x = x_ref[0] + mod_ref[0, 2:3, :] * y
    h = _norm_mod(x, g_ref[...], mod_ref[0, 3:4, :], mod_ref[0, 4:5, :]).astype(BF16)
    hidden = w1_ref.shape[1]
    acc = jnp.zeros(x.shape, F32)
    for c0 in range(0, hidden, hidden_chunk):
        a = jnp.dot(h, w1_ref[:, c0:c0 + hidden_chunk], preferred_element_type=F32)
        a = jnp.square(jnp.maximum(a, 0.0)).astype(BF16)
        acc = acc + jnp.dot(a, w2_ref[c0:c0 + hidden_chunk, :], preferred_element_type=F32)
    out = x + mod_ref[0, 5:6, :] * acc
    if final_norm:
        out = (out * lax.rsqrt(jnp.mean(out * out, axis=-1, keepdims=True) + NORM_EPS)) * fg_ref[...]
    out_ref[0] = out


def layer_tail(kind, mixer_args, x, mod, norm_g, w1, w2, final_g=None):
    B, L, D = x.shape
    tm = min(TAIL_TOKENS, L)
    per_batch = mod.shape[0] != 1
    final_norm = final_g is not None
    fg = (final_g if final_norm else norm_g).reshape(1, D)
    tok = lambda n: pl.BlockSpec((1, tm, n), lambda b, i: (b, i, 0))
    if kind == "hyena":
        y_t, w_o, b_o = mixer_args
        m_specs = [pl.BlockSpec((1, D, tm), lambda b, i: (b, 0, i)), _const_spec(w_o.shape), _const_spec((1, D))]
        m_args = [y_t, w_o, b_o.reshape(1, D)]
    elif kind == "gla":
        o, og, onorm, w_o = mixer_args
        m_specs = [tok(D), tok(D), _const_spec((1, GLA_HV)), _const_spec(w_o.shape)]
        m_args = [o, og, onorm.reshape(1, GLA_HV), w_o]
    else:
        a, w_o = mixer_args
        m_specs = [tok(a.shape[2]), _const_spec(w_o.shape)]
        m_args = [a, w_o]
    kern = functools.partial(_layer_tail_kernel, kind=kind, hidden_chunk=1024, final_norm=final_norm)
    return pl.pallas_call(
        kern,
        grid=(B, L // tm),
        in_specs=m_specs + [
            tok(D),
            _const_spec((1, D)),
            pl.BlockSpec((1, 6, D), (lambda b, i: (b, 0, 0)) if per_batch else (lambda b, i: (0, 0, 0))),
            _const_spec(w1.shape),
            _const_spec(w2.shape),
            _const_spec((1, D)),
        ],
        out_specs=tok(D),
        out_shape=jax.ShapeDtypeStruct((B, L, D), F32),
        compiler_params=pltpu.CompilerParams(
            dimension_semantics=("arbitrary", "arbitrary"), vmem_limit_bytes=VMEM_LIMIT_BYTES),
        name="layer_tail_" + kind,
    )(*m_args, x, norm_g.reshape(1, D), mod, w1, w2, fg)


def _attn_kernel(q_ref, kc_ref, vc_ref, k_ref, v_ref, o_ref, m_ref, acc_ref, sa_ref, sb_ref, *, sub):
    q = q_ref[0]
    m_ref[...] = jnp.full(m_ref.shape, -jnp.inf, F32)
    acc_ref[...] = jnp.zeros(acc_ref.shape, F32)
    lanes = m_ref.shape[1]
    n_sub = k_ref.shape[1] // sub

    def scores(k):
        return lax.dot_general(q, k, (((1,), (1,)), ((), ())), preferred_element_type=F32)

    def lat(ref, n):
        return ref[0, pl.ds(pl.multiple_of(n * sub, sub), sub), :]

    def accumulate(s, v):
        m_prev = m_ref[...]
        m_new = jnp.maximum(m_prev, jnp.max(s, axis=-1, keepdims=True))
        alpha = jnp.exp2(m_prev - m_new)
        ps = [jnp.exp2(s[:, t:t + lanes] - m_new) for t in range(0, s.shape[1], lanes)]
        p = jnp.concatenate(ps, axis=1).astype(BF16)
        pv = jnp.dot(p, v, preferred_element_type=F32)
        for t in range(0, acc_ref.shape[1], lanes):
            acc_ref[:, t:t + lanes] = alpha * acc_ref[:, t:t + lanes] + pv[:, t:t + lanes]
        m_ref[...] = m_new

    sa_ref[...] = scores(lat(k_ref, 0))
    accumulate(scores(kc_ref[0]), vc_ref[0])

    def pair(n):
        sb_ref[...] = scores(lat(k_ref, n + 1))
        accumulate(sa_ref[...], lat(v_ref, n))

    def body(j, carry):
        n = 2 * j
        pair(n)
        sa_ref[...] = scores(lat(k_ref, n + 2))
        accumulate(sb_ref[...], lat(v_ref, n + 1))
        return carry

    lax.fori_loop(0, n_sub // 2 - 1, body, 0)
    pair(n_sub - 2)
    accumulate(sb_ref[...], lat(v_ref, n_sub - 1))
    o_ref[0] = (acc_ref[:, 0:lanes] / acc_ref[:, lanes:2 * lanes]).astype(o_ref.dtype)


def mla_attention(q, kc, vc, k, v):
    B, L, _ = q.shape
    C = kc.shape[1]
    H = MLA_HEADS
    tq = min(2048, L)
    sub = min(512, L // 4)
    assert L % (2 * sub) == 0 and L % tq == 0
    kern = functools.partial(_attn_kernel, sub=sub)
    return pl.pallas_call(
        kern,
        grid=(B, H, L // tq),
        in_specs=[
            pl.BlockSpec((1, tq, MLA_QK_PAD), lambda b, h, i: (b, i, h)),
            pl.BlockSpec((1, C, MLA_QK_PAD), lambda b, h, i: (b, 0, h)),
            pl.BlockSpec((1, C, 2 * MLA_V), lambda b, h, i: (b, 0, h)),
            pl.BlockSpec((1, L, MLA_QK_PAD), lambda b, h, i: (b, 0, h)),
            pl.BlockSpec((1, L, 2 * MLA_V), lambda b, h, i: (b, 0, h)),
        ],
        out_specs=pl.BlockSpec((1, tq, MLA_V), lambda b, h, i: (b, i, h)),
        out_shape=jax.ShapeDtypeStruct((B, L, H * MLA_V), BF16),
        scratch_shapes=[pltpu.VMEM((tq, MLA_V), F32), pltpu.VMEM((tq, 2 * MLA_V), F32),
                        pltpu.VMEM((tq, sub), F32), pltpu.VMEM((tq, sub), F32)],
        compiler_params=pltpu.CompilerParams(
            dimension_semantics=("arbitrary", "arbitrary", "arbitrary"), vmem_limit_bytes=VMEM_LIMIT_BYTES),
        name="mla_attention",
    )(q, kc, vc, k, v)


DFT_N1 = 128
HY_CH_BLOCK = 32
HY_HEAD_UNROLL = 8
HY_TAP_ROWS = 64


def _dft_constants():
    n1 = DFT_N1
    n = n1 * n1
    idx = np.arange(n1, dtype=np.float64)
    th = 2.0 * np.pi * np.outer(idx, idx) / n1
    cos1, sin1 = np.cos(th), np.sin(th)
    tw = 2.0 * np.pi * np.outer(idx, idx) / n
    fr, fi = cos1, -sin1
    c = dict(
        e1=np.concatenate([cos1, -sin1], axis=0),
        tr=np.cos(tw), ti=-np.sin(tw),
        g=np.block([[fr, fi], [-fi, fr]]),
        gbar=np.block([[fr, -fi], [fi, fr]]),
        e2=np.concatenate([cos1, -sin1], axis=1) / n,
    )
    return {k: jnp.asarray(v, F32) for k, v in c.items()}


def _hy_in_kernel(x_ref, g_ref, mod_ref, wt_ref, b_ref, o_ref, *, row_chunk):
    h = _norm_mod(x_ref[0], g_ref[...], mod_ref[0, 0:1, :], mod_ref[0, 1:2, :]).astype(BF16)
    for r0 in range(0, wt_ref.shape[0], row_chunk):
        z = lax.dot_general(wt_ref[r0:r0 + row_chunk, :], h, (((1,), (1,)), ((), ())),
                            preferred_element_type=F32)
        o_ref[0, r0:r0 + row_chunk, :] = z + b_ref[r0:r0 + row_chunk, :]


def hyena_in_proj(x, norm_g, mod, w_in_t, b_in):
    B, L, D = x.shape
    n_out = w_in_t.shape[0]
    tm = min(PROJ_TOKENS, L)
    per_batch = mod.shape[0] != 1
    return pl.pallas_call(
        functools.partial(_hy_in_kernel, row_chunk=512),
        grid=(B, L // tm),
        in_specs=[
            pl.BlockSpec((1, tm, D), lambda b, i: (b, i, 0)),
            _const_spec((1, D)),
            pl.BlockSpec((1, 6, D), (lambda b, i: (b, 0, 0)) if per_batch else (lambda b, i: (0, 0, 0))),
            _const_spec(w_in_t.shape),
            _const_spec(b_in.shape),
        ],
        out_specs=pl.BlockSpec((1, n_out, tm), lambda b, i: (b, 0, i)),
        out_shape=jax.ShapeDtypeStruct((B, n_out, L), F32),
        compiler_params=pltpu.CompilerParams(
            dimension_semantics=("arbitrary", "arbitrary"), vmem_limit_bytes=VMEM_LIMIT_BYTES),
        name="hyena_in_proj",
    )(x, norm_g.reshape(1, D), mod, w_in_t, b_in)


def _hy_hidden_kernel(zf_ref, w1_ref, b1_ref, w2_ref, b2_ref, w3_ref, b3_ref, fr_ref, o_ref):
    fr = fr_ref[...]
    h = zf_ref[...].astype(BF16)
    for w_ref, b_ref in ((w1_ref, b1_ref), (w2_ref, b2_ref), (w3_ref, b3_ref)):
        h = jnp.sin(fr * (jnp.dot(w_ref[...], h, preferred_element_type=F32) + b_ref[...]))
        out = h
        h = h.astype(BF16)
    o_ref[...] = out


def _hy_taps_kernel(hid_ref, t_ref, w4_ref, dl_ref, o_ref, *, half):
    hid = hid_ref[...].astype(BF16)
    tf = jnp.dot(w4_ref[0, 0].astype(BF16), hid[:, :half], preferred_element_type=F32)
    tb = jnp.dot(w4_ref[0, 1].astype(BF16), hid[:, half:], preferred_element_type=F32)
    taps = jnp.concatenate([tf, tb], axis=1) * jnp.exp(-t_ref[...] * dl_ref[...])
    pos = lax.broadcasted_iota(jnp.int32, taps.shape, 1)
    taps = jnp.where(pos == half, 0.0, taps)
    o_ref[0] = (taps / jnp.sum(jnp.abs(taps), axis=1, keepdims=True)).astype(o_ref.dtype)


def hyena_filter_taps(L, f_w1, f_b1, f_w2, f_b2, f_w3, f_b3, f_w4, freq):
    width = f_w1.shape[1]
    n = 2 * L
    pos = np.arange(n)
    pos = np.where(pos <= L, np.minimum(pos, L - 1), n - pos).astype(np.float64)
    bands = (HY_EMB - 1) // 2
    t_np = (pos / (L - 1))[None, :]
    w_np = (2.0 * math.pi * pos / L)[None, :]
    f_np = np.linspace(1e-4, bands - 1, bands)[:, None]
    zf_np = np.concatenate([t_np, np.cos(f_np * w_np), -np.sin(f_np * w_np),
                            np.zeros((width - HY_EMB, n))], axis=0)
    t = jnp.asarray(t_np, F32)
    zf = jnp.asarray(zf_np, F32)
    w1t = jnp.pad(f_w1.T, ((0, 0), (0, width - HY_EMB))).astype(BF16)
    col = lambda v: v.reshape(width, 1).astype(F32)
    lane_blk = min(2048, n)
    hidden = pl.pallas_call(
        _hy_hidden_kernel,
        grid=(n // lane_blk,),
        in_specs=[pl.BlockSpec((width, lane_blk), lambda i: (0, i))] + [_const_spec((width, width)), _const_spec((width, 1))] * 3
        + [_const_spec((width, 1))],
        out_specs=pl.BlockSpec((width, lane_blk), lambda i: (0, i)),
        out_shape=jax.ShapeDtypeStruct((width, n), F32),
        name="hyena_filter_hidden",
    )(zf, w1t, col(f_b1), f_w2.T.astype(BF16), col(f_b2), f_w3.T.astype(BF16), col(f_b3), col(freq))
    max_decay = math.log(HY_TARGET) / HY_FAST_DECAY
    min_decay = math.log(HY_TARGET) / HY_SLOW_DECAY
    deltas = jnp.abs(jnp.linspace(min_decay, max_decay, D_MODEL, dtype=F32)).reshape(D_MODEL, 1)
    w4t = f_w4.T.reshape(HY_ORDER, 2, D_MODEL, width)
    rows = HY_TAP_ROWS
    return pl.pallas_call(
        functools.partial(_hy_taps_kernel, half=L),
        grid=(HY_ORDER, D_MODEL // rows),
        in_specs=[
            _const_spec((width, n)),
            _const_spec((1, n)),
            pl.BlockSpec((1, 2, rows, width), lambda o, i: (o, 0, i, 0)),
            pl.BlockSpec((rows, 1), lambda o, i: (i, 0)),
        ],
        out_specs=pl.BlockSpec((1, rows, n), lambda o, i: (o, i, 0)),
        out_shape=jax.ShapeDtypeStruct((HY_ORDER, D_MODEL, n), BF16),
        compiler_params=pltpu.CompilerParams(
            dimension_semantics=("arbitrary", "arbitrary"), vmem_limit_bytes=VMEM_LIMIT_BYTES),
        name="hyena_filter_taps",
    )(hidden, t, w4t, deltas)


def _dft_stage1(x, c, e1, tr, ti, zb_ref):
    n1 = DFT_N1
    z = jnp.dot(e1, x, preferred_element_type=F32)
    zr, zi = z[:n1], z[n1:]
    r0 = pl.multiple_of(c * n1, n1)
    zb_ref[pl.ds(r0, n1), 0:n1] = (zr * tr - zi * ti).astype(BF16)
    zb_ref[pl.ds(r0, n1), n1:2 * n1] = (zr * ti + zi * tr).astype(BF16)


def _hy_spectrum_kernel(x_ref, e1_ref, tr_ref, ti_ref, g_ref, o_ref, zb_ref):
    n_ch = x_ref.shape[0]
    e1, tr, ti = e1_ref[...], tr_ref[...], ti_ref[...]

    def stage1(c, carry):
        _dft_stage1(x_ref[c].astype(BF16), c, e1, tr, ti, zb_ref)
        return carry

    lax.fori_loop(0, n_ch, stage1, 0, unroll=8)
    xh = jnp.dot(zb_ref[...], g_ref[...], preferred_element_type=F32)
    o_ref[...] = xh.reshape(o_ref.shape).astype(o_ref.dtype)


def hyena_filter_spectrum(taps, consts):
    R = taps.shape[0]
    n1 = DFT_N1
    C = HY_CH_BLOCK
    x = taps.reshape(R, n1, n1)
    return pl.pallas_call(
        _hy_spectrum_kernel,
        grid=(R // C,),
        in_specs=[pl.BlockSpec((C, n1, n1), lambda i: (i, 0, 0)), _const_spec((2 * n1, n1)),
                  _const_spec((n1, n1)), _const_spec((n1, n1)), _const_spec((2 * n1, 2 * n1))],
        out_specs=pl.BlockSpec((C, n1, 2 * n1), lambda i: (i, 0, 0)),
        out_shape=jax.ShapeDtypeStruct((R, n1, 2 * n1), BF16),
        scratch_shapes=[pltpu.VMEM((C * n1, 2 * n1), BF16)],
        compiler_params=pltpu.CompilerParams(
            dimension_semantics=("arbitrary",), vmem_limit_bytes=VMEM_LIMIT_BYTES),
        name="hyena_filter_spectrum",
    )(x, consts["e1"].astype(BF16), consts["tr"], consts["ti"], consts["g"].astype(BF16))


def _hy_conv_kernel(cw_ref, cb_ref, sk_ref, zy_ref, zg1_ref, zg2_ref, hh_ref, e1_ref, tr_ref, ti_ref,
                    g_ref, gb_ref, e2_ref, o_ref, y_sc, zb_sc, u_sc, pad_sc, *, n_ch, d_model):
    n1 = DFT_N1
    rows = zy_ref.shape[2]
    ch0 = pl.program_id(0) * n_ch
    b_idx = lax.broadcasted_iota(jnp.int32, (rows, n1), 1)
    e1, tr, ti, e2 = e1_ref[...], tr_ref[...], ti_ref[...], e2_ref[...]

    pad = pad_sc.shape[1] - rows
    top = pad // 2
    zero_rows = jnp.zeros((top, n1), F32)

    def short_conv(z_ref, c, slot, col):
        z = z_ref[0, c]
        pad_sc[slot, 0:top, :] = zero_rows
        pad_sc[slot, top + rows:pad + rows, :] = zero_rows
        pad_sc[slot, top:top + rows, :] = z
        up = pad_sc[slot, top - 1:top - 1 + rows, :]
        down = pad_sc[slot, top + 1:top + 1 + rows, :]
        prev = pltpu.roll(jnp.where(b_idx == n1 - 1, up, z), 1, axis=1)
        nxt = pltpu.roll(jnp.where(b_idx == 0, down, z), n1 - 1, axis=1)
        n_col = 3 * d_model
        return cw_ref[col] * prev + cw_ref[n_col + col] * z + cw_ref[2 * n_col + col] * nxt + cb_ref[col]

    gate_refs = (zg1_ref, zg2_ref)

    def spectral_product(order):
        xh = jnp.dot(zb_sc[...], g_ref[...], preferred_element_type=F32)
        hh = hh_ref[order].reshape(n_ch * n1, 2 * n1).astype(F32)
        xr, xi, hr, hi = xh[:, :n1], xh[:, n1:], hh[:, :n1], hh[:, n1:]
        yh = jnp.concatenate([xr * hr - xi * hi, xr * hi + xi * hr], axis=1).astype(BF16)
        u_sc[...] = jnp.dot(yh, gb_ref[...], preferred_element_type=F32)

    def finish(c, order, slot):
        r0 = pl.multiple_of(c * n1, n1)
        ur, ui = u_sc[pl.ds(r0, n1), 0:n1], u_sc[pl.ds(r0, n1), n1:2 * n1]
        stacked = jnp.concatenate([ur * tr + ui * ti, ui * tr - ur * ti], axis=0).astype(BF16)
        conv = jnp.dot(e2, stacked, preferred_element_type=F32)
        gate = short_conv(gate_refs[order], c, slot, (order + 1) * d_model + ch0 + c)
        return gate * (conv + y_sc[c] * sk_ref[order * d_model + ch0 + c])

    n_slot = pad_sc.shape[0]

    def head(q, carry):
        for j in range(n_slot):
            c = q * n_slot + j
            y = short_conv(zy_ref, c, j, ch0 + c)
            y_sc[c] = y
            _dft_stage1(y.astype(BF16), c, e1, tr, ti, zb_sc)
        return carry

    lax.fori_loop(0, n_ch // n_slot, head, 0)
    n_grp = n_ch // n_slot
    for order in range(HY_ORDER - 1):
        spectral_product(order)

        def close_group(q, order=order):
            for j in range(n_slot):
                c = q * n_slot + j
                y_sc[c] = finish(c, order, j)

        def open_group(q):
            for j in range(n_slot):
                c = q * n_slot + j
                _dft_stage1(y_sc[c].astype(BF16), c, e1, tr, ti, zb_sc)

        close_group(0)

        def turn(q, carry):
            close_group(q)
            open_group(q - 1)
            return carry

        lax.fori_loop(1, n_grp, turn, 0)
        open_group(n_grp - 1)
    spectral_product(HY_ORDER - 1)

    def tail(q, carry):
        for j in range(n_slot):
            c = q * n_slot + j
            o_ref[0, c] = finish(c, HY_ORDER - 1, j).astype(o_ref.dtype)
        return carry

    lax.fori_loop(0, n_ch // n_slot, tail, 0)


def hyena_long_conv(z_t, spectrum, conv_w, conv_b, skip, consts):
    B, n_col, L = z_t.shape
    D = n_col // (HY_ORDER + 1)
    n1 = DFT_N1
    rows = L // n1
    assert 2 * rows == n1 and HY_ORDER == 2
    C = HY_CH_BLOCK
    nblk = D // C
    z4 = z_t.reshape(B, n_col, rows, n1)
    smem = pl.BlockSpec(memory_space=pltpu.SMEM)
    zspec = lambda off: pl.BlockSpec((1, C, rows, n1), lambda i, b: (b, i + off * nblk, 0, 0))
    kern = functools.partial(_hy_conv_kernel, n_ch=C, d_model=D)
    out = pl.pallas_call(
        kern,
        grid=(nblk, B),
        in_specs=[smem, smem, smem, zspec(0), zspec(1), zspec(2),
                  pl.BlockSpec((HY_ORDER, C, n1, 2 * n1), lambda i, b: (0, i, 0, 0)),
                  _const_spec((2 * n1, rows)), _const_spec((n1, n1)), _const_spec((n1, n1)),
                  _const_spec((2 * n1, 2 * n1)), _const_spec((2 * n1, 2 * n1)), _const_spec((rows, 2 * n1))],
        out_specs=pl.BlockSpec((1, C, rows, n1), lambda i, b: (b, i, 0, 0)),
        out_shape=jax.ShapeDtypeStruct((B, D, rows, n1), BF16),
        scratch_shapes=[pltpu.VMEM((C, rows, n1), F32),
                        pltpu.VMEM((C * n1, 2 * n1), BF16), pltpu.VMEM((C * n1, 2 * n1), F32),
                        pltpu.VMEM((HY_HEAD_UNROLL, rows + 2 * SUBLANES, n1), F32)],
        compiler_params=pltpu.CompilerParams(
            dimension_semantics=("arbitrary", "arbitrary"), vmem_limit_bytes=VMEM_LIMIT_BYTES),
        name="hyena_long_conv",
    )(conv_w.reshape(-1), conv_b.reshape(-1), skip.reshape(-1), z4, z4, z4, spectrum,
      consts["e1"][:, :rows].astype(BF16), consts["tr"], consts["ti"], consts["g"].astype(BF16),
      consts["gbar"].astype(BF16), consts["e2"][:rows].astype(BF16))
    return out.reshape(B, D, L)


def _hy_short_seq_kernel(zy_ref, zg1_ref, zg2_ref, taps_ref, par_ref, dfull_ref, dinv_ref, o_ref):
    rows, L = zy_ref.shape[1], zy_ref.shape[2]
    lane = lax.broadcasted_iota(jnp.int32, (rows, L), 1)
    par = par_ref[...]
    nfreq = dinv_ref.shape[0] // 2

    def short_conv(z, grp):
        prev = jnp.where(lane == 0, 0.0, pltpu.roll(z, 1, axis=1))
        nxt = jnp.where(lane == L - 1, 0.0, pltpu.roll(z, L - 1, axis=1))
        c = 4 * grp
        return par[:, c:c + 1] * prev + par[:, c + 1:c + 2] * z + par[:, c + 2:c + 3] * nxt + par[:, c + 3:c + 4]

    y = short_conv(zy_ref[0], 0)
    gates = (short_conv(zg1_ref[0], 1), short_conv(zg2_ref[0], 2))
    d_first = dfull_ref[0:L, :]
    for order in range(HY_ORDER):
        hh = jnp.dot(taps_ref[order].astype(BF16), dfull_ref[...], preferred_element_type=F32)
        xh = jnp.dot(y.astype(BF16), d_first, preferred_element_type=F32)
        xr, xi, hr, hi = xh[:, :nfreq], xh[:, nfreq:], hh[:, :nfreq], hh[:, nfreq:]
        yh = jnp.concatenate([xr * hr - xi * hi, xr * hi + xi * hr], axis=1).astype(BF16)
        conv = jnp.dot(yh, dinv_ref[...], preferred_element_type=F32)
        y = gates[order] * (conv + y * par[:, 12 + order:13 + order])
    o_ref[0] = y.astype(o_ref.dtype)


def hyena_short_seq_conv(z_t, taps, conv_w, conv_b, skip):
    B, n_col, L = z_t.shape
    D = n_col // (HY_ORDER + 1)
    n = 2 * L
    idx = np.arange(n, dtype=np.float64)
    ang = 2.0 * np.pi * np.outer(idx, idx) / n
    dfull = jnp.asarray(np.concatenate([np.cos(ang), -np.sin(ang)], axis=1), F32)
    dinv = jnp.asarray(np.concatenate([np.cos(ang), -np.sin(ang)], axis=0)[:, :L] / n, F32)
    cw = conv_w.reshape(HY_SHORT, HY_ORDER + 1, D)
    cb = conv_b.reshape(1, HY_ORDER + 1, D)
    par = jnp.concatenate([cw, cb], axis=0)
    par = jnp.transpose(par, (2, 1, 0)).reshape(D, 4 * (HY_ORDER + 1))
    par = jnp.concatenate([par, skip.T, jnp.zeros((D, 2), F32)], axis=1)
    rows = 256
    nblk = D // rows
    zspec = lambda off: pl.BlockSpec((1, rows, L), lambda i, b: (b, i + off * nblk, 0))
    return pl.pallas_call(
        _hy_short_seq_kernel,
        grid=(nblk, B),
        in_specs=[zspec(0), zspec(1), zspec(2),
                  pl.BlockSpec((HY_ORDER, rows, n), lambda i, b: (0, i, 0)),
                  pl.BlockSpec((rows, par.shape[1]), lambda i, b: (i, 0)),
                  _const_spec(dfull.shape), _const_spec(dinv.shape)],
        out_specs=pl.BlockSpec((1, rows, L), lambda i, b: (b, i, 0)),
        out_shape=jax.ShapeDtypeStruct((B, D, L), BF16),
        compiler_params=pltpu.CompilerParams(
            dimension_semantics=("arbitrary", "arbitrary"), vmem_limit_bytes=VMEM_LIMIT_BYTES),
        name="hyena_short_seq_conv",
    )(z_t, z_t, z_t, taps, par, dfull.astype(BF16), dinv.astype(BF16))


def hyena_layer(x, mod, norm1_g, w_in, b_in, conv_w, conv_b, f_w1, f_b1, f_w2, f_b2, f_w3, f_b3, f_w4,
                freq, bias, w_out, b_out, dft, tail):
    L = x.shape[1]
    taps = hyena_filter_taps(L, f_w1, f_b1, f_w2, f_b2, f_w3, f_b3, f_w4, freq)
    z_t = hyena_in_proj(x, norm1_g, mod, w_in.T.astype(BF16), b_in.reshape(-1, 1))
    if 2 * L == DFT_N1 * DFT_N1:
        spec = hyena_filter_spectrum(taps.reshape(HY_ORDER * D_MODEL, 2 * L), dft)
        spec = spec.reshape(HY_ORDER, D_MODEL, DFT_N1, 2 * DFT_N1)
        y_t = hyena_long_conv(z_t, spec, conv_w, conv_b, bias, dft)
    else:
        y_t = hyena_short_seq_conv(z_t, taps, conv_w, conv_b, bias)
    return layer_tail("hyena", (y_t, w_out.astype(BF16), b_out), x, mod, *tail)


def _gla_in_kernel(x_ref, g_ref, mod_ref, w_ref, wr_ref, w2_ref, gb_ref, qk_ref, v_ref, og_ref, gate_ref):
    h = _norm_mod(x_ref[0], g_ref[...], mod_ref[0, 0:1, :], mod_ref[0, 1:2, :]).astype(BF16)
    n = qk_ref.shape[2]
    qk = jnp.dot(h, w_ref[:, 0:n], preferred_element_type=F32)
    half = n // 2
    qk_ref[0, :, 0:half] = qk[:, 0:half] * (GLA_HK ** -0.5)
    qk_ref[0, :, half:n] = qk[:, half:n]
    v_ref[0] = jnp.dot(h, w_ref[:, n:2 * n], preferred_element_type=F32).astype(v_ref.dtype)
    og_ref[0] = jnp.dot(h, w_ref[:, 2 * n:3 * n], preferred_element_type=F32)
    r = jnp.dot(h, wr_ref[...], preferred_element_type=F32).astype(BF16)
    gk = jnp.dot(r, w2_ref[...], preferred_element_type=F32) + gb_ref[...]
    gate_ref[0] = -(jnp.maximum(-gk, 0.0) + jnp.log1p(jnp.exp(-jnp.abs(gk)))) * (1.0 / GLA_GATE_NORM)


def gla_in_proj(x, norm_g, mod, w_in, gk_w2, gk_b):
    B, L, D = x.shape
    tm = min(PROJ_TOKENS, L)
    per_batch = mod.shape[0] != 1
    n_main = 2 * GLA_DK + 2 * GLA_DV
    w_main = w_in[:, :n_main].astype(BF16)
    lanes = LANES
    w_r = jnp.pad(w_in[:, n_main:], ((0, 0), (0, lanes - 2 * GLA_GATE_RANK))).astype(BF16)
    w2 = jnp.zeros((lanes, 2 * GLA_DK), F32)
    w2 = w2.at[:GLA_GATE_RANK, :GLA_DK].set(gk_w2[0]).at[GLA_GATE_RANK:2 * GLA_GATE_RANK, GLA_DK:].set(gk_w2[1])
    tok = lambda n: pl.BlockSpec((1, tm, n), lambda b, i: (b, i, 0))
    n = 2 * GLA_DK
    assert GLA_DV == n
    return pl.pallas_call(
        _gla_in_kernel,
        grid=(B, L // tm),
        in_specs=[tok(D), _const_spec((1, D)),
                  pl.BlockSpec((1, 6, D), (lambda b, i: (b, 0, 0)) if per_batch else (lambda b, i: (0, 0, 0))),
                  _const_spec(w_main.shape), _const_spec(w_r.shape), _const_spec(w2.shape), _const_spec((1, n))],
        out_specs=[tok(n), tok(n), tok(n), tok(n)],
        out_shape=[jax.ShapeDtypeStruct((B, L, n), F32), jax.ShapeDtypeStruct((B, L, n), BF16),
                   jax.ShapeDtypeStruct((B, L, n), F32), jax.ShapeDtypeStruct((B, L, n), F32)],
        compiler_params=pltpu.CompilerParams(
            dimension_semantics=("arbitrary", "arbitrary"), vmem_limit_bytes=VMEM_LIMIT_BYTES),
        name="gla_in_proj",
    )(x, norm_g.reshape(1, D), mod, w_main, w_r, w2.astype(BF16), gk_b.reshape(1, n))


def _gla_scan_kernel(*refs, reverse, add_prev):
    if add_prev:
        qk_ref, v_ref, g_ref, s0_ref, prev_ref, o_ref, sfin_ref, st_ref = refs
    else:
        qk_ref, v_ref, g_ref, s0_ref, o_ref, sfin_ref, st_ref = refs
        prev_ref = None
    i = pl.program_id(1)
    C, H, dk, dv = GLA_CHUNK, GLA_HEADS, GLA_HK, GLA_HV

    @pl.when(i == 0)
    def _():
        st_ref[...] = s0_ref[0]

    r_idx = lax.broadcasted_iota(jnp.int32, (C, C), 0)
    c_idx = lax.broadcasted_iota(jnp.int32, (C, C), 1)
    keep = (r_idx <= c_idx) if reverse else (r_idx >= c_idx)
    tri = keep.astype(F32)
    n_chunks = qk_ref.shape[1] // C
    order = range(n_chunks - 1, -1, -1) if reverse else range(n_chunks)
    for ci in order:
        rows = slice(ci * C, (ci + 1) * C)
        b = jnp.dot(tri, g_ref[0, rows, :], precision=lax.Precision.HIGHEST, preferred_element_type=F32)
        b_last = b[0:1] if reverse else b[C - 1:C]
        e_pos, e_neg, e_end, dec = jnp.exp(b), jnp.exp(-b), jnp.exp(b_last - b), jnp.exp(b_last)
        for h in range(H):
            kc = slice(h * dk, (h + 1) * dk)
            vc = slice(h * dv, (h + 1) * dv)
            q = qk_ref[0, rows, kc]
            k = qk_ref[0, rows, H * dk + h * dk:H * dk + (h + 1) * dk]
            v = v_ref[0, rows, vc]
            q_t = (q * e_pos[:, kc]).astype(BF16)
            k_t = (k * e_neg[:, kc]).astype(BF16)
            k_end = (k * e_end[:, kc]).astype(BF16)
            att = lax.dot_general(q_t, k_t, (((1,), (1,)), ((), ())), preferred_element_type=F32)
            att = jnp.where(keep, att, 0.0).astype(BF16)
            st = st_ref[h]
            o = jnp.dot(att, v, preferred_element_type=F32) + lax.dot_general(
                q_t, st.astype(BF16), (((1,), (1,)), ((), ())), preferred_element_type=F32)
            st_ref[h] = st * dec[:, kc] + lax.dot_general(
                v, k_end, (((0,), (0,)), ((), ())), preferred_element_type=F32)
            if prev_ref is not None:
                o = o + prev_ref[0, rows, vc]
            o_ref[0, rows, vc] = o

    @pl.when(i == pl.num_programs(1) - 1)
    def _():
        sfin_ref[0] = st_ref[...]


def gla_scan(qk, v, gates, s0, direction, prev=None):
    B, L, _ = qk.shape
    reverse = direction == 1
    T = min(512, L)
    nT = L // T
    H, dk, dv = GLA_HEADS, GLA_HK, GLA_HV
    blk = (lambda i: nT - 1 - i) if reverse else (lambda i: i)
    tok = lambda n, col=0: pl.BlockSpec((1, T, n), lambda b, i: (b, blk(i), col))
    st_spec = pl.BlockSpec((1, H, dv, dk), lambda b, i: (b, 0, 0, 0))
    in_specs = [tok(2 * GLA_DK), tok(GLA_DV), tok(GLA_DK, direction), st_spec]
    args = [qk, v, gates, s0]
    if prev is not None:
        in_specs.append(tok(GLA_DV))
        args.append(prev)
    kern = functools.partial(_gla_scan_kernel, reverse=reverse, add_prev=prev is not None)
    return pl.pallas_call(
        kern,
        grid=(B, nT),
        in_specs=in_specs,
        out_specs=[tok(GLA_DV), st_spec],
        out_shape=[jax.ShapeDtypeStruct((B, L, GLA_DV), F32), jax.ShapeDtypeStruct((B, H, dv, dk), F32)],
        scratch_shapes=[pltpu.VMEM((H, dv, dk), F32)],
        compiler_params=pltpu.CompilerParams(
            dimension_semantics=("arbitrary", "arbitrary"), vmem_limit_bytes=VMEM_LIMIT_BYTES),
        name="gla_scan_bwd" if reverse else "gla_scan_fwd",
    )(*args)


def gla_layer(x_lat, x_ctx, mod_lat, mod_ctx, norm1_g, w_in, gk_w2, gk_b, onorm, wo, tail_lat, tail_ctx):
    B = x_lat.shape[0]
    qk_l, v_l, og_l, g_l = gla_in_proj(x_lat, norm1_g, mod_lat, w_in, gk_w2, gk_b)
    qk_c, v_c, og_c, g_c = gla_in_proj(x_ctx, norm1_g, mod_ctx, w_in, gk_w2, gk_b)
    s0 = jnp.zeros((B, GLA_HEADS, GLA_HV, GLA_HK), F32)
    oc, s_f = gla_scan(qk_c, v_c, g_c, s0, 0)
    oc, s_b = gla_scan(qk_c, v_c, g_c, s0, 1, prev=oc)
    ol, _ = gla_scan(qk_l, v_l, g_l, s_f, 0)
    ol, _ = gla_scan(qk_l, v_l, g_l, s_b, 1, prev=ol)
    wo = wo.astype(BF16)
    x_lat = layer_tail("gla", (ol, og_l, onorm, wo), x_lat, mod_lat, *tail_lat)
    if tail_ctx is not None:
        x_ctx = layer_tail("gla", (oc, og_c, onorm, wo), x_ctx, mod_ctx, *tail_ctx)
    return x_lat, x_ctx


def _rope_swap_perm():
    half = MLA_ROPE // 2
    quarter = half // 2
    p = []
    for base in (0, half):
        p += list(range(base + quarter, base + half)) + list(range(base, base + quarter))
    return np.asarray(p)


def mla_rope_tables(L, rotate):
    half = MLA_ROPE // 2
    zeros = jnp.zeros((L, MLA_ROPE), F32)
    if not rotate:
        return jnp.concatenate([jnp.ones((L, MLA_ROPE), F32), zeros], axis=1), jnp.zeros((L, 2 * MLA_ROPE), F32)
    pos = jnp.arange(L)
    inv_freq = ROPE_THETA ** (-jnp.arange(0, half, 2, dtype=F32) / half)
    ang_row = (pos // GRID_W).astype(F32)[:, None] * inv_freq[None, :]
    ang_col = (pos % GRID_W).astype(F32)[:, None] * inv_freq[None, :]
    cr, sr, cc, sc = jnp.cos(ang_row), jnp.sin(ang_row), jnp.cos(ang_col), jnp.sin(ang_col)
    cos = jnp.concatenate([cr, cr, cc, cc, zeros], axis=1)
    sin = jnp.concatenate([-sr, sr, -sc, sc, zeros], axis=1)
    return cos, sin


def _mla_qkv_kernel(x_ref, g_ref, mod_ref, wd_ref, qn_ref, wq_ref, kn_ref, wkv_ref, cos_ref, sin_ref,
                    q_ref, k_ref, v_ref):
    h = _norm_mod(x_ref[0], g_ref[...], mod_ref[0, 0:1, :], mod_ref[0, 1:2, :]).astype(BF16)
    c = jnp.dot(h, wd_ref[...], preferred_element_type=F32)
    cos, sin = cos_ref[...], sin_ref[...]
    lanes = cos.shape[1]

    def rms(a, g):
        return (a * lax.rsqrt(jnp.mean(a * a, axis=-1, keepdims=True) + NORM_EPS) * g).astype(BF16)

    def rope(tile):
        return tile * cos + pltpu.roll(tile, lanes // 2, axis=1) * sin

    cq = rms(c[:, :MLA_Q_RANK], qn_ref[...])
    ckv = rms(c[:, MLA_Q_RANK:MLA_Q_RANK + MLA_KV_RANK], kn_ref[...])
    k_rope = rope(c[:, MLA_Q_RANK + MLA_KV_RANK:]).astype(k_ref.dtype)
    q = jnp.dot(cq, wq_ref[...], preferred_element_type=F32)
    kv = jnp.dot(ckv, wkv_ref[...], preferred_element_type=F32)
    ones = jnp.ones((x_ref.shape[1], MLA_V), v_ref.dtype)
    for hd in range(MLA_HEADS):
        o = hd * MLA_QK_PAD
        q_ref[0, :, o:o + MLA_NOPE] = (q[:, o:o + MLA_NOPE] * MLA_Q_PRESCALE).astype(q_ref.dtype)
        q_ref[0, :, o + MLA_NOPE:o + MLA_QK_PAD] = (
            rope(q[:, o + MLA_NOPE:o + MLA_QK_PAD]) * MLA_Q_PRESCALE).astype(q_ref.dtype)
        k_ref[0, :, o:o + MLA_NOPE] = kv[:, o:o + MLA_NOPE].astype(k_ref.dtype)
        k_ref[0, :, o + MLA_NOPE:o + MLA_QK_PAD] = k_rope
        v_ref[0, :, 2 * hd * MLA_V:(2 * hd + 1) * MLA_V] = kv[:, o + MLA_NOPE:o + MLA_QK_PAD].astype(v_ref.dtype)
        v_ref[0, :, (2 * hd + 1) * MLA_V:(2 * hd + 2) * MLA_V] = ones


def mla_qkv_proj(x, norm_g, mod, w_down, qnorm, w_uq, kvnorm, w_ukv, rotate):
    B, L, D = x.shape
    tm = min(PROJ_TOKENS, L)
    per_batch = mod.shape[0] != 1
    perm = _rope_swap_perm()
    rope0 = MLA_Q_RANK + MLA_KV_RANK
    wd = jnp.concatenate([w_down, w_down[:, rope0:][:, perm]], axis=1).astype(BF16)
    wq = w_uq.reshape(MLA_Q_RANK, MLA_HEADS, MLA_NOPE + MLA_ROPE)
    wq = jnp.concatenate([wq, wq[:, :, MLA_NOPE:][:, :, perm]], axis=2)
    wq = wq.reshape(MLA_Q_RANK, MLA_HEADS * MLA_QK_PAD).astype(BF16)
    cos, sin = mla_rope_tables(L, rotate)
    tok = lambda n: pl.BlockSpec((1, tm, n), lambda b, i: (b, i, 0))
    nq = MLA_HEADS * MLA_QK_PAD
    nv = MLA_HEADS * 2 * MLA_V
    tab = pl.BlockSpec((tm, 2 * MLA_ROPE), lambda b, i: (i, 0))
    return pl.pallas_call(
        _mla_qkv_kernel,
        grid=(B, L // tm),
        in_specs=[tok(D), _const_spec((1, D)),
                  pl.BlockSpec((1, 6, D), (lambda b, i: (b, 0, 0)) if per_batch else (lambda b, i: (0, 0, 0))),
                  _const_spec(wd.shape), _const_spec((1, MLA_Q_RANK)), _const_spec(wq.shape),
                  _const_spec((1, MLA_KV_RANK)), _const_spec(w_ukv.shape), tab, tab],
        out_specs=[tok(nq), tok(nq), tok(nv)],
        out_shape=[jax.ShapeDtypeStruct((B, L, nq), BF16), jax.ShapeDtypeStruct((B, L, nq), BF16),
                   jax.ShapeDtypeStruct((B, L, nv), BF16)],
        compiler_params=pltpu.CompilerParams(
            dimension_semantics=("arbitrary", "arbitrary"), vmem_limit_bytes=VMEM_LIMIT_BYTES),
        name="mla_qkv_proj",
    )(x, norm_g.reshape(1, D), mod, wd, qnorm.reshape(1, -1), wq, kvnorm.reshape(1, -1),
      w_ukv.astype(BF16), cos, sin)


def mla_layer(x_lat, x_ctx, mod_lat, mod_ctx, norm1_g, w_down, qnorm, w_uq, kvnorm, w_ukv, wo, tail):
    ql, kl, vl = mla_qkv_proj(x_lat, norm1_g, mod_lat, w_down, qnorm, w_uq, kvnorm, w_ukv, True)
    _, kc, vc = mla_qkv_proj(x_ctx, norm1_g, mod_ctx, w_down, qnorm, w_uq, kvnorm, w_ukv, False)
    o = mla_attention(ql, kc, vc, kl, vl)
    return layer_tail("mla", (o, wo.astype(BF16)), x_lat, mod_lat, *tail)


def _adaln_kernel(c_ref, w_ref, b_ref, o_ref):
    cond = c_ref[...]
    s = (cond * jax.nn.sigmoid(cond)).astype(BF16)
    o_ref[0] = jnp.dot(s, w_ref[0].astype(BF16), preferred_element_type=F32) + b_ref[0]


def adaln_modulation(cond, ada_w, ada_b):
    R, D = cond.shape
    depth, _, n_out = ada_w.shape
    tn = 1024
    return pl.pallas_call(
        _adaln_kernel,
        grid=(depth, n_out // tn),
        in_specs=[_const_spec((R, D)),
                  pl.BlockSpec((1, D, tn), lambda i, j: (i, 0, j)),
                  pl.BlockSpec((1, 1, tn), lambda i, j: (i, 0, j))],
        out_specs=pl.BlockSpec((1, R, tn), lambda i, j: (i, 0, j)),
        out_shape=jax.ShapeDtypeStruct((depth, R, n_out), F32),
        compiler_params=pltpu.CompilerParams(dimension_semantics=("arbitrary", "arbitrary")),
        name="adaln_modulation",
    )(cond, ada_w, ada_b.reshape(depth, 1, n_out))


def kernel(x, c, ctx, c_ctx, ada_w, ada_b, norm1_g, norm2_g, mlp_w1, mlp_w2, final_g, hy_w_in, hy_b_in, hy_conv_w, hy_conv_b, hy_f_w1, hy_f_b1, hy_f_w2, hy_f_b2, hy_f_w3, hy_f_b3, hy_f_w4, hy_freq, hy_bias, hy_w_out, hy_b_out, gla_w_in, gla_gk_w2, gla_gk_b, gla_onorm, gla_wo, mla_w_down, mla_qnorm, mla_w_uq, mla_kvnorm, mla_w_ukv, mla_wo):
    x_lat = x
    x_ctx = ctx
    B = x.shape[0]
    cond = jnp.concatenate([c, c_ctx[None, :], jnp.zeros((SUBLANES - (B + 1) % SUBLANES, D_MODEL), F32)], axis=0)
    mod_all = adaln_modulation(cond, ada_w, ada_b)
    dft = _dft_constants()
    for i in range(DEPTH):
        kind = i % N_MIXERS
        j = i // N_MIXERS
        ctx_live = any(l % N_MIXERS != 0 for l in range(i + 1, DEPTH))
        mod_lat = mod_all[i, :B].reshape(B, 6, D_MODEL)
        mod_ctx = mod_all[i, B:B + 1].reshape(1, 6, D_MODEL)
        w1 = mlp_w1[i].astype(BF16)
        w2 = mlp_w2[i].astype(BF16)
        tail_lat = (norm2_g[i], w1, w2, final_g if i == DEPTH - 1 else None)
        tail_ctx = (norm2_g[i], w1, w2)
        if kind == 0:
            hp = (hy_w_in[j], hy_b_in[j], hy_conv_w[j], hy_conv_b[j], hy_f_w1[j], hy_f_b1[j],
                  hy_f_w2[j], hy_f_b2[j], hy_f_w3[j], hy_f_b3[j], hy_f_w4[j], hy_freq[j],
                  hy_bias[j], hy_w_out[j], hy_b_out[j])
            x_lat = hyena_layer(x_lat, mod_lat, norm1_g[i], *hp, dft, tail_lat)
            if ctx_live:
                x_ctx = hyena_layer(x_ctx, mod_ctx, norm1_g[i], *hp, dft, tail_ctx)
        elif kind == 1:
            x_lat, x_ctx = gla_layer(x_lat, x_ctx, mod_lat, mod_ctx, norm1_g[i], gla_w_in[j], gla_gk_w2[j],
                                     gla_gk_b[j], gla_onorm[j], gla_wo[j], tail_lat,
                                     tail_ctx if ctx_live else None)
        else:
            assert not ctx_live
            x_lat = mla_layer(x_lat, x_ctx, mod_lat, mod_ctx, norm1_g[i], mla_w_down[j], mla_qnorm[j],
                              mla_w_uq[j], mla_kvnorm[j], mla_w_ukv[j], mla_wo[j], tail_lat)
    return x_lat
```

```python
import functools
import math

import jax
import jax.numpy as jnp
import numpy as np
from jax import lax
from jax.experimental import pallas as pl
from jax.experimental.pallas import tpu as pltpu

F32 = jnp.float32
BF16 = jnp.bfloat16

D_MODEL = 1024
DEPTH = 4
GRID_W = 64
N_MIXERS = 3
NORM_EPS = 1e-6

HY_ORDER = 2
HY_EMB = 33
HY_SHORT = 3
HY_FAST_DECAY = 0.3
HY_SLOW_DECAY = 1.5
HY_TARGET = 1e-2

GLA_HEADS = 4
GLA_DK = D_MODEL // 2
GLA_DV = D_MODEL
GLA_HK = GLA_DK // GLA_HEADS
GLA_HV = GLA_DV // GLA_HEADS
GLA_GATE_RANK = 16
GLA_GATE_NORM = 16.0
GLA_CHUNK = 64

MLA_HEADS = 8
MLA_Q_RANK = 384
MLA_KV_RANK = 256
MLA_NOPE = 128
MLA_ROPE = 64
MLA_V = 128
ROPE_THETA = 10000.0

VMEM_LIMIT_BYTES = 56 * 1024 * 1024
MLA_QK_PAD = 256
SUBLANES = 8
LANES = 128
PROJ_TOKENS = 1024
TAIL_TOKENS = 512
MLA_Q_PRESCALE = (MLA_NOPE + MLA_ROPE) ** -0.5 * math.log2(math.e)


def _const_spec(shape):
    nd = len(shape)
    return pl.BlockSpec(shape, lambda *_: (0,) * nd, pipeline_mode=pl.Buffered(1))


def _norm_mod(x, g, shift, scale):
    y = x * lax.rsqrt(jnp.mean(x * x, axis=-1, keepdims=True) + NORM_EPS)
    return (y * g) * (1.0 + scale) + shift


def _layer_tail_kernel(*refs, kind, hidden_chunk, final_norm):
    if kind == "hyena":
        y_ref, wo_ref, bo_ref = refs[:3]
        rest = refs[3:]
        y = lax.dot_general(y_ref[0], wo_ref[...], (((0,), (0,)), ((), ())),
                            preferred_element_type=F32) + bo_ref[...]
    elif kind == "gla":
        o_ref_in, og_ref, on_ref, wo_ref = refs[:4]
        rest = refs[4:]
        og = og_ref[0]
        parts = []
        for h in range(GLA_HEADS):
            cols = slice(h * GLA_HV, (h + 1) * GLA_HV)
            o = o_ref_in[0, :, cols]
            n = o * lax.rsqrt(jnp.mean(o * o, axis=-1, keepdims=True) + NORM_EPS) * on_ref[...]
            parts.append((n * (og[:, cols] * jax.nn.sigmoid(og[:, cols]))).astype(BF16))
        y = jnp.dot(jnp.concatenate(parts, axis=1), wo_ref[...], preferred_element_type=F32)
    else:
        a_ref, wo_ref = refs[:2]
        rest = refs[2:]
        y = jnp.dot(a_ref[0], wo_ref[...], preferred_element_type=F32)
    x_ref, g_ref, mod_ref, w1_ref, w2_ref, fg_ref, out_ref = rest
    x = x_ref[0] + mod_ref[0, 2:3, :] * y
    h = _norm_mod(x, g_ref[...], mod_ref[0, 3:4, :], mod_ref[0, 4:5, :]).astype(BF16)
    hidden = w1_ref.shape[1]
    acc = jnp.zeros(x.shape, F32)
    for c0 in range(0, hidden, hidden_chunk):
        a = jnp.dot(h, w1_ref[:, c0:c0 + hidden_chunk], preferred_element_type=F32)
        a = jnp.square(jnp.maximum(a, 0.0)).astype(BF16)
        acc = acc + jnp.dot(a, w2_ref[c0:c0 + hidden_chunk, :], preferred_element_type=F32)
    out = x + mod_ref[0, 5:6, :] * acc
    if final_norm:
        out = (out * lax.rsqrt(jnp.mean(out * out, axis=-1, keepdims=True) + NORM_EPS)) * fg_ref[...]
    out_ref[0] = out


def layer_tail(kind, mixer_args, x, mod, norm_g, w1, w2, final_g=None):
    B, L, D = x.shape
    tm = min(TAIL_TOKENS, L)
    per_batch = mod.shape[0] != 1
    final_norm = final_g is not None
    fg = (final_g if final_norm else norm_g).reshape(1, D)
    tok = lambda n: pl.BlockSpec((1, tm, n), lambda b, i: (b, i, 0))
    if kind == "hyena":
        y_t, w_o, b_o = mixer_args
        m_specs = [pl.BlockSpec((1, D, tm), lambda b, i: (b, 0, i)), _const_spec(w_o.shape), _const_spec((1, D))]
        m_args = [y_t, w_o, b_o.reshape(1, D)]
    elif kind == "gla":
        o, og, onorm, w_o = mixer_args
        m_specs = [tok(D), tok(D), _const_spec((1, GLA_HV)), _const_spec(w_o.shape)]
        m_args = [o, og, onorm.reshape(1, GLA_HV), w_o]
    else:
        a, w_o = mixer_args
        m_specs = [tok(a.shape[2]), _const_spec(w_o.shape)]
        m_args = [a, w_o]
    kern = functools.partial(_layer_tail_kernel, kind=kind, hidden_chunk=1024, final_norm=final_norm)
    return pl.pallas_call(
        kern,
        grid=(B, L // tm),
        in_specs=m_specs + [
            tok(D),
            _const_spec((1, D)),
            pl.BlockSpec((1, 6, D), (lambda b, i: (b, 0, 0)) if per_batch else (lambda b, i: (0, 0, 0))),
            _const_spec(w1.shape),
            _const_spec(w2.shape),
            _const_spec((1, D)),
        ],
        out_specs=tok(D),
        out_shape=jax.ShapeDtypeStruct((B, L, D), F32),
        compiler_params=pltpu.CompilerParams(
            dimension_semantics=("arbitrary", "arbitrary"), vmem_limit_bytes=VMEM_LIMIT_BYTES),
        name="layer_tail_" + kind,
    )(*m_args, x, norm_g.reshape(1, D), mod, w1, w2, fg)


def _attn_kernel(q_ref, kc_ref, vc_ref, k_ref, v_ref, o_ref, m_ref, acc_ref, sa_ref, sb_ref, *, sub):
    q = q_ref[0]
    m_ref[...] = jnp.full(m_ref.shape, -jnp.inf, F32)
    acc_ref[...] = jnp.zeros(acc_ref.shape, F32)
    lanes = m_ref.shape[1]
    n_sub = k_ref.shape[1] // sub

    def scores(k):
        return lax.dot_general(q, k, (((1,), (1,)), ((), ())), preferred_element_type=F32)

    def lat(ref, n):
        return ref[0, pl.ds(pl.multiple_of(n * sub, sub), sub), :]

    def accumulate(s, v):
        m_prev = m_ref[...]
        m_new = jnp.maximum(m_prev, jnp.max(s, axis=-1, keepdims=True))
        alpha = jnp.exp2(m_prev - m_new)
        ps = [jnp.exp2(s[:, t:t + lanes] - m_new) for t in range(0, s.shape[1], lanes)]
        p = jnp.concatenate(ps, axis=1).astype(BF16)
        pv = jnp.dot(p, v, preferred_element_type=F32)
        for t in range(0, acc_ref.shape[1], lanes):
            acc_ref[:, t:t + lanes] = alpha * acc_ref[:, t:t + lanes] + pv[:, t:t + lanes]
        m_ref[...] = m_new

    sa_ref[...] = scores(lat(k_ref, 0))
    accumulate(scores(kc_ref[0]), vc_ref[0])

    def pair(n):
        sb_ref[...] = scores(lat(k_ref, n + 1))
        accumulate(sa_ref[...], lat(v_ref, n))

    def body(j, carry):
        n = 2 * j
        pair(n)
        sa_ref[...] = scores(lat(k_ref, n + 2))
        accumulate(sb_ref[...], lat(v_ref, n + 1))
        return carry

    lax.fori_loop(0, n_sub // 2 - 1, body, 0)
    pair(n_sub - 2)
    accumulate(sb_ref[...], lat(v_ref, n_sub - 1))
    o_ref[0] = (acc_ref[:, 0:lanes] / acc_ref[:, lanes:2 * lanes]).astype(o_ref.dtype)


def mla_attention(q, kc, vc, k, v):
    B, L, _ = q.shape
    C = kc.shape[1]
    H = MLA_HEADS
    tq = min(2048, L)
    sub = min(512, L // 4)
    assert L % (2 * sub) == 0 and L % tq == 0
    kern = functools.partial(_attn_kernel, sub=sub)
    return pl.pallas_call(
        kern,
        grid=(B, H, L // tq),
        in_specs=[
            pl.BlockSpec((1, tq, MLA_QK_PAD), lambda b, h, i: (b, i, h)),
            pl.BlockSpec((1, C, MLA_QK_PAD), lambda b, h, i: (b, 0, h)),
            pl.BlockSpec((1, C, 2 * MLA_V), lambda b, h, i: (b, 0, h)),
            pl.BlockSpec((1, L, MLA_QK_PAD), lambda b, h, i: (b, 0, h)),
            pl.BlockSpec((1, L, 2 * MLA_V), lambda b, h, i: (b, 0, h)),
        ],
        out_specs=pl.BlockSpec((1, tq, MLA_V), lambda b, h, i: (b, i, h)),
        out_shape=jax.ShapeDtypeStruct((B, L, H * MLA_V), BF16),
        scratch_shapes=[pltpu.VMEM((tq, MLA_V), F32), pltpu.VMEM((tq, 2 * MLA_V), F32),
                        pltpu.VMEM((tq, sub), F32), pltpu.VMEM((tq, sub), F32)],
        compiler_params=pltpu.CompilerParams(
            dimension_semantics=("arbitrary", "arbitrary", "arbitrary"), vmem_limit_bytes=VMEM_LIMIT_BYTES),
        name="mla_attention",
    )(q, kc, vc, k, v)


DFT_N1 = 128
HY_CH_BLOCK = 32
HY_HEAD_UNROLL = 16
HY_TAP_ROWS = 64


def _dft_constants():
    n1 = DFT_N1
    n = n1 * n1
    idx = np.arange(n1, dtype=np.float64)
    th = 2.0 * np.pi * np.outer(idx, idx) / n1
    cos1, sin1 = np.cos(th), np.sin(th)
    tw = 2.0 * np.pi * np.outer(idx, idx) / n
    fr, fi = cos1, -sin1
    c = dict(
        e1=np.concatenate([cos1, -sin1], axis=0),
        tr=np.cos(tw), ti=-np.sin(tw),
        g=np.block([[fr, fi], [-fi, fr]]),
        gbar=np.block([[fr, -fi], [fi, fr]]),
        e2=np.concatenate([cos1, -sin1], axis=1) / n,
    )
    return {k: jnp.asarray(v, F32) for k, v in c.items()}


def _hy_in_kernel(x_ref, g_ref, mod_ref, wt_ref, b_ref, o_ref, *, row_chunk):
    h = _norm_mod(x_ref[0], g_ref[...], mod_ref[0, 0:1, :], mod_ref[0, 1:2, :]).astype(BF16)
    for r0 in range(0, wt_ref.shape[0], row_chunk):
        z = lax.dot_general(wt_ref[r0:r0 + row_chunk, :], h, (((1,), (1,)), ((), ())),
                            preferred_element_type=F32)
        o_ref[0, r0:r0 + row_chunk, :] = z + b_ref[r0:r0 + row_chunk, :]


def hyena_in_proj(x, norm_g, mod, w_in_t, b_in):
    B, L, D = x.shape
    n_out = w_in_t.shape[0]
    tm = min(PROJ_TOKENS, L)
    per_batch = mod.shape[0] != 1
    return pl.pallas_call(
        functools.partial(_hy_in_kernel, row_chunk=512),
        grid=(B, L // tm),
        in_specs=[
            pl.BlockSpec((1, tm, D), lambda b, i: (b, i, 0)),
            _const_spec((1, D)),
            pl.BlockSpec((1, 6, D), (lambda b, i: (b, 0, 0)) if per_batch else (lambda b, i: (0, 0, 0))),
            _const_spec(w_in_t.shape),
            _const_spec(b_in.shape),
        ],
        out_specs=pl.BlockSpec((1, n_out, tm), lambda b, i: (b, 0, i)),
        out_shape=jax.ShapeDtypeStruct((B, n_out, L), F32),
        compiler_params=pltpu.CompilerParams(
            dimension_semantics=("arbitrary", "arbitrary"), vmem_limit_bytes=VMEM_LIMIT_BYTES),
        name="hyena_in_proj",
    )(x, norm_g.reshape(1, D), mod, w_in_t, b_in)


def _hy_hidden_kernel(zf_ref, w1_ref, b1_ref, w2_ref, b2_ref, w3_ref, b3_ref, fr_ref, o_ref):
    fr = fr_ref[...]
    h = zf_ref[...].astype(BF16)
    for w_ref, b_ref in ((w1_ref, b1_ref), (w2_ref, b2_ref), (w3_ref, b3_ref)):
        h = jnp.sin(fr * (jnp.dot(w_ref[...], h, preferred_element_type=F32) + b_ref[...]))
        out = h
        h = h.astype(BF16)
    o_ref[...] = out


def _hy_taps_kernel(hid_ref, t_ref, w4_ref, dl_ref, o_ref, *, half):
    hid = hid_ref[...].astype(BF16)
    tf = jnp.dot(w4_ref[0, 0].astype(BF16), hid[:, :half], preferred_element_type=F32)
    tb = jnp.dot(w4_ref[0, 1].astype(BF16), hid[:, half:], preferred_element_type=F32)
    taps = jnp.concatenate([tf, tb], axis=1) * jnp.exp(-t_ref[...] * dl_ref[...])
    pos = lax.broadcasted_iota(jnp.int32, taps.shape, 1)
    taps = jnp.where(pos == half, 0.0, taps)
    o_ref[0] = (taps / jnp.sum(jnp.abs(taps), axis=1, keepdims=True)).astype(o_ref.dtype)


def hyena_filter_taps(L, f_w1, f_b1, f_w2, f_b2, f_w3, f_b3, f_w4, freq):
    width = f_w1.shape[1]
    n = 2 * L
    pos = np.arange(n)
    pos = np.where(pos <= L, np.minimum(pos, L - 1), n - pos).astype(np.float64)
    bands = (HY_EMB - 1) // 2
    t_np = (pos / (L - 1))[None, :]
    w_np = (2.0 * math.pi * pos / L)[None, :]
    f_np = np.linspace(1e-4, bands - 1, bands)[:, None]
    zf_np = np.concatenate([t_np, np.cos(f_np * w_np), -np.sin(f_np * w_np),
                            np.zeros((width - HY_EMB, n))], axis=0)
    t = jnp.asarray(t_np, F32)
    zf = jnp.asarray(zf_np, F32)
    w1t = jnp.pad(f_w1.T, ((0, 0), (0, width - HY_EMB))).astype(BF16)
    col = lambda v: v.reshape(width, 1).astype(F32)
    lane_blk = min(2048, n)
    hidden = pl.pallas_call(
        _hy_hidden_kernel,
        grid=(n // lane_blk,),
        in_specs=[pl.BlockSpec((width, lane_blk), lambda i: (0, i))] + [_const_spec((width, width)), _const_spec((width, 1))] * 3
        + [_const_spec((width, 1))],
        out_specs=pl.BlockSpec((width, lane_blk), lambda i: (0, i)),
        out_shape=jax.ShapeDtypeStruct((width, n), F32),
        name="hyena_filter_hidden",
    )(zf, w1t, col(f_b1), f_w2.T.astype(BF16), col(f_b2), f_w3.T.astype(BF16), col(f_b3), col(freq))
    max_decay = math.log(HY_TARGET) / HY_FAST_DECAY
    min_decay = math.log(HY_TARGET) / HY_SLOW_DECAY
    deltas = jnp.abs(jnp.linspace(min_decay, max_decay, D_MODEL, dtype=F32)).reshape(D_MODEL, 1)
    w4t = f_w4.T.reshape(HY_ORDER, 2, D_MODEL, width)
    rows = HY_TAP_ROWS
    return pl.pallas_call(
        functools.partial(_hy_taps_kernel, half=L),
        grid=(HY_ORDER, D_MODEL // rows),
        in_specs=[
            _const_spec((width, n)),
            _const_spec((1, n)),
            pl.BlockSpec((1, 2, rows, width), lambda o, i: (o, 0, i, 0)),
            pl.BlockSpec((rows, 1), lambda o, i: (i, 0)),
        ],
        out_specs=pl.BlockSpec((1, rows, n), lambda o, i: (o, i, 0)),
        out_shape=jax.ShapeDtypeStruct((HY_ORDER, D_MODEL, n), BF16),
        compiler_params=pltpu.CompilerParams(
            dimension_semantics=("arbitrary", "arbitrary"), vmem_limit_bytes=VMEM_LIMIT_BYTES),
        name="hyena_filter_taps",
    )(hidden, t, w4t, deltas)


def _dft_stage1(x, c, e1, tr, ti, zb_ref):
    n1 = DFT_N1
    z = jnp.dot(e1, x, preferred_element_type=F32)
    zr, zi = z[:n1], z[n1:]
    r0 = pl.multiple_of(c * n1, n1)
    zb_ref[pl.ds(r0, n1), 0:n1] = (zr * tr - zi * ti).astype(BF16)
    zb_ref[pl.ds(r0, n1), n1:2 * n1] = (zr * ti + zi * tr).astype(BF16)


def _hy_spectrum_kernel(x_ref, e1_ref, tr_ref, ti_ref, g_ref, o_ref, zb_ref):
    n_ch = x_ref.shape[0]
    e1, tr, ti = e1_ref[...], tr_ref[...], ti_ref[...]

    def stage1(c, carry):
        _dft_stage1(x_ref[c].astype(BF16), c, e1, tr, ti, zb_ref)
        return carry

    lax.fori_loop(0, n_ch, stage1, 0, unroll=8)
    xh = jnp.dot(zb_ref[...], g_ref[...], preferred_element_type=F32)
    o_ref[...] = xh.reshape(o_ref.shape).astype(o_ref.dtype)


def hyena_filter_spectrum(taps, consts):
    R = taps.shape[0]
    n1 = DFT_N1
    C = HY_CH_BLOCK
    x = taps.reshape(R, n1, n1)
    return pl.pallas_call(
        _hy_spectrum_kernel,
        grid=(R // C,),
        in_specs=[pl.BlockSpec((C, n1, n1), lambda i: (i, 0, 0)), _const_spec((2 * n1, n1)),
                  _const_spec((n1, n1)), _const_spec((n1, n1)), _const_spec((2 * n1, 2 * n1))],
        out_specs=pl.BlockSpec((C, n1, 2 * n1), lambda i: (i, 0, 0)),
        out_shape=jax.ShapeDtypeStruct((R, n1, 2 * n1), BF16),
        scratch_shapes=[pltpu.VMEM((C * n1, 2 * n1), BF16)],
        compiler_params=pltpu.CompilerParams(
            dimension_semantics=("arbitrary",), vmem_limit_bytes=VMEM_LIMIT_BYTES),
        name="hyena_filter_spectrum",
    )(x, consts["e1"].astype(BF16), consts["tr"], consts["ti"], consts["g"].astype(BF16))


def _hy_conv_kernel(cw_ref, cb_ref, sk_ref, zy_ref, zg1_ref, zg2_ref, hh_ref, e1_ref, tr_ref, ti_ref,
                    g_ref, gb_ref, e2_ref, o_ref, y_sc, zb_sc, u_sc, pad_sc, *, n_ch, d_model):
    n1 = DFT_N1
    rows = zy_ref.shape[2]
    ch0 = pl.program_id(0) * n_ch
    b_idx = lax.broadcasted_iota(jnp.int32, (rows, n1), 1)
    e1, tr, ti, e2 = e1_ref[...], tr_ref[...], ti_ref[...], e2_ref[...]

    pad = pad_sc.shape[1] - rows
    top = pad // 2
    zero_rows = jnp.zeros((top, n1), F32)

    def short_conv(z_ref, c, slot, col):
        z = z_ref[0, c]
        pad_sc[slot, 0:top, :] = zero_rows
        pad_sc[slot, top + rows:pad + rows, :] = zero_rows
        pad_sc[slot, top:top + rows, :] = z
        up = pad_sc[slot, top - 1:top - 1 + rows, :]
        down = pad_sc[slot, top + 1:top + 1 + rows, :]
        prev = pltpu.roll(jnp.where(b_idx == n1 - 1, up, z), 1, axis=1)
        nxt = pltpu.roll(jnp.where(b_idx == 0, down, z), n1 - 1, axis=1)
        n_col = 3 * d_model
        return cw_ref[col] * prev + cw_ref[n_col + col] * z + cw_ref[2 * n_col + col] * nxt + cb_ref[col]

    gate_refs = (zg1_ref, zg2_ref)

    def spectral_product(order):
        xh = jnp.dot(zb_sc[...], g_ref[...], preferred_element_type=F32)
        hh = hh_ref[order].reshape(n_ch * n1, 2 * n1).astype(F32)
        xr, xi, hr, hi = xh[:, :n1], xh[:, n1:], hh[:, :n1], hh[:, n1:]
        yh = jnp.concatenate([xr * hr - xi * hi, xr * hi + xi * hr], axis=1).astype(BF16)
        u_sc[...] = jnp.dot(yh, gb_ref[...], preferred_element_type=F32)

    def finish(c, order, slot):
        r0 = pl.multiple_of(c * n1, n1)
        ur, ui = u_sc[pl.ds(r0, n1), 0:n1], u_sc[pl.ds(r0, n1), n1:2 * n1]
        stacked = jnp.concatenate([ur * tr + ui * ti, ui * tr - ur * ti], axis=0).astype(BF16)
        conv = jnp.dot(e2, stacked, preferred_element_type=F32)
        gate = short_conv(gate_refs[order], c, slot, (order + 1) * d_model + ch0 + c)
        return gate * (conv + y_sc[c] * sk_ref[order * d_model + ch0 + c])

    n_slot = pad_sc.shape[0]

    def head(q, carry):
        for j in range(n_slot):
            c = q * n_slot + j
            y = short_conv(zy_ref, c, j, ch0 + c)
            y_sc[c] = y
            _dft_stage1(y.astype(BF16), c, e1, tr, ti, zb_sc)
        return carry

    lax.fori_loop(0, n_ch // n_slot, head, 0)
    n_grp = n_ch // n_slot
    for order in range(HY_ORDER - 1):
        spectral_product(order)

        def close_group(q, order=order):
            for j in range(n_slot):
                c = q * n_slot + j
                y_sc[c] = finish(c, order, j)

        def open_group(q):
            for j in range(n_slot):
                c = q * n_slot + j
                _dft_stage1(y_sc[c].astype(BF16), c, e1, tr, ti, zb_sc)

        close_group(0)

        def turn(q, carry):
            close_group(q)
            open_group(q - 1)
            return carry

        lax.fori_loop(1, n_grp, turn, 0)
        open_group(n_grp - 1)
    spectral_product(HY_ORDER - 1)

    def tail(q, carry):
        for j in range(n_slot):
            c = q * n_slot + j
            o_ref[0, c] = finish(c, HY_ORDER - 1, j).astype(o_ref.dtype)
        return carry

    lax.fori_loop(0, n_ch // n_slot, tail, 0)


def hyena_long_conv(z_t, spectrum, conv_w, conv_b, skip, consts):
    B, n_col, L = z_t.shape
    D = n_col // (HY_ORDER + 1)
    n1 = DFT_N1
    rows = L // n1
    assert 2 * rows == n1 and HY_ORDER == 2
    C = HY_CH_BLOCK
    nblk = D // C
    z4 = z_t.reshape(B, n_col, rows, n1)
    smem = pl.BlockSpec(memory_space=pltpu.SMEM)
    zspec = lambda off: pl.BlockSpec((1, C, rows, n1), lambda i, b: (b, i + off * nblk, 0, 0))
    kern = functools.partial(_hy_conv_kernel, n_ch=C, d_model=D)
    out = pl.pallas_call(
        kern,
        grid=(nblk, B),
        in_specs=[smem, smem, smem, zspec(0), zspec(1), zspec(2),
                  pl.BlockSpec((HY_ORDER, C, n1, 2 * n1), lambda i, b: (0, i, 0, 0)),
                  _const_spec((2 * n1, rows)), _const_spec((n1, n1)), _const_spec((n1, n1)),
                  _const_spec((2 * n1, 2 * n1)), _const_spec((2 * n1, 2 * n1)), _const_spec((rows, 2 * n1))],
        out_specs=pl.BlockSpec((1, C, rows, n1), lambda i, b: (b, i, 0, 0)),
        out_shape=jax.ShapeDtypeStruct((B, D, rows, n1), BF16),
        scratch_shapes=[pltpu.VMEM((C, rows, n1), F32),
                        pltpu.VMEM((C * n1, 2 * n1), BF16), pltpu.VMEM((C * n1, 2 * n1), F32),
                        pltpu.VMEM((HY_HEAD_UNROLL, rows + 2 * SUBLANES, n1), F32)],
        compiler_params=pltpu.CompilerParams(
            dimension_semantics=("arbitrary", "arbitrary"), vmem_limit_bytes=VMEM_LIMIT_BYTES),
        name="hyena_long_conv",
    )(conv_w.reshape(-1), conv_b.reshape(-1), skip.reshape(-1), z4, z4, z4, spectrum,
      consts["e1"][:, :rows].astype(BF16), consts["tr"], consts["ti"], consts["g"].astype(BF16),
      consts["gbar"].astype(BF16), consts["e2"][:rows].astype(BF16))
    return out.reshape(B, D, L)


def _hy_short_seq_kernel(zy_ref, zg1_ref, zg2_ref, taps_ref, par_ref, dfull_ref, dinv_ref, o_ref):
    rows, L = zy_ref.shape[1], zy_ref.shape[2]
    lane = lax.broadcasted_iota(jnp.int32, (rows, L), 1)
    par = par_ref[...]
    nfreq = dinv_ref.shape[0] // 2

    def short_conv(z, grp):
        prev = jnp.where(lane == 0, 0.0, pltpu.roll(z, 1, axis=1))
        nxt = jnp.where(lane == L - 1, 0.0, pltpu.roll(z, L - 1, axis=1))
        c = 4 * grp
        return par[:, c:c + 1] * prev + par[:, c + 1:c + 2] * z + par[:, c + 2:c + 3] * nxt + par[:, c + 3:c + 4]

    y = short_conv(zy_ref[0], 0)
    gates = (short_conv(zg1_ref[0], 1), short_conv(zg2_ref[0], 2))
    d_first = dfull_ref[0:L, :]
    for order in range(HY_ORDER):
        hh = jnp.dot(taps_ref[order].astype(BF16), dfull_ref[...], preferred_element_type=F32)
        xh = jnp.dot(y.astype(BF16), d_first, preferred_element_type=F32)
        xr, xi, hr, hi = xh[:, :nfreq], xh[:, nfreq:], hh[:, :nfreq], hh[:, nfreq:]
        yh = jnp.concatenate([xr * hr - xi * hi, xr * hi + xi * hr], axis=1).astype(BF16)
        conv = jnp.dot(yh, dinv_ref[...], preferred_element_type=F32)
        y = gates[order] * (conv + y * par[:, 12 + order:13 + order])
    o_ref[0] = y.astype(o_ref.dtype)


def hyena_short_seq_conv(z_t, taps, conv_w, conv_b, skip):
    B, n_col, L = z_t.shape
    D = n_col // (HY_ORDER + 1)
    n = 2 * L
    idx = np.arange(n, dtype=np.float64)
    ang = 2.0 * np.pi * np.outer(idx, idx) / n
    dfull = jnp.asarray(np.concatenate([np.cos(ang), -np.sin(ang)], axis=1), F32)
    dinv = jnp.asarray(np.concatenate([np.cos(ang), -np.sin(ang)], axis=0)[:, :L] / n, F32)
    cw = conv_w.reshape(HY_SHORT, HY_ORDER + 1, D)
    cb = conv_b.reshape(1, HY_ORDER + 1, D)
    par = jnp.concatenate([cw, cb], axis=0)
    par = jnp.transpose(par, (2, 1, 0)).reshape(D, 4 * (HY_ORDER + 1))
    par = jnp.concatenate([par, skip.T, jnp.zeros((D, 2), F32)], axis=1)
    rows = 256
    nblk = D // rows
    zspec = lambda off: pl.BlockSpec((1, rows, L), lambda i, b: (b, i + off * nblk, 0))
    return pl.pallas_call(
        _hy_short_seq_kernel,
        grid=(nblk, B),
        in_specs=[zspec(0), zspec(1), zspec(2),
                  pl.BlockSpec((HY_ORDER, rows, n), lambda i, b: (0, i, 0)),
                  pl.BlockSpec((rows, par.shape[1]), lambda i, b: (i, 0)),
                  _const_spec(dfull.shape), _const_spec(dinv.shape)],
        out_specs=pl.BlockSpec((1, rows, L), lambda i, b: (b, i, 0)),
        out_shape=jax.ShapeDtypeStruct((B, D, L), BF16),
        compiler_params=pltpu.CompilerParams(
            dimension_semantics=("arbitrary", "arbitrary"), vmem_limit_bytes=VMEM_LIMIT_BYTES),
        name="hyena_short_seq_conv",
    )(z_t, z_t, z_t, taps, par, dfull.astype(BF16), dinv.astype(BF16))


def hyena_layer(x, mod, norm1_g, w_in, b_in, conv_w, conv_b, f_w1, f_b1, f_w2, f_b2, f_w3, f_b3, f_w4,
                freq, bias, w_out, b_out, dft, tail):
    L = x.shape[1]
    taps = hyena_filter_taps(L, f_w1, f_b1, f_w2, f_b2, f_w3, f_b3, f_w4, freq)
    z_t = hyena_in_proj(x, norm1_g, mod, w_in.T.astype(BF16), b_in.reshape(-1, 1))
    if 2 * L == DFT_N1 * DFT_N1:
        spec = hyena_filter_spectrum(taps.reshape(HY_ORDER * D_MODEL, 2 * L), dft)
        spec = spec.reshape(HY_ORDER, D_MODEL, DFT_N1, 2 * DFT_N1)
        y_t = hyena_long_conv(z_t, spec, conv_w, conv_b, bias, dft)
    else:
        y_t = hyena_short_seq_conv(z_t, taps, conv_w, conv_b, bias)
    return layer_tail("hyena", (y_t, w_out.astype(BF16), b_out), x, mod, *tail)


def _gla_in_kernel(x_ref, g_ref, mod_ref, w_ref, wr_ref, w2_ref, gb_ref, qk_ref, v_ref, og_ref, gate_ref):
    h = _norm_mod(x_ref[0], g_ref[...], mod_ref[0, 0:1, :], mod_ref[0, 1:2, :]).astype(BF16)
    n = qk_ref.shape[2]
    qk = jnp.dot(h, w_ref[:, 0:n], preferred_element_type=F32)
    half = n // 2
    qk_ref[0, :, 0:half] = qk[:, 0:half] * (GLA_HK ** -0.5)
    qk_ref[0, :, half:n] = qk[:, half:n]
    v_ref[0] = jnp.dot(h, w_ref[:, n:2 * n], preferred_element_type=F32).astype(v_ref.dtype)
    og_ref[0] = jnp.dot(h, w_ref[:, 2 * n:3 * n], preferred_element_type=F32)
    r = jnp.dot(h, wr_ref[...], preferred_element_type=F32).astype(BF16)
    gk = jnp.dot(r, w2_ref[...], preferred_element_type=F32) + gb_ref[...]
    gate_ref[0] = -(jnp.maximum(-gk, 0.0) + jnp.log1p(jnp.exp(-jnp.abs(gk)))) * (1.0 / GLA_GATE_NORM)


def gla_in_proj(x, norm_g, mod, w_in, gk_w2, gk_b):
    B, L, D = x.shape
    tm = min(PROJ_TOKENS, L)
    per_batch = mod.shape[0] != 1
    n_main = 2 * GLA_DK + 2 * GLA_DV
    w_main = w_in[:, :n_main].astype(BF16)
    lanes = LANES
    w_r = jnp.pad(w_in[:, n_main:], ((0, 0), (0, lanes - 2 * GLA_GATE_RANK))).astype(BF16)
    w2 = jnp.zeros((lanes, 2 * GLA_DK), F32)
    w2 = w2.at[:GLA_GATE_RANK, :GLA_DK].set(gk_w2[0]).at[GLA_GATE_RANK:2 * GLA_GATE_RANK, GLA_DK:].set(gk_w2[1])
    tok = lambda n: pl.BlockSpec((1, tm, n), lambda b, i: (b, i, 0))
    n = 2 * GLA_DK
    assert GLA_DV == n
    return pl.pallas_call(
        _gla_in_kernel,
        grid=(B, L // tm),
        in_specs=[tok(D), _const_spec((1, D)),
                  pl.BlockSpec((1, 6, D), (lambda b, i: (b, 0, 0)) if per_batch else (lambda b, i: (0, 0, 0))),
                  _const_spec(w_main.shape), _const_spec(w_r.shape), _const_spec(w2.shape), _const_spec((1, n))],
        out_specs=[tok(n), tok(n), tok(n), tok(n)],
        out_shape=[jax.ShapeDtypeStruct((B, L, n), F32), jax.ShapeDtypeStruct((B, L, n), BF16),
                   jax.ShapeDtypeStruct((B, L, n), F32), jax.ShapeDtypeStruct((B, L, n), F32)],
        compiler_params=pltpu.CompilerParams(
            dimension_semantics=("arbitrary", "arbitrary"), vmem_limit_bytes=VMEM_LIMIT_BYTES),
        name="gla_in_proj",
    )(x, norm_g.reshape(1, D), mod, w_main, w_r, w2.astype(BF16), gk_b.reshape(1, n))


def _gla_scan_kernel(*refs, reverse, add_prev):
    if add_prev:
        qk_ref, v_ref, g_ref, s0_ref, prev_ref, o_ref, sfin_ref, st_ref = refs
    else:
        qk_ref, v_ref, g_ref, s0_ref, o_ref, sfin_ref, st_ref = refs
        prev_ref = None
    i = pl.program_id(1)
    C, H, dk, dv = GLA_CHUNK, GLA_HEADS, GLA_HK, GLA_HV

    @pl.when(i == 0)
    def _():
        st_ref[...] = s0_ref[0]

    r_idx = lax.broadcasted_iota(jnp.int32, (C, C), 0)
    c_idx = lax.broadcasted_iota(jnp.int32, (C, C), 1)
    keep = (r_idx <= c_idx) if reverse else (r_idx >= c_idx)
    tri = keep.astype(F32)
    n_chunks = qk_ref.shape[1] // C
    order = range(n_chunks - 1, -1, -1) if reverse else range(n_chunks)
    for ci in order:
        rows = slice(ci * C, (ci + 1) * C)
        b = jnp.dot(tri, g_ref[0, rows, :], precision=lax.Precision.HIGHEST, preferred_element_type=F32)
        b_last = b[0:1] if reverse else b[C - 1:C]
        e_pos, e_neg, e_end, dec = jnp.exp(b), jnp.exp(-b), jnp.exp(b_last - b), jnp.exp(b_last)
        for h in range(H):
            kc = slice(h * dk, (h + 1) * dk)
            vc = slice(h * dv, (h + 1) * dv)
            q = qk_ref[0, rows, kc]
            k = qk_ref[0, rows, H * dk + h * dk:H * dk + (h + 1) * dk]
            v = v_ref[0, rows, vc]
            q_t = (q * e_pos[:, kc]).astype(BF16)
            k_t = (k * e_neg[:, kc]).astype(BF16)
            k_end = (k * e_end[:, kc]).astype(BF16)
            att = lax.dot_general(q_t, k_t, (((1,), (1,)), ((), ())), preferred_element_type=F32)
            att = jnp.where(keep, att, 0.0).astype(BF16)
            st = st_ref[h]
            o = jnp.dot(att, v, preferred_element_type=F32) + lax.dot_general(
                q_t, st.astype(BF16), (((1,), (1,)), ((), ())), preferred_element_type=F32)
            st_ref[h] = st * dec[:, kc] + lax.dot_general(
                v, k_end, (((0,), (0,)), ((), ())), preferred_element_type=F32)
            if prev_ref is not None:
                o = o + prev_ref[0, rows, vc]
            o_ref[0, rows, vc] = o

    @pl.when(i == pl.num_programs(1) - 1)
    def _():
        sfin_ref[0] = st_ref[...]


def gla_scan(qk, v, gates, s0, direction, prev=None):
    B, L, _ = qk.shape
    reverse = direction == 1
    T = min(512, L)
    nT = L // T
    H, dk, dv = GLA_HEADS, GLA_HK, GLA_HV
    blk = (lambda i: nT - 1 - i) if reverse else (lambda i: i)
    tok = lambda n, col=0: pl.BlockSpec((1, T, n), lambda b, i: (b, blk(i), col))
    st_spec = pl.BlockSpec((1, H, dv, dk), lambda b, i: (b, 0, 0, 0))
    in_specs = [tok(2 * GLA_DK), tok(GLA_DV), tok(GLA_DK, direction), st_spec]
    args = [qk, v, gates, s0]
    if prev is not None:
        in_specs.append(tok(GLA_DV))
        args.append(prev)
    kern = functools.partial(_gla_scan_kernel, reverse=reverse, add_prev=prev is not None)
    return pl.pallas_call(
        kern,
        grid=(B, nT),
        in_specs=in_specs,
        out_specs=[tok(GLA_DV), st_spec],
        out_shape=[jax.ShapeDtypeStruct((B, L, GLA_DV), F32), jax.ShapeDtypeStruct((B, H, dv, dk), F32)],
        scratch_shapes=[pltpu.VMEM((H, dv, dk), F32)],
        compiler_params=pltpu.CompilerParams(
            dimension_semantics=("arbitrary", "arbitrary"), vmem_limit_bytes=VMEM_LIMIT_BYTES),
        name="gla_scan_bwd" if reverse else "gla_scan_fwd",
    )(*args)


def gla_layer(x_lat, x_ctx, mod_lat, mod_ctx, norm1_g, w_in, gk_w2, gk_b, onorm, wo, tail_lat, tail_ctx):
    B = x_lat.shape[0]
    qk_l, v_l, og_l, g_l = gla_in_proj(x_lat, norm1_g, mod_lat, w_in, gk_w2, gk_b)
    qk_c, v_c, og_c, g_c = gla_in_proj(x_ctx, norm1_g, mod_ctx, w_in, gk_w2, gk_b)
    s0 = jnp.zeros((B, GLA_HEADS, GLA_HV, GLA_HK), F32)
    oc, s_f = gla_scan(qk_c, v_c, g_c, s0, 0)
    oc, s_b = gla_scan(qk_c, v_c, g_c, s0, 1, prev=oc)
    ol, _ = gla_scan(qk_l, v_l, g_l, s_f, 0)
    ol, _ = gla_scan(qk_l, v_l, g_l, s_b, 1, prev=ol)
    wo = wo.astype(BF16)
    x_lat = layer_tail("gla", (ol, og_l, onorm, wo), x_lat, mod_lat, *tail_lat)
    if tail_ctx is not None:
        x_ctx = layer_tail("gla", (oc, og_c, onorm, wo), x_ctx, mod_ctx, *tail_ctx)
    return x_lat, x_ctx


def _rope_swap_perm():
    half = MLA_ROPE // 2
    quarter = half // 2
    p = []
    for base in (0, half):
        p += list(range(base + quarter, base + half)) + list(range(base, base + quarter))
    return np.asarray(p)


def mla_rope_tables(L, rotate):
    half = MLA_ROPE // 2
    zeros = jnp.zeros((L, MLA_ROPE), F32)
    if not rotate:
        return jnp.concatenate([jnp.ones((L, MLA_ROPE), F32), zeros], axis=1), jnp.zeros((L, 2 * MLA_ROPE), F32)
    pos = jnp.arange(L)
    inv_freq = ROPE_THETA ** (-jnp.arange(0, half, 2, dtype=F32) / half)
    ang_row = (pos // GRID_W).astype(F32)[:, None] * inv_freq[None, :]
    ang_col = (pos % GRID_W).astype(F32)[:, None] * inv_freq[None, :]
    cr, sr, cc, sc = jnp.cos(ang_row), jnp.sin(ang_row), jnp.cos(ang_col), jnp.sin(ang_col)
    cos = jnp.concatenate([cr, cr, cc, cc, zeros], axis=1)
    sin = jnp.concatenate([-sr, sr, -sc, sc, zeros], axis=1)
    return cos, sin


def _mla_qkv_kernel(x_ref, g_ref, mod_ref, wd_ref, qn_ref, wq_ref, kn_ref, wkv_ref, cos_ref, sin_ref,
                    q_ref, k_ref, v_ref):
    h = _norm_mod(x_ref[0], g_ref[...], mod_ref[0, 0:1, :], mod_ref[0, 1:2, :]).astype(BF16)
    c = jnp.dot(h, wd_ref[...], preferred_element_type=F32)
    cos, sin = cos_ref[...], sin_ref[...]
    lanes = cos.shape[1]

    def rms(a, g):
        return (a * lax.rsqrt(jnp.mean(a * a, axis=-1, keepdims=True) + NORM_EPS) * g).astype(BF16)

    def rope(tile):
        return tile * cos + pltpu.roll(tile, lanes // 2, axis=1) * sin

    cq = rms(c[:, :MLA_Q_RANK], qn_ref[...])
    ckv = rms(c[:, MLA_Q_RANK:MLA_Q_RANK + MLA_KV_RANK], kn_ref[...])
    k_rope = rope(c[:, MLA_Q_RANK + MLA_KV_RANK:]).astype(k_ref.dtype)
    q = jnp.dot(cq, wq_ref[...], preferred_element_type=F32)
    kv = jnp.dot(ckv, wkv_ref[...], preferred_element_type=F32)
    ones = jnp.ones((x_ref.shape[1], MLA_V), v_ref.dtype)
    for hd in range(MLA_HEADS):
        o = hd * MLA_QK_PAD
        q_ref[0, :, o:o + MLA_NOPE] = (q[:, o:o + MLA_NOPE] * MLA_Q_PRESCALE).astype(q_ref.dtype)
        q_ref[0, :, o + MLA_NOPE:o + MLA_QK_PAD] = (
            rope(q[:, o + MLA_NOPE:o + MLA_QK_PAD]) * MLA_Q_PRESCALE).astype(q_ref.dtype)
        k_ref[0, :, o:o + MLA_NOPE] = kv[:, o:o + MLA_NOPE].astype(k_ref.dtype)
        k_ref[0, :, o + MLA_NOPE:o + MLA_QK_PAD] = k_rope
        v_ref[0, :, 2 * hd * MLA_V:(2 * hd + 1) * MLA_V] = kv[:, o + MLA_NOPE:o + MLA_QK_PAD].astype(v_ref.dtype)
        v_ref[0, :, (2 * hd + 1) * MLA_V:(2 * hd + 2) * MLA_V] = ones


def mla_qkv_proj(x, norm_g, mod, w_down, qnorm, w_uq, kvnorm, w_ukv, rotate):
    B, L, D = x.shape
    tm = min(PROJ_TOKENS, L)
    per_batch = mod.shape[0] != 1
    perm = _rope_swap_perm()
    rope0 = MLA_Q_RANK + MLA_KV_RANK
    wd = jnp.concatenate([w_down, w_down[:, rope0:][:, perm]], axis=1).astype(BF16)
    wq = w_uq.reshape(MLA_Q_RANK, MLA_HEADS, MLA_NOPE + MLA_ROPE)
    wq = jnp.concatenate([wq, wq[:, :, MLA_NOPE:][:, :, perm]], axis=2)
    wq = wq.reshape(MLA_Q_RANK, MLA_HEADS * MLA_QK_PAD).astype(BF16)
    cos, sin = mla_rope_tables(L, rotate)
    tok = lambda n: pl.BlockSpec((1, tm, n), lambda b, i: (b, i, 0))
    nq = MLA_HEADS * MLA_QK_PAD
    nv = MLA_HEADS * 2 * MLA_V
    tab = pl.BlockSpec((tm, 2 * MLA_ROPE), lambda b, i: (i, 0))
    return pl.pallas_call(
        _mla_qkv_kernel,
        grid=(B, L // tm),
        in_specs=[tok(D), _const_spec((1, D)),
                  pl.BlockSpec((1, 6, D), (lambda b, i: (b, 0, 0)) if per_batch else (lambda b, i: (0, 0, 0))),
                  _const_spec(wd.shape), _const_spec((1, MLA_Q_RANK)), _const_spec(wq.shape),
                  _const_spec((1, MLA_KV_RANK)), _const_spec(w_ukv.shape), tab, tab],
        out_specs=[tok(nq), tok(nq), tok(nv)],
        out_shape=[jax.ShapeDtypeStruct((B, L, nq), BF16), jax.ShapeDtypeStruct((B, L, nq), BF16),
                   jax.ShapeDtypeStruct((B, L, nv), BF16)],
        compiler_params=pltpu.CompilerParams(
            dimension_semantics=("arbitrary", "arbitrary"), vmem_limit_bytes=VMEM_LIMIT_BYTES),
        name="mla_qkv_proj",
    )(x, norm_g.reshape(1, D), mod, wd, qnorm.reshape(1, -1), wq, kvnorm.reshape(1, -1),
      w_ukv.astype(BF16), cos, sin)


def mla_layer(x_lat, x_ctx, mod_lat, mod_ctx, norm1_g, w_down, qnorm, w_uq, kvnorm, w_ukv, wo, tail):
    ql, kl, vl = mla_qkv_proj(x_lat, norm1_g, mod_lat, w_down, qnorm, w_uq, kvnorm, w_ukv, True)
    _, kc, vc = mla_qkv_proj(x_ctx, norm1_g, mod_ctx, w_down, qnorm, w_uq, kvnorm, w_ukv, False)
    o = mla_attention(ql, kc, vc, kl, vl)
    return layer_tail("mla", (o, wo.astype(BF16)), x_lat, mod_lat, *tail)


def _adaln_kernel(c_ref, w_ref, b_ref, o_ref):
    cond = c_ref[...]
    s = (cond * jax.nn.sigmoid(cond)).astype(BF16)
    o_ref[0] = jnp.dot(s, w_ref[0].astype(BF16), preferred_element_type=F32) + b_ref[0]


def adaln_modulation(cond, ada_w, ada_b):
    R, D = cond.shape
    depth, _, n_out = ada_w.shape
    tn = 1024
    return pl.pallas_call(
        _adaln_kernel,
        grid=(depth, n_out // tn),
        in_specs=[_const_spec((R, D)),
                  pl.BlockSpec((1, D, tn), lambda i, j: (i, 0, j)),
                  pl.BlockSpec((1, 1, tn), lambda i, j: (i, 0, j))],
        out_specs=pl.BlockSpec((1, R, tn), lambda i, j: (i, 0, j)),
        out_shape=jax.ShapeDtypeStruct((depth, R, n_out), F32),
        compiler_params=pltpu.CompilerParams(dimension_semantics=("arbitrary", "arbitrary")),
        name="adaln_modulation",
    )(cond, ada_w, ada_b.reshape(depth, 1, n_out))


def kernel(x, c, ctx, c_ctx, ada_w, ada_b, norm1_g, norm2_g, mlp_w1, mlp_w2, final_g, hy_w_in, hy_b_in, hy_conv_w, hy_conv_b, hy_f_w1, hy_f_b1, hy_f_w2, hy_f_b2, hy_f_w3, hy_f_b3, hy_f_w4, hy_freq, hy_bias, hy_w_out, hy_b_out, gla_w_in, gla_gk_w2, gla_gk_b, gla_onorm, gla_wo, mla_w_down, mla_qnorm, mla_w_uq, mla_kvnorm, mla_w_ukv, mla_wo):
    x_lat = x
    x_ctx = ctx
    B = x.shape[0]
    cond = jnp.concatenate([c, c_ctx[None, :], jnp.zeros((SUBLANES - (B + 1) % SUBLANES, D_MODEL), F32)], axis=0)
    mod_all = adaln_modulation(cond, ada_w, ada_b)
    dft = _dft_constants()
    for i in range(DEPTH):
        kind = i % N_MIXERS
        j = i // N_MIXERS
        ctx_live = any(l % N_MIXERS != 0 for l in range(i + 1, DEPTH))
        mod_lat = mod_all[i, :B].reshape(B, 6, D_MODEL)
        mod_ctx = mod_all[i, B:B + 1].reshape(1, 6, D_MODEL)
        w1 = mlp_w1[i].astype(BF16)
        w2 = mlp_w2[i].astype(BF16)
        tail_lat = (norm2_g[i], w1, w2, final_g if i == DEPTH - 1 else None)
        tail_ctx = (norm2_g[i], w1, w2)
        if kind == 0:
            hp = (hy_w_in[j], hy_b_in[j], hy_conv_w[j], hy_conv_b[j], hy_f_w1[j], hy_f_b1[j],
                  hy_f_w2[j], hy_f_b2[j], hy_f_w3[j], hy_f_b3[j], hy_f_w4[j], hy_freq[j],
                  hy_bias[j], hy_w_out[j], hy_b_out[j])
            x_lat = hyena_layer(x_lat, mod_lat, norm1_g[i], *hp, dft, tail_lat)
            if ctx_live:
                x_ctx = hyena_layer(x_ctx, mod_ctx, norm1_g[i], *hp, dft, tail_ctx)
        elif kind == 1:
            x_lat, x_ctx = gla_layer(x_lat, x_ctx, mod_lat, mod_ctx, norm1_g[i], gla_w_in[j], gla_gk_w2[j],
                                     gla_gk_b[j], gla_onorm[j], gla_wo[j], tail_lat,
                                     tail_ctx if ctx_live else None)
        else:
            assert not ctx_live
            x_lat = mla_layer(x_lat, x_ctx, mod_lat, mod_ctx, norm1_g[i], mla_w_down[j], mla_qnorm[j],
                              mla_w_uq[j], mla_kvnorm[j], mla_w_ukv[j], mla_wo[j], tail_lat)
    return x_lat
```

```python
import functools
import math

import jax
import jax.numpy as jnp
import numpy as np
from jax import lax
from jax.experimental import pallas as pl
from jax.experimental.pallas import tpu as pltpu

F32 = jnp.float32
BF16 = jnp.bfloat16

D_MODEL = 1024
DEPTH = 4
GRID_W = 64
N_MIXERS = 3
NORM_EPS = 1e-6

HY_ORDER = 2
HY_EMB = 33
HY_SHORT = 3
HY_FAST_DECAY = 0.3
HY_SLOW_DECAY = 1.5
HY_TARGET = 1e-2

GLA_HEADS = 4
GLA_DK = D_MODEL // 2
GLA_DV = D_MODEL
GLA_HK = GLA_DK // GLA_HEADS
GLA_HV = GLA_DV // GLA_HEADS
GLA_GATE_RANK = 16
GLA_GATE_NORM = 16.0
GLA_CHUNK = 64

MLA_HEADS = 8
MLA_Q_RANK = 384
MLA_KV_RANK = 256
MLA_NOPE = 128
MLA_ROPE = 64
MLA_V = 128
ROPE_THETA = 10000.0

VMEM_LIMIT_BYTES = 56 * 1024 * 1024
MLA_QK_PAD = 256
SUBLANES = 8
LANES = 128
PROJ_TOKENS = 1024
TAIL_TOKENS = 512
MLA_Q_PRESCALE = (MLA_NOPE + MLA_ROPE) ** -0.5 * math.log2(math.e)


def _const_spec(shape):
    nd = len(shape)
    return pl.BlockSpec(shape, lambda *_: (0,) * nd, pipeline_mode=pl.Buffered(1))


def _norm_mod(x, g, shift, scale):
    y = x * lax.rsqrt(jnp.mean(x * x, axis=-1, keepdims=True) + NORM_EPS)
    return (y * g) * (1.0 + scale) + shift


def _layer_tail_kernel(*refs, kind, hidden_chunk, final_norm):
    if kind == "hyena":
        y_ref, wo_ref, bo_ref = refs[:3]
        rest = refs[3:]
        y = lax.dot_general(y_ref[0], wo_ref[...], (((0,), (0,)), ((), ())),
                            preferred_element_type=F32) + bo_ref[...]
    elif kind == "gla":
        o_ref_in, og_ref, on_ref, wo_ref = refs[:4]
        rest = refs[4:]
        og = og_ref[0]
        parts = []
        for h in range(GLA_HEADS):
            cols = slice(h * GLA_HV, (h + 1) * GLA_HV)
            o = o_ref_in[0, :, cols]
            n = o * lax.rsqrt(jnp.mean(o * o, axis=-1, keepdims=True) + NORM_EPS) * on_ref[...]
            parts.append((n * (og[:, cols] * jax.nn.sigmoid(og[:, cols]))).astype(BF16))
        y = jnp.dot(jnp.concatenate(parts, axis=1), wo_ref[...], preferred_element_type=F32)
    else:
        a_ref, wo_ref = refs[:2]
        rest = refs[2:]
        y = jnp.dot(a_ref[0], wo_ref[...], preferred_element_type=F32)
    x_ref, g_ref, mod_ref, w1_ref, w2_ref, fg_ref, out_ref = rest
    x = x_ref[0] + mod_ref[0, 2:3, :] * y
    h = _norm_mod(x, g_ref[...], mod_ref[0, 3:4, :], mod_ref[0, 4:5, :]).astype(BF16)
    hidden = w1_ref.shape[1]
    acc = jnp.zeros(x.shape, F32)
    for c0 in range(0, hidden, hidden_chunk):
        a = jnp.dot(h, w1_ref[:, c0:c0 + hidden_chunk], preferred_element_type=F32)
        a = jnp.square(jnp.maximum(a, 0.0)).astype(BF16)
        acc = acc + jnp.dot(a, w2_ref[c0:c0 + hidden_chunk, :], preferred_element_type=F32)
    out = x + mod_ref[0, 5:6, :] * acc
    if final_norm:
        out = (out * lax.rsqrt(jnp.mean(out * out, axis=-1, keepdims=True) + NORM_EPS)) * fg_ref[...]
    out_ref[0] = out


def layer_tail(kind, mixer_args, x, mod, norm_g, w1, w2, final_g=None):
    B, L, D = x.shape
    tm = min(TAIL_TOKENS, L)
    per_batch = mod.shape[0] != 1
    final_norm = final_g is not None
    fg = (final_g if final_norm else norm_g).reshape(1, D)
    tok = lambda n: pl.BlockSpec((1, tm, n), lambda b, i: (b, i, 0))
    if kind == "hyena":
        y_t, w_o, b_o = mixer_args
        m_specs = [pl.BlockSpec((1, D, tm), lambda b, i: (b, 0, i)), _const_spec(w_o.shape), _const_spec((1, D))]
        m_args = [y_t, w_o, b_o.reshape(1, D)]
    elif kind == "gla":
        o, og, onorm, w_o = mixer_args
        m_specs = [tok(D), tok(D), _const_spec((1, GLA_HV)), _const_spec(w_o.shape)]
        m_args = [o, og, onorm.reshape(1, GLA_HV), w_o]
    else:
        a, w_o = mixer_args
        m_specs = [tok(a.shape[2]), _const_spec(w_o.shape)]
        m_args = [a, w_o]
    kern = functools.partial(_layer_tail_kernel, kind=kind, hidden_chunk=1024, final_norm=final_norm)
    return pl.pallas_call(
        kern,
        grid=(B, L // tm),
        in_specs=m_specs + [
            tok(D),
            _const_spec((1, D)),
            pl.BlockSpec((1, 6, D), (lambda b, i: (b, 0, 0)) if per_batch else (lambda b, i: (0, 0, 0))),
            _const_spec(w1.shape),
            _const_spec(w2.shape),
            _const_spec((1, D)),
        ],
        out_specs=tok(D),
        out_shape=jax.ShapeDtypeStruct((B, L, D), F32),
        compiler_params=pltpu.CompilerParams(
            dimension_semantics=("arbitrary", "arbitrary"), vmem_limit_bytes=VMEM_LIMIT_BYTES),
        name="layer_tail_" + kind,
    )(*m_args, x, norm_g.reshape(1, D), mod, w1, w2, fg)


def _attn_kernel(q_ref, kc_ref, vc_ref, k_ref, v_ref, o_ref, m_ref, acc_ref, sa_ref, sb_ref, *, sub):
    q = q_ref[0]
    m_ref[...] = jnp.full(m_ref.shape, -jnp.inf, F32)
    acc_ref[...] = jnp.zeros(acc_ref.shape, F32)
    lanes = m_ref.shape[1]
    n_sub = k_ref.shape[1] // sub

    def scores(k):
        return lax.dot_general(q, k, (((1,), (1,)), ((), ())), preferred_element_type=F32)

    def lat(ref, n):
        return ref[0, pl.ds(pl.multiple_of(n * sub, sub), sub), :]

    def accumulate(s, v):
        m_prev = m_ref[...]
        m_new = jnp.maximum(m_prev, jnp.max(s, axis=-1, keepdims=True))
        alpha = jnp.exp2(m_prev - m_new)
        ps = [jnp.exp2(s[:, t:t + lanes] - m_new) for t in range(0, s.shape[1], lanes)]
        p = jnp.concatenate(ps, axis=1).astype(BF16)
        pv = jnp.dot(p, v, preferred_element_type=F32)
        for t in range(0, acc_ref.shape[1], lanes):
            acc_ref[:, t:t + lanes] = alpha * acc_ref[:, t:t + lanes] + pv[:, t:t + lanes]
        m_ref[...] = m_new

    sa_ref[...] = scores(lat(k_ref, 0))
    accumulate(scores(kc_ref[0]), vc_ref[0])

    def pair(n):
        sb_ref[...] = scores(lat(k_ref, n + 1))
        accumulate(sa_ref[...], lat(v_ref, n))

    def body(j, carry):
        n = 2 * j
        pair(n)
        sa_ref[...] = scores(lat(k_ref, n + 2))
        accumulate(sb_ref[...], lat(v_ref, n + 1))
        return carry

    lax.fori_loop(0, n_sub // 2 - 1, body, 0)
    pair(n_sub - 2)
    accumulate(sb_ref[...], lat(v_ref, n_sub - 1))
    o_ref[0] = (acc_ref[:, 0:lanes] / acc_ref[:, lanes:2 * lanes]).astype(o_ref.dtype)


def mla_attention(q, kc, vc, k, v):
    B, L, _ = q.shape
    C = kc.shape[1]
    H = MLA_HEADS
    tq = min(2048, L)
    sub = min(512, L // 4)
    assert L % (2 * sub) == 0 and L % tq == 0
    kern = functools.partial(_attn_kernel, sub=sub)
    return pl.pallas_call(
        kern,
        grid=(B, H, L // tq),
        in_specs=[
            pl.BlockSpec((1, tq, MLA_QK_PAD), lambda b, h, i: (b, i, h)),
            pl.BlockSpec((1, C, MLA_QK_PAD), lambda b, h, i: (b, 0, h)),
            pl.BlockSpec((1, C, 2 * MLA_V), lambda b, h, i: (b, 0, h)),
            pl.BlockSpec((1, L, MLA_QK_PAD), lambda b, h, i: (b, 0, h)),
            pl.BlockSpec((1, L, 2 * MLA_V), lambda b, h, i: (b, 0, h)),
        ],
        out_specs=pl.BlockSpec((1, tq, MLA_V), lambda b, h, i: (b, i, h)),
        out_shape=jax.ShapeDtypeStruct((B, L, H * MLA_V), BF16),
        scratch_shapes=[pltpu.VMEM((tq, MLA_V), F32), pltpu.VMEM((tq, 2 * MLA_V), F32),
                        pltpu.VMEM((tq, sub), F32), pltpu.VMEM((tq, sub), F32)],
        compiler_params=pltpu.CompilerParams(
            dimension_semantics=("arbitrary", "arbitrary", "arbitrary"), vmem_limit_bytes=VMEM_LIMIT_BYTES),
        name="mla_attention",
    )(q, kc, vc, k, v)


DFT_N1 = 128
HY_CH_BLOCK = 32
HY_HEAD_UNROLL = 32
HY_TAP_ROWS = 64


def _dft_constants():
    n1 = DFT_N1
    n = n1 * n1
    idx = np.arange(n1, dtype=np.float64)
    th = 2.0 * np.pi * np.outer(idx, idx) / n1
    cos1, sin1 = np.cos(th), np.sin(th)
    tw = 2.0 * np.pi * np.outer(idx, idx) / n
    fr, fi = cos1, -sin1
    c = dict(
        e1=np.concatenate([cos1, -sin1], axis=0),
        tr=np.cos(tw), ti=-np.sin(tw),
        g=np.block([[fr, fi], [-fi, fr]]),
        gbar=np.block([[fr, -fi], [fi, fr]]),
        e2=np.concatenate([cos1, -sin1], axis=1) / n,
    )
    return {k: jnp.asarray(v, F32) for k, v in c.items()}


def _hy_in_kernel(x_ref, g_ref, mod_ref, wt_ref, b_ref, o_ref, *, row_chunk):
    h = _norm_mod(x_ref[0], g_ref[...], mod_ref[0, 0:1, :], mod_ref[0, 1:2, :]).astype(BF16)
    for r0 in range(0, wt_ref.shape[0], row_chunk):
        z = lax.dot_general(wt_ref[r0:r0 + row_chunk, :], h, (((1,), (1,)), ((), ())),
                            preferred_element_type=F32)
        o_ref[0, r0:r0 + row_chunk, :] = z + b_ref[r0:r0 + row_chunk, :]


def hyena_in_proj(x, norm_g, mod, w_in_t, b_in):
    B, L, D = x.shape
    n_out = w_in_t.shape[0]
    tm = min(PROJ_TOKENS, L)
    per_batch = mod.shape[0] != 1
    return pl.pallas_call(
        functools.partial(_hy_in_kernel, row_chunk=512),
        grid=(B, L // tm),
        in_specs=[
            pl.BlockSpec((1, tm, D), lambda b, i: (b, i, 0)),
            _const_spec((1, D)),
            pl.BlockSpec((1, 6, D), (lambda b, i: (b, 0, 0)) if per_batch else (lambda b, i: (0, 0, 0))),
            _const_spec(w_in_t.shape),
            _const_spec(b_in.shape),
        ],
        out_specs=pl.BlockSpec((1, n_out, tm), lambda b, i: (b, 0, i)),
        out_shape=jax.ShapeDtypeStruct((B, n_out, L), F32),
        compiler_params=pltpu.CompilerParams(
            dimension_semantics=("arbitrary", "arbitrary"), vmem_limit_bytes=VMEM_LIMIT_BYTES),
        name="hyena_in_proj",
    )(x, norm_g.reshape(1, D), mod, w_in_t, b_in)


def _hy_hidden_kernel(zf_ref, w1_ref, b1_ref, w2_ref, b2_ref, w3_ref, b3_ref, fr_ref, o_ref):
    fr = fr_ref[...]
    h = zf_ref[...].astype(BF16)
    for w_ref, b_ref in ((w1_ref, b1_ref), (w2_ref, b2_ref), (w3_ref, b3_ref)):
        h = jnp.sin(fr * (jnp.dot(w_ref[...], h, preferred_element_type=F32) + b_ref[...]))
        out = h
        h = h.astype(BF16)
    o_ref[...] = out


def _hy_taps_kernel(hid_ref, t_ref, w4_ref, dl_ref, o_ref, *, half):
    hid = hid_ref[...].astype(BF16)
    tf = jnp.dot(w4_ref[0, 0].astype(BF16), hid[:, :half], preferred_element_type=F32)
    tb = jnp.dot(w4_ref[0, 1].astype(BF16), hid[:, half:], preferred_element_type=F32)
    taps = jnp.concatenate([tf, tb], axis=1) * jnp.exp(-t_ref[...] * dl_ref[...])
    pos = lax.broadcasted_iota(jnp.int32, taps.shape, 1)
    taps = jnp.where(pos == half, 0.0, taps)
    o_ref[0] = (taps / jnp.sum(jnp.abs(taps), axis=1, keepdims=True)).astype(o_ref.dtype)


def hyena_filter_taps(L, f_w1, f_b1, f_w2, f_b2, f_w3, f_b3, f_w4, freq):
    width = f_w1.shape[1]
    n = 2 * L
    pos = np.arange(n)
    pos = np.where(pos <= L, np.minimum(pos, L - 1), n - pos).astype(np.float64)
    bands = (HY_EMB - 1) // 2
    t_np = (pos / (L - 1))[None, :]
    w_np = (2.0 * math.pi * pos / L)[None, :]
    f_np = np.linspace(1e-4, bands - 1, bands)[:, None]
    zf_np = np.concatenate([t_np, np.cos(f_np * w_np), -np.sin(f_np * w_np),
                            np.zeros((width - HY_EMB, n))], axis=0)
    t = jnp.asarray(t_np, F32)
    zf = jnp.asarray(zf_np, F32)
    w1t = jnp.pad(f_w1.T, ((0, 0), (0, width - HY_EMB))).astype(BF16)
    col = lambda v: v.reshape(width, 1).astype(F32)
    lane_blk = min(2048, n)
    hidden = pl.pallas_call(
        _hy_hidden_kernel,
        grid=(n // lane_blk,),
        in_specs=[pl.BlockSpec((width, lane_blk), lambda i: (0, i))] + [_const_spec((width, width)), _const_spec((width, 1))] * 3
        + [_const_spec((width, 1))],
        out_specs=pl.BlockSpec((width, lane_blk), lambda i: (0, i)),
        out_shape=jax.ShapeDtypeStruct((width, n), F32),
        name="hyena_filter_hidden",
    )(zf, w1t, col(f_b1), f_w2.T.astype(BF16), col(f_b2), f_w3.T.astype(BF16), col(f_b3), col(freq))
    max_decay = math.log(HY_TARGET) / HY_FAST_DECAY
    min_decay = math.log(HY_TARGET) / HY_SLOW_DECAY
    deltas = jnp.abs(jnp.linspace(min_decay, max_decay, D_MODEL, dtype=F32)).reshape(D_MODEL, 1)
    w4t = f_w4.T.reshape(HY_ORDER, 2, D_MODEL, width)
    rows = HY_TAP_ROWS
    return pl.pallas_call(
        functools.partial(_hy_taps_kernel, half=L),
        grid=(HY_ORDER, D_MODEL // rows),
        in_specs=[
            _const_spec((width, n)),
            _const_spec((1, n)),
            pl.BlockSpec((1, 2, rows, width), lambda o, i: (o, 0, i, 0)),
            pl.BlockSpec((rows, 1), lambda o, i: (i, 0)),
        ],
        out_specs=pl.BlockSpec((1, rows, n), lambda o, i: (o, i, 0)),
        out_shape=jax.ShapeDtypeStruct((HY_ORDER, D_MODEL, n), BF16),
        compiler_params=pltpu.CompilerParams(
            dimension_semantics=("arbitrary", "arbitrary"), vmem_limit_bytes=VMEM_LIMIT_BYTES),
        name="hyena_filter_taps",
    )(hidden, t, w4t, deltas)


def _dft_stage1(x, c, e1, tr, ti, zb_ref):
    n1 = DFT_N1
    z = jnp.dot(e1, x, preferred_element_type=F32)
    zr, zi = z[:n1], z[n1:]
    r0 = pl.multiple_of(c * n1, n1)
    zb_ref[pl.ds(r0, n1), 0:n1] = (zr * tr - zi * ti).astype(BF16)
    zb_ref[pl.ds(r0, n1), n1:2 * n1] = (zr * ti + zi * tr).astype(BF16)


def _hy_spectrum_kernel(x_ref, e1_ref, tr_ref, ti_ref, g_ref, o_ref, zb_ref):
    n_ch = x_ref.shape[0]
    e1, tr, ti = e1_ref[...], tr_ref[...], ti_ref[...]

    def stage1(c, carry):
        _dft_stage1(x_ref[c].astype(BF16), c, e1, tr, ti, zb_ref)
        return carry

    lax.fori_loop(0, n_ch, stage1, 0, unroll=8)
    xh = jnp.dot(zb_ref[...], g_ref[...], preferred_element_type=F32)
    o_ref[...] = xh.reshape(o_ref.shape).astype(o_ref.dtype)


def hyena_filter_spectrum(taps, consts):
    R = taps.shape[0]
    n1 = DFT_N1
    C = HY_CH_BLOCK
    x = taps.reshape(R, n1, n1)
    return pl.pallas_call(
        _hy_spectrum_kernel,
        grid=(R // C,),
        in_specs=[pl.BlockSpec((C, n1, n1), lambda i: (i, 0, 0)), _const_spec((2 * n1, n1)),
                  _const_spec((n1, n1)), _const_spec((n1, n1)), _const_spec((2 * n1, 2 * n1))],
        out_specs=pl.BlockSpec((C, n1, 2 * n1), lambda i: (i, 0, 0)),
        out_shape=jax.ShapeDtypeStruct((R, n1, 2 * n1), BF16),
        scratch_shapes=[pltpu.VMEM((C * n1, 2 * n1), BF16)],
        compiler_params=pltpu.CompilerParams(
            dimension_semantics=("arbitrary",), vmem_limit_bytes=VMEM_LIMIT_BYTES),
        name="hyena_filter_spectrum",
    )(x, consts["e1"].astype(BF16), consts["tr"], consts["ti"], consts["g"].astype(BF16))


def _hy_conv_kernel(cw_ref, cb_ref, sk_ref, zy_ref, zg1_ref, zg2_ref, hh_ref, e1_ref, tr_ref, ti_ref,
                    g_ref, gb_ref, e2_ref, o_ref, y_sc, zb_sc, u_sc, pad_sc, *, n_ch, d_model):
    n1 = DFT_N1
    rows = zy_ref.shape[2]
    ch0 = pl.program_id(0) * n_ch
    b_idx = lax.broadcasted_iota(jnp.int32, (rows, n1), 1)
    e1, tr, ti, e2 = e1_ref[...], tr_ref[...], ti_ref[...], e2_ref[...]

    pad = pad_sc.shape[1] - rows
    top = pad // 2
    zero_rows = jnp.zeros((top, n1), F32)

    def short_conv(z_ref, c, slot, col):
        z = z_ref[0, c]
        pad_sc[slot, 0:top, :] = zero_rows
        pad_sc[slot, top + rows:pad + rows, :] = zero_rows
        pad_sc[slot, top:top + rows, :] = z
        up = pad_sc[slot, top - 1:top - 1 + rows, :]
        down = pad_sc[slot, top + 1:top + 1 + rows, :]
        prev = pltpu.roll(jnp.where(b_idx == n1 - 1, up, z), 1, axis=1)
        nxt = pltpu.roll(jnp.where(b_idx == 0, down, z), n1 - 1, axis=1)
        n_col = 3 * d_model
        return cw_ref[col] * prev + cw_ref[n_col + col] * z + cw_ref[2 * n_col + col] * nxt + cb_ref[col]

    gate_refs = (zg1_ref, zg2_ref)

    def spectral_product(order):
        xh = jnp.dot(zb_sc[...], g_ref[...], preferred_element_type=F32)
        hh = hh_ref[order].reshape(n_ch * n1, 2 * n1).astype(F32)
        xr, xi, hr, hi = xh[:, :n1], xh[:, n1:], hh[:, :n1], hh[:, n1:]
        yh = jnp.concatenate([xr * hr - xi * hi, xr * hi + xi * hr], axis=1).astype(BF16)
        u_sc[...] = jnp.dot(yh, gb_ref[...], preferred_element_type=F32)

    def finish(c, order, slot):
        r0 = pl.multiple_of(c * n1, n1)
        ur, ui = u_sc[pl.ds(r0, n1), 0:n1], u_sc[pl.ds(r0, n1), n1:2 * n1]
        stacked = jnp.concatenate([ur * tr + ui * ti, ui * tr - ur * ti], axis=0).astype(BF16)
        conv = jnp.dot(e2, stacked, preferred_element_type=F32)
        gate = short_conv(gate_refs[order], c, slot, (order + 1) * d_model + ch0 + c)
        return gate * (conv + y_sc[c] * sk_ref[order * d_model + ch0 + c])

    n_slot = pad_sc.shape[0]

    def head(q, carry):
        for j in range(n_slot):
            c = q * n_slot + j
            y = short_conv(zy_ref, c, j, ch0 + c)
            y_sc[c] = y
            _dft_stage1(y.astype(BF16), c, e1, tr, ti, zb_sc)
        return carry

    lax.fori_loop(0, n_ch // n_slot, head, 0)
    n_grp = n_ch // n_slot
    for order in range(HY_ORDER - 1):
        spectral_product(order)

        def close_group(q, order=order):
            for j in range(n_slot):
                c = q * n_slot + j
                y_sc[c] = finish(c, order, j)

        def open_group(q):
            for j in range(n_slot):
                c = q * n_slot + j
                _dft_stage1(y_sc[c].astype(BF16), c, e1, tr, ti, zb_sc)

        close_group(0)

        def turn(q, carry):
            close_group(q)
            open_group(q - 1)
            return carry

        lax.fori_loop(1, n_grp, turn, 0)
        open_group(n_grp - 1)
    spectral_product(HY_ORDER - 1)

    def tail(q, carry):
        for j in range(n_slot):
            c = q * n_slot + j
            o_ref[0, c] = finish(c, HY_ORDER - 1, j).astype(o_ref.dtype)
        return carry

    lax.fori_loop(0, n_ch // n_slot, tail, 0)


def hyena_long_conv(z_t, spectrum, conv_w, conv_b, skip, consts):
    B, n_col, L = z_t.shape
    D = n_col // (HY_ORDER + 1)
    n1 = DFT_N1
    rows = L // n1
    assert 2 * rows == n1 and HY_ORDER == 2
    C = HY_CH_BLOCK
    nblk = D // C
    z4 = z_t.reshape(B, n_col, rows, n1)
    smem = pl.BlockSpec(memory_space=pltpu.SMEM)
    zspec = lambda off: pl.BlockSpec((1, C, rows, n1), lambda i, b: (b, i + off * nblk, 0, 0))
    kern = functools.partial(_hy_conv_kernel, n_ch=C, d_model=D)
    out = pl.pallas_call(
        kern,
        grid=(nblk, B),
        in_specs=[smem, smem, smem, zspec(0), zspec(1), zspec(2),
                  pl.BlockSpec((HY_ORDER, C, n1, 2 * n1), lambda i, b: (0, i, 0, 0)),
                  _const_spec((2 * n1, rows)), _const_spec((n1, n1)), _const_spec((n1, n1)),
                  _const_spec((2 * n1, 2 * n1)), _const_spec((2 * n1, 2 * n1)), _const_spec((rows, 2 * n1))],
        out_specs=pl.BlockSpec((1, C, rows, n1), lambda i, b: (b, i, 0, 0)),
        out_shape=jax.ShapeDtypeStruct((B, D, rows, n1), BF16),
        scratch_shapes=[pltpu.VMEM((C, rows, n1), F32),
                        pltpu.VMEM((C * n1, 2 * n1), BF16), pltpu.VMEM((C * n1, 2 * n1), F32),
                        pltpu.VMEM((HY_HEAD_UNROLL, rows + 2 * SUBLANES, n1), F32)],
        compiler_params=pltpu.CompilerParams(
            dimension_semantics=("arbitrary", "arbitrary"), vmem_limit_bytes=VMEM_LIMIT_BYTES),
        name="hyena_long_conv",
    )(conv_w.reshape(-1), conv_b.reshape(-1), skip.reshape(-1), z4, z4, z4, spectrum,
      consts["e1"][:, :rows].astype(BF16), consts["tr"], consts["ti"], consts["g"].astype(BF16),
      consts["gbar"].astype(BF16), consts["e2"][:rows].astype(BF16))
    return out.reshape(B, D, L)


def _hy_short_seq_kernel(zy_ref, zg1_ref, zg2_ref, taps_ref, par_ref, dfull_ref, dinv_ref, o_ref):
    rows, L = zy_ref.shape[1], zy_ref.shape[2]
    lane = lax.broadcasted_iota(jnp.int32, (rows, L), 1)
    par = par_ref[...]
    nfreq = dinv_ref.shape[0] // 2

    def short_conv(z, grp):
        prev = jnp.where(lane == 0, 0.0, pltpu.roll(z, 1, axis=1))
        nxt = jnp.where(lane == L - 1, 0.0, pltpu.roll(z, L - 1, axis=1))
        c = 4 * grp
        return par[:, c:c + 1] * prev + par[:, c + 1:c + 2] * z + par[:, c + 2:c + 3] * nxt + par[:, c + 3:c + 4]

    y = short_conv(zy_ref[0], 0)
    gates = (short_conv(zg1_ref[0], 1), short_conv(zg2_ref[0], 2))
    d_first = dfull_ref[0:L, :]
    for order in range(HY_ORDER):
        hh = jnp.dot(taps_ref[order].astype(BF16), dfull_ref[...], preferred_element_type=F32)
        xh = jnp.dot(y.astype(BF16), d_first, preferred_element_type=F32)
        xr, xi, hr, hi = xh[:, :nfreq], xh[:, nfreq:], hh[:, :nfreq], hh[:, nfreq:]
        yh = jnp.concatenate([xr * hr - xi * hi, xr * hi + xi * hr], axis=1).astype(BF16)
        conv = jnp.dot(yh, dinv_ref[...], preferred_element_type=F32)
        y = gates[order] * (conv + y * par[:, 12 + order:13 + order])
    o_ref[0] = y.astype(o_ref.dtype)


def hyena_short_seq_conv(z_t, taps, conv_w, conv_b, skip):
    B, n_col, L = z_t.shape
    D = n_col // (HY_ORDER + 1)
    n = 2 * L
    idx = np.arange(n, dtype=np.float64)
    ang = 2.0 * np.pi * np.outer(idx, idx) / n
    dfull = jnp.asarray(np.concatenate([np.cos(ang), -np.sin(ang)], axis=1), F32)
    dinv = jnp.asarray(np.concatenate([np.cos(ang), -np.sin(ang)], axis=0)[:, :L] / n, F32)
    cw = conv_w.reshape(HY_SHORT, HY_ORDER + 1, D)
    cb = conv_b.reshape(1, HY_ORDER + 1, D)
    par = jnp.concatenate([cw, cb], axis=0)
    par = jnp.transpose(par, (2, 1, 0)).reshape(D, 4 * (HY_ORDER + 1))
    par = jnp.concatenate([par, skip.T, jnp.zeros((D, 2), F32)], axis=1)
    rows = 256
    nblk = D // rows
    zspec = lambda off: pl.BlockSpec((1, rows, L), lambda i, b: (b, i + off * nblk, 0))
    return pl.pallas_call(
        _hy_short_seq_kernel,
        grid=(nblk, B),
        in_specs=[zspec(0), zspec(1), zspec(2),
                  pl.BlockSpec((HY_ORDER, rows, n), lambda i, b: (0, i, 0)),
                  pl.BlockSpec((rows, par.shape[1]), lambda i, b: (i, 0)),
                  _const_spec(dfull.shape), _const_spec(dinv.shape)],
        out_specs=pl.BlockSpec((1, rows, L), lambda i, b: (b, i, 0)),
        out_shape=jax.ShapeDtypeStruct((B, D, L), BF16),
        compiler_params=pltpu.CompilerParams(
            dimension_semantics=("arbitrary", "arbitrary"), vmem_limit_bytes=VMEM_LIMIT_BYTES),
        name="hyena_short_seq_conv",
    )(z_t, z_t, z_t, taps, par, dfull.astype(BF16), dinv.astype(BF16))


def hyena_layer(x, mod, norm1_g, w_in, b_in, conv_w, conv_b, f_w1, f_b1, f_w2, f_b2, f_w3, f_b3, f_w4,
                freq, bias, w_out, b_out, dft, tail):
    L = x.shape[1]
    taps = hyena_filter_taps(L, f_w1, f_b1, f_w2, f_b2, f_w3, f_b3, f_w4, freq)
    z_t = hyena_in_proj(x, norm1_g, mod, w_in.T.astype(BF16), b_in.reshape(-1, 1))
    if 2 * L == DFT_N1 * DFT_N1:
        spec = hyena_filter_spectrum(taps.reshape(HY_ORDER * D_MODEL, 2 * L), dft)
        spec = spec.reshape(HY_ORDER, D_MODEL, DFT_N1, 2 * DFT_N1)
        y_t = hyena_long_conv(z_t, spec, conv_w, conv_b, bias, dft)
    else:
        y_t = hyena_short_seq_conv(z_t, taps, conv_w, conv_b, bias)
    return layer_tail("hyena", (y_t, w_out.astype(BF16), b_out), x, mod, *tail)


def _gla_in_kernel(x_ref, g_ref, mod_ref, w_ref, wr_ref, w2_ref, gb_ref, qk_ref, v_ref, og_ref, gate_ref):
    h = _norm_mod(x_ref[0], g_ref[...], mod_ref[0, 0:1, :], mod_ref[0, 1:2, :]).astype(BF16)
    n = qk_ref.shape[2]
    qk = jnp.dot(h, w_ref[:, 0:n], preferred_element_type=F32)
    half = n // 2
    qk_ref[0, :, 0:half] = qk[:, 0:half] * (GLA_HK ** -0.5)
    qk_ref[0, :, half:n] = qk[:, half:n]
    v_ref[0] = jnp.dot(h, w_ref[:, n:2 * n], preferred_element_type=F32).astype(v_ref.dtype)
    og_ref[0] = jnp.dot(h, w_ref[:, 2 * n:3 * n], preferred_element_type=F32)
    r = jnp.dot(h, wr_ref[...], preferred_element_type=F32).astype(BF16)
    gk = jnp.dot(r, w2_ref[...], preferred_element_type=F32) + gb_ref[...]
    gate_ref[0] = -(jnp.maximum(-gk, 0.0) + jnp.log1p(jnp.exp(-jnp.abs(gk)))) * (1.0 / GLA_GATE_NORM)


def gla_in_proj(x, norm_g, mod, w_in, gk_w2, gk_b):
    B, L, D = x.shape
    tm = min(PROJ_TOKENS, L)
    per_batch = mod.shape[0] != 1
    n_main = 2 * GLA_DK + 2 * GLA_DV
    w_main = w_in[:, :n_main].astype(BF16)
    lanes = LANES
    w_r = jnp.pad(w_in[:, n_main:], ((0, 0), (0, lanes - 2 * GLA_GATE_RANK))).astype(BF16)
    w2 = jnp.zeros((lanes, 2 * GLA_DK), F32)
    w2 = w2.at[:GLA_GATE_RANK, :GLA_DK].set(gk_w2[0]).at[GLA_GATE_RANK:2 * GLA_GATE_RANK, GLA_DK:].set(gk_w2[1])
    tok = lambda n: pl.BlockSpec((1, tm, n), lambda b, i: (b, i, 0))
    n = 2 * GLA_DK
    assert GLA_DV == n
    return pl.pallas_call(
        _gla_in_kernel,
        grid=(B, L // tm),
        in_specs=[tok(D), _const_spec((1, D)),
                  pl.BlockSpec((1, 6, D), (lambda b, i: (b, 0, 0)) if per_batch else (lambda b, i: (0, 0, 0))),
                  _const_spec(w_main.shape), _const_spec(w_r.shape), _const_spec(w2.shape), _const_spec((1, n))],
        out_specs=[tok(n), tok(n), tok(n), tok(n)],
        out_shape=[jax.ShapeDtypeStruct((B, L, n), F32), jax.ShapeDtypeStruct((B, L, n), BF16),
                   jax.ShapeDtypeStruct((B, L, n), F32), jax.ShapeDtypeStruct((B, L, n), F32)],
        compiler_params=pltpu.CompilerParams(
            dimension_semantics=("arbitrary", "arbitrary"), vmem_limit_bytes=VMEM_LIMIT_BYTES),
        name="gla_in_proj",
    )(x, norm_g.reshape(1, D), mod, w_main, w_r, w2.astype(BF16), gk_b.reshape(1, n))


def _gla_scan_kernel(*refs, reverse, add_prev):
    if add_prev:
        qk_ref, v_ref, g_ref, s0_ref, prev_ref, o_ref, sfin_ref, st_ref = refs
    else:
        qk_ref, v_ref, g_ref, s0_ref, o_ref, sfin_ref, st_ref = refs
        prev_ref = None
    i = pl.program_id(1)
    C, H, dk, dv = GLA_CHUNK, GLA_HEADS, GLA_HK, GLA_HV

    @pl.when(i == 0)
    def _():
        st_ref[...] = s0_ref[0]

    r_idx = lax.broadcasted_iota(jnp.int32, (C, C), 0)
    c_idx = lax.broadcasted_iota(jnp.int32, (C, C), 1)
    keep = (r_idx <= c_idx) if reverse else (r_idx >= c_idx)
    tri = keep.astype(F32)
    n_chunks = qk_ref.shape[1] // C
    order = range(n_chunks - 1, -1, -1) if reverse else range(n_chunks)
    for ci in order:
        rows = slice(ci * C, (ci + 1) * C)
        b = jnp.dot(tri, g_ref[0, rows, :], precision=lax.Precision.HIGHEST, preferred_element_type=F32)
        b_last = b[0:1] if reverse else b[C - 1:C]
        e_pos, e_neg, e_end, dec = jnp.exp(b), jnp.exp(-b), jnp.exp(b_last - b), jnp.exp(b_last)
        for h in range(H):
            kc = slice(h * dk, (h + 1) * dk)
            vc = slice(h * dv, (h + 1) * dv)
            q = qk_ref[0, rows, kc]
            k = qk_ref[0, rows, H * dk + h * dk:H * dk + (h + 1) * dk]
            v = v_ref[0, rows, vc]
            q_t = (q * e_pos[:, kc]).astype(BF16)
            k_t = (k * e_neg[:, kc]).astype(BF16)
            k_end = (k * e_end[:, kc]).astype(BF16)
            att = lax.dot_general(q_t, k_t, (((1,), (1,)), ((), ())), preferred_element_type=F32)
            att = jnp.where(keep, att, 0.0).astype(BF16)
            st = st_ref[h]
            o = jnp.dot(att, v, preferred_element_type=F32) + lax.dot_general(
                q_t, st.astype(BF16), (((1,), (1,)), ((), ())), preferred_element_type=F32)
            st_ref[h] = st * dec[:, kc] + lax.dot_general(
                v, k_end, (((0,), (0,)), ((), ())), preferred_element_type=F32)
            if prev_ref is not None:
                o = o + prev_ref[0, rows, vc]
            o_ref[0, rows, vc] = o

    @pl.when(i == pl.num_programs(1) - 1)
    def _():
        sfin_ref[0] = st_ref[...]


def gla_scan(qk, v, gates, s0, direction, prev=None):
    B, L, _ = qk.shape
    reverse = direction == 1
    T = min(512, L)
    nT = L // T
    H, dk, dv = GLA_HEADS, GLA_HK, GLA_HV
    blk = (lambda i: nT - 1 - i) if reverse else (lambda i: i)
    tok = lambda n, col=0: pl.BlockSpec((1, T, n), lambda b, i: (b, blk(i), col))
    st_spec = pl.BlockSpec((1, H, dv, dk), lambda b, i: (b, 0, 0, 0))
    in_specs = [tok(2 * GLA_DK), tok(GLA_DV), tok(GLA_DK, direction), st_spec]
    args = [qk, v, gates, s0]
    if prev is not None:
        in_specs.append(tok(GLA_DV))
        args.append(prev)
    kern = functools.partial(_gla_scan_kernel, reverse=reverse, add_prev=prev is not None)
    return pl.pallas_call(
        kern,
        grid=(B, nT),
        in_specs=in_specs,
        out_specs=[tok(GLA_DV), st_spec],
        out_shape=[jax.ShapeDtypeStruct((B, L, GLA_DV), F32), jax.ShapeDtypeStruct((B, H, dv, dk), F32)],
        scratch_shapes=[pltpu.VMEM((H, dv, dk), F32)],
        compiler_params=pltpu.CompilerParams(
            dimension_semantics=("arbitrary", "arbitrary"), vmem_limit_bytes=VMEM_LIMIT_BYTES),
        name="gla_scan_bwd" if reverse else "gla_scan_fwd",
    )(*args)


def gla_layer(x_lat, x_ctx, mod_lat, mod_ctx, norm1_g, w_in, gk_w2, gk_b, onorm, wo, tail_lat, tail_ctx):
    B = x_lat.shape[0]
    qk_l, v_l, og_l, g_l = gla_in_proj(x_lat, norm1_g, mod_lat, w_in, gk_w2, gk_b)
    qk_c, v_c, og_c, g_c = gla_in_proj(x_ctx, norm1_g, mod_ctx, w_in, gk_w2, gk_b)
    s0 = jnp.zeros((B, GLA_HEADS, GLA_HV, GLA_HK), F32)
    oc, s_f = gla_scan(qk_c, v_c, g_c, s0, 0)
    oc, s_b = gla_scan(qk_c, v_c, g_c, s0, 1, prev=oc)
    ol, _ = gla_scan(qk_l, v_l, g_l, s_f, 0)
    ol, _ = gla_scan(qk_l, v_l, g_l, s_b, 1, prev=ol)
    wo = wo.astype(BF16)
    x_lat = layer_tail("gla", (ol, og_l, onorm, wo), x_lat, mod_lat, *tail_lat)
    if tail_ctx is not None:
        x_ctx = layer_tail("gla", (oc, og_c, onorm, wo), x_ctx, mod_ctx, *tail_ctx)
    return x_lat, x_ctx


def _rope_swap_perm():
    half = MLA_ROPE // 2
    quarter = half // 2
    p = []
    for base in (0, half):
        p += list(range(base + quarter, base + half)) + list(range(base, base + quarter))
    return np.asarray(p)


def mla_rope_tables(L, rotate):
    half = MLA_ROPE // 2
    zeros = jnp.zeros((L, MLA_ROPE), F32)
    if not rotate:
        return jnp.concatenate([jnp.ones((L, MLA_ROPE), F32), zeros], axis=1), jnp.zeros((L, 2 * MLA_ROPE), F32)
    pos = jnp.arange(L)
    inv_freq = ROPE_THETA ** (-jnp.arange(0, half, 2, dtype=F32) / half)
    ang_row = (pos // GRID_W).astype(F32)[:, None] * inv_freq[None, :]
    ang_col = (pos % GRID_W).astype(F32)[:, None] * inv_freq[None, :]
    cr, sr, cc, sc = jnp.cos(ang_row), jnp.sin(ang_row), jnp.cos(ang_col), jnp.sin(ang_col)
    cos = jnp.concatenate([cr, cr, cc, cc, zeros], axis=1)
    sin = jnp.concatenate([-sr, sr, -sc, sc, zeros], axis=1)
    return cos, sin


def _mla_qkv_kernel(x_ref, g_ref, mod_ref, wd_ref, qn_ref, wq_ref, kn_ref, wkv_ref, cos_ref, sin_ref,
                    q_ref, k_ref, v_ref):
    h = _norm_mod(x_ref[0], g_ref[...], mod_ref[0, 0:1, :], mod_ref[0, 1:2, :]).astype(BF16)
    c = jnp.dot(h, wd_ref[...], preferred_element_type=F32)
    cos, sin = cos_ref[...], sin_ref[...]
    lanes = cos.shape[1]

    def rms(a, g):
        return (a * lax.rsqrt(jnp.mean(a * a, axis=-1, keepdims=True) + NORM_EPS) * g).astype(BF16)

    def rope(tile):
        return tile * cos + pltpu.roll(tile, lanes // 2, axis=1) * sin

    cq = rms(c[:, :MLA_Q_RANK], qn_ref[...])
    ckv = rms(c[:, MLA_Q_RANK:MLA_Q_RANK + MLA_KV_RANK], kn_ref[...])
    k_rope = rope(c[:, MLA_Q_RANK + MLA_KV_RANK:]).astype(k_ref.dtype)
    q = jnp.dot(cq, wq_ref[...], preferred_element_type=F32)
    kv = jnp.dot(ckv, wkv_ref[...], preferred_element_type=F32)
    ones = jnp.ones((x_ref.shape[1], MLA_V), v_ref.dtype)
    for hd in range(MLA_HEADS):
        o = hd * MLA_QK_PAD
        q_ref[0, :, o:o + MLA_NOPE] = (q[:, o:o + MLA_NOPE] * MLA_Q_PRESCALE).astype(q_ref.dtype)
        q_ref[0, :, o + MLA_NOPE:o + MLA_QK_PAD] = (
            rope(q[:, o + MLA_NOPE:o + MLA_QK_PAD]) * MLA_Q_PRESCALE).astype(q_ref.dtype)
        k_ref[0, :, o:o + MLA_NOPE] = kv[:, o:o + MLA_NOPE].astype(k_ref.dtype)
        k_ref[0, :, o + MLA_NOPE:o + MLA_QK_PAD] = k_rope
        v_ref[0, :, 2 * hd * MLA_V:(2 * hd + 1) * MLA_V] = kv[:, o + MLA_NOPE:o + MLA_QK_PAD].astype(v_ref.dtype)
        v_ref[0, :, (2 * hd + 1) * MLA_V:(2 * hd + 2) * MLA_V] = ones


def mla_qkv_proj(x, norm_g, mod, w_down, qnorm, w_uq, kvnorm, w_ukv, rotate):
    B, L, D = x.shape
    tm = min(PROJ_TOKENS, L)
    per_batch = mod.shape[0] != 1
    perm = _rope_swap_perm()
    rope0 = MLA_Q_RANK + MLA_KV_RANK
    wd = jnp.concatenate([w_down, w_down[:, rope0:][:, perm]], axis=1).astype(BF16)
    wq = w_uq.reshape(MLA_Q_RANK, MLA_HEADS, MLA_NOPE + MLA_ROPE)
    wq = jnp.concatenate([wq, wq[:, :, MLA_NOPE:][:, :, perm]], axis=2)
    wq = wq.reshape(MLA_Q_RANK, MLA_HEADS * MLA_QK_PAD).astype(BF16)
    cos, sin = mla_rope_tables(L, rotate)
    tok = lambda n: pl.BlockSpec((1, tm, n), lambda b, i: (b, i, 0))
    nq = MLA_HEADS * MLA_QK_PAD
    nv = MLA_HEADS * 2 * MLA_V
    tab = pl.BlockSpec((tm, 2 * MLA_ROPE), lambda b, i: (i, 0))
    return pl.pallas_call(
        _mla_qkv_kernel,
        grid=(B, L // tm),
        in_specs=[tok(D), _const_spec((1, D)),
                  pl.BlockSpec((1, 6, D), (lambda b, i: (b, 0, 0)) if per_batch else (lambda b, i: (0, 0, 0))),
                  _const_spec(wd.shape), _const_spec((1, MLA_Q_RANK)), _const_spec(wq.shape),
                  _const_spec((1, MLA_KV_RANK)), _const_spec(w_ukv.shape), tab, tab],
        out_specs=[tok(nq), tok(nq), tok(nv)],
        out_shape=[jax.ShapeDtypeStruct((B, L, nq), BF16), jax.ShapeDtypeStruct((B, L, nq), BF16),
                   jax.ShapeDtypeStruct((B, L, nv), BF16)],
        compiler_params=pltpu.CompilerParams(
            dimension_semantics=("arbitrary", "arbitrary"), vmem_limit_bytes=VMEM_LIMIT_BYTES),
        name="mla_qkv_proj",
    )(x, norm_g.reshape(1, D), mod, wd, qnorm.reshape(1, -1), wq, kvnorm.reshape(1, -1),
      w_ukv.astype(BF16), cos, sin)


def mla_layer(x_lat, x_ctx, mod_lat, mod_ctx, norm1_g, w_down, qnorm, w_uq, kvnorm, w_ukv, wo, tail):
    ql, kl, vl = mla_qkv_proj(x_lat, norm1_g, mod_lat, w_down, qnorm, w_uq, kvnorm, w_ukv, True)
    _, kc, vc = mla_qkv_proj(x_ctx, norm1_g, mod_ctx, w_down, qnorm, w_uq, kvnorm, w_ukv, False)
    o = mla_attention(ql, kc, vc, kl, vl)
    return layer_tail("mla", (o, wo.astype(BF16)), x_lat, mod_lat, *tail)


def _adaln_kernel(c_ref, w_ref, b_ref, o_ref):
    cond = c_ref[...]
    s = (cond * jax.nn.sigmoid(cond)).astype(BF16)
    o_ref[0] = jnp.dot(s, w_ref[0].astype(BF16), preferred_element_type=F32) + b_ref[0]


def adaln_modulation(cond, ada_w, ada_b):
    R, D = cond.shape
    depth, _, n_out = ada_w.shape
    tn = 1024
    return pl.pallas_call(
        _adaln_kernel,
        grid=(depth, n_out // tn),
        in_specs=[_const_spec((R, D)),
                  pl.BlockSpec((1, D, tn), lambda i, j: (i, 0, j)),
                  pl.BlockSpec((1, 1, tn), lambda i, j: (i, 0, j))],
        out_specs=pl.BlockSpec((1, R, tn), lambda i, j: (i, 0, j)),
        out_shape=jax.ShapeDtypeStruct((depth, R, n_out), F32),
        compiler_params=pltpu.CompilerParams(dimension_semantics=("arbitrary", "arbitrary")),
        name="adaln_modulation",
    )(cond, ada_w, ada_b.reshape(depth, 1, n_out))


def kernel(x, c, ctx, c_ctx, ada_w, ada_b, norm1_g, norm2_g, mlp_w1, mlp_w2, final_g, hy_w_in, hy_b_in, hy_conv_w, hy_conv_b, hy_f_w1, hy_f_b1, hy_f_w2, hy_f_b2, hy_f_w3, hy_f_b3, hy_f_w4, hy_freq, hy_bias, hy_w_out, hy_b_out, gla_w_in, gla_gk_w2, gla_gk_b, gla_onorm, gla_wo, mla_w_down, mla_qnorm, mla_w_uq, mla_kvnorm, mla_w_ukv, mla_wo):
    x_lat = x
    x_ctx = ctx
    B = x.shape[0]
    cond = jnp.concatenate([c, c_ctx[None, :], jnp.zeros((SUBLANES - (B + 1) % SUBLANES, D_MODEL), F32)], axis=0)
    mod_all = adaln_modulation(cond, ada_w, ada_b)
    dft = _dft_constants()
    for i in range(DEPTH):
        kind = i % N_MIXERS
        j = i // N_MIXERS
        ctx_live = any(l % N_MIXERS != 0 for l in range(i + 1, DEPTH))
        mod_lat = mod_all[i, :B].reshape(B, 6, D_MODEL)
        mod_ctx = mod_all[i, B:B + 1].reshape(1, 6, D_MODEL)
        w1 = mlp_w1[i].astype(BF16)
        w2 = mlp_w2[i].astype(BF16)
        tail_lat = (norm2_g[i], w1, w2, final_g if i == DEPTH - 1 else None)
        tail_ctx = (norm2_g[i], w1, w2)
        if kind == 0:
            hp = (hy_w_in[j], hy_b_in[j], hy_conv_w[j], hy_conv_b[j], hy_f_w1[j], hy_f_b1[j],
                  hy_f_w2[j], hy_f_b2[j], hy_f_w3[j], hy_f_b3[j], hy_f_w4[j], hy_freq[j],
                  hy_bias[j], hy_w_out[j], hy_b_out[j])
            x_lat = hyena_layer(x_lat, mod_lat, norm1_g[i], *hp, dft, tail_lat)
            if ctx_live:
                x_ctx = hyena_layer(x_ctx, mod_ctx, norm1_g[i], *hp, dft, tail_ctx)
        elif kind == 1:
            x_lat, x_ctx = gla_layer(x_lat, x_ctx, mod_lat, mod_ctx, norm1_g[i], gla_w_in[j], gla_gk_w2[j],
                                     gla_gk_b[j], gla_onorm[j], gla_wo[j], tail_lat,
                                     tail_ctx if ctx_live else None)
        else:
            assert not ctx_live
            x_lat = mla_layer(x_lat, x_ctx, mod_lat, mod_ctx, norm1_g[i], mla_w_down[j], mla_qnorm[j],
                              mla_w_uq[j], mla_kvnorm[j], mla_w_ukv[j], mla_wo[j], tail_lat)
    return x_lat
```
